```python
import functools
import jax, jax.numpy as jnp
from jax import lax
import numpy as np

D_MODEL = 2048
BATCH = 4
SEQ = 4096
DEPTH = 2

GRID_W = 64
CTX_LEN = 256
N_EVEN = (DEPTH + 1) // 2
N_ODD = DEPTH // 2
D_FF = 5504
N_MOD = 9
EPS = 1e-6
CHUNK = 128
Q_BLOCK = 128
ROPE_BASE = 10000.0
NEG_BIG = -1e30

RET_HEADS = 4
RET_DK = 128
RET_DV = 256
MLSTM_HEADS = 4
MLSTM_DK = 128
MLSTM_DV = 256
LRU_WIDTH = 1024
LRU_BLOCKS = 8
LRU_BLOCK = LRU_WIDTH // LRU_BLOCKS
LRU_CONV = 4
LRU_C = 8.0
MLA_HEADS = 8
MLA_Q_RANK = 512
MLA_KV_RANK = 256
MLA_NOPE = 128
MLA_ROPE = 64
MLA_V = 128

AB_SPLITS = (RET_HEADS * RET_DK, RET_HEADS * RET_DK, RET_HEADS * RET_DV, RET_HEADS * RET_DV,
             MLSTM_HEADS * MLSTM_DK, MLSTM_HEADS * MLSTM_DK, MLSTM_HEADS * MLSTM_DV, MLSTM_HEADS * MLSTM_DV,
             4 * MLSTM_HEADS)
AB_IN = sum(AB_SPLITS)
AB_OUT = RET_HEADS * RET_DV + MLSTM_HEADS * MLSTM_DV
CD_SPLITS = (LRU_WIDTH, LRU_WIDTH, MLA_Q_RANK, MLA_KV_RANK, MLA_ROPE)
CD_IN = sum(CD_SPLITS)
CD_OUT = LRU_WIDTH + MLA_HEADS * MLA_V

kernel_name = "hybrid_retention_mlstm_rglru_mla_dit"

F32 = jnp.float32


def split_cols(p, sizes):
    return jnp.split(p, np.cumsum(sizes)[:-1].tolist(), axis=-1)


def split_heads(t, n_heads):
    return t.reshape(t.shape[0], t.shape[1], n_heads, -1)


def rms_norm(x, g):
    xf = x.astype(F32)
    y = xf * lax.rsqrt(jnp.mean(xf * xf, axis=-1, keepdims=True) + EPS)
    return (y * g.astype(F32)).astype(x.dtype)


def head_layernorm(y, g):
    yf = y.astype(F32)
    mu = jnp.mean(yf, axis=-1, keepdims=True)
    var = jnp.mean(jnp.square(yf - mu), axis=-1, keepdims=True)
    yn = (yf - mu) * lax.rsqrt(var + EPS)
    return yn.reshape(y.shape[0], y.shape[1], -1) * g.astype(F32)


def modulate(h, shift, scale):
    return h * (1.0 + scale) + shift


def adaln(cond, w, b):
    return jnp.split(jax.nn.silu(cond) @ w + b, N_MOD, axis=-1)


def half_ffn(x, g, shift, scale, gate, wg, wu, wd):
    h = modulate(rms_norm(x, g), shift, scale)
    return x + 0.5 * gate * ((jax.nn.silu(h @ wg) * (h @ wu)) @ wd)


def rope_rotate(x, ang):
    m = ang.shape[-1]
    cos = jnp.cos(ang)[:, None, :].astype(x.dtype)
    sin = jnp.sin(ang)[:, None, :].astype(x.dtype)
    x1, x2 = x[..., :m], x[..., m:]
    return jnp.concatenate([x1 * cos - x2 * sin, x1 * sin + x2 * cos], axis=-1)


def grid_positions(n_tokens):
    rows = n_tokens // GRID_W
    row = jnp.repeat(jnp.arange(rows, dtype=F32), GRID_W)
    col = jnp.tile(jnp.arange(GRID_W, dtype=F32), rows)
    return row, col


def axial_rope(x):
    row, col = grid_positions(x.shape[1])
    half = MLA_ROPE // 2
    freqs = ROPE_BASE ** (-jnp.arange(half // 2, dtype=F32) / (half // 2))
    xn = x[..., :MLA_NOPE]
    xr = x[..., MLA_NOPE:MLA_NOPE + half]
    xc = x[..., MLA_NOPE + half:]
    return jnp.concatenate([xn, rope_rotate(xr, row[:, None] * freqs), rope_rotate(xc, col[:, None] * freqs)], axis=-1)


def retnet_angles(n_tokens):
    freqs = ROPE_BASE ** (-jnp.arange(RET_DK // 2, dtype=F32) / (RET_DK // 2))
    return jnp.arange(n_tokens, dtype=F32)[:, None] * freqs


def dwconv_centred(x, w, b):
    k = w.shape[0]
    out = lax.conv_general_dilated(x, w[:, None, :].astype(x.dtype), window_strides=(1,),
                                   padding=[(k // 2, k - 1 - k // 2)],
                                   dimension_numbers=('NWC', 'WIO', 'NWC'),
                                   feature_group_count=x.shape[-1])
    return out + b


def bidirectional(scan_fwd, scan_bwd, ctx_f, lat_f, ctx_b, lat_b, state0):
    flip = lambda t: jnp.flip(t, axis=1)
    yc_f, st_f = scan_fwd(ctx_f, state0)
    yl_f, _ = scan_fwd(lat_f, st_f)
    yc_b, st_b = scan_bwd(tuple(flip(t) for t in ctx_b), state0)
    yl_b, _ = scan_bwd(tuple(flip(t) for t in lat_b), st_b)
    return yl_f + flip(yl_b), yc_f + flip(yc_b)


def retention_chunkwise(seq, s0, log_gamma):
    q, k, v = seq
    B, T, H, dk = q.shape
    dv = v.shape[-1]
    n = T // CHUNK
    qc = q.reshape(B, n, CHUNK, H, dk)
    kc = k.reshape(B, n, CHUNK, H, dk)
    vc = v.reshape(B, n, CHUNK, H, dv)
    pos = jnp.arange(CHUNK, dtype=F32)
    rel = pos[:, None] - pos[None, :]
    decay = jnp.where((rel >= 0)[None], jnp.exp(jnp.maximum(rel, 0.0)[None] * log_gamma[:, None, None]), 0.0)
    scores = jnp.einsum('bnqhd,bnkhd->bnhqk', qc, kc) * decay
    inner = jnp.einsum('bnhqk,bnkhe->bnqhe', scores, vc)
    zeta = jnp.exp((CHUNK - 1.0 - pos)[:, None] * log_gamma[None, :])
    u = jnp.einsum('bnchd,bnche->bnhde', kc * zeta[:, :, None], vc).astype(F32)
    gamma_chunk = jnp.exp(CHUNK * log_gamma)[None, :, None, None]

    def step(s, u_j):
        return gamma_chunk * s + u_j, s

    s_last, s_prev = lax.scan(step, s0, jnp.moveaxis(u, 1, 0))
    s_prev = jnp.moveaxis(s_prev, 0, 1)
    xi = jnp.exp((pos + 1.0)[:, None] * log_gamma[None, :])
    cross = jnp.einsum('bnqhd,bnhde->bnqhe', qc, s_prev) * xi[:, :, None]
    return (inner + cross).reshape(B, T, H, dv), s_last


def mlstm_chunkwise(seq, state):
    q, k, v, i_pre, log_f = seq
    B, T, H, dk = q.shape
    dv = v.shape[-1]
    n = T // CHUNK
    qc = (q * dk ** -0.5).reshape(B, n, CHUNK, H, dk)
    kc = k.reshape(B, n, CHUNK, H, dk)
    vc = v.reshape(B, n, CHUNK, H, dv)
    ic = i_pre.reshape(B, n, CHUNK, H)
    b = jnp.cumsum(log_f.reshape(B, n, CHUNK, H), axis=2)
    lower = jnp.tril(jnp.ones((CHUNK, CHUNK), dtype=bool))
    log_d = jnp.where(lower[None, None, :, :, None],
                      b[:, :, :, None, :] - b[:, :, None, :, :] + ic[:, :, None, :, :], -jnp.inf)
    m_intra = jnp.max(log_d, axis=3)
    b_last = b[:, :, -1]
    log_w = b_last[:, :, None] - b + ic
    m_loc = jnp.max(log_w, axis=2)
    kw = kc * jnp.exp(log_w - m_loc[:, :, None])[..., None]
    u_c = jnp.einsum('bnchd,bnche->bnhde', kw, vc).astype(F32)
    u_n = jnp.sum(kw, axis=2).astype(F32)

    def step(carry, inp):
        c_s, n_s, m_s = carry
        uc, un, ml, bl = inp
        m_new = jnp.maximum(bl + m_s, ml)
        a = jnp.exp(bl + m_s - m_new)
        g = jnp.exp(ml - m_new)
        new = (a[..., None, None] * c_s + g[..., None, None] * uc, a[..., None] * n_s + g[..., None] * un, m_new)
        return new, carry

    final, prev = lax.scan(step, state, tuple(jnp.moveaxis(t, 1, 0) for t in (u_c, u_n, m_loc, b_last)))
    c_prev, n_prev, m_prev = (jnp.moveaxis(t, 0, 1) for t in prev)
    log_inter = b + m_prev[:, :, None]
    m_t = jnp.maximum(log_inter, m_intra)
    d = jnp.exp(log_d - m_t[:, :, :, None])
    s = jnp.einsum('bnqhd,bnkhd->bnqkh', qc, kc) * d
    inter = jnp.exp(log_inter - m_t)
    num = jnp.einsum('bnqkh,bnkhe->bnqhe', s, vc) + jnp.einsum('bnqhd,bnhde->bnqhe', qc, c_prev) * inter[..., None]
    den = jnp.sum(s, axis=3) + jnp.einsum('bnqhd,bnhd->bnqh', qc, n_prev) * inter
    h = num / jnp.maximum(jnp.abs(den), jnp.exp(-m_t))[..., None]
    return h.reshape(B, T, H, dv), final


def lru_scan(seq, h0):
    a, b = seq
    acc_a, acc_b = lax.associative_scan(lambda e1, e2: (e1[0] * e2[0], e2[0] * e1[1] + e2[1]), (a, b), axis=1)
    h = acc_a * h0[:, None, :] + acc_b
    return h, h[:, -1]


def lru_coeffs(xc, wa, ba, wx, bx, lam):
    xb = xc.reshape(xc.shape[0], xc.shape[1], LRU_BLOCKS, LRU_BLOCK)
    r = jax.nn.sigmoid(jnp.einsum('btgi,gij->btgj', xb, wa.astype(F32)).reshape(xc.shape) + ba.astype(F32))
    i = jax.nn.sigmoid(jnp.einsum('btgi,gij->btgj', xb, wx.astype(F32)).reshape(xc.shape) + bx.astype(F32))
    log_a = -LRU_C * r * jax.nn.softplus(-lam.astype(F32))
    a = jnp.exp(log_a)
    return a, jnp.sqrt(-jnp.expm1(2.0 * log_a)) * (i * xc)


def attend_blocks(q, k, v):
    B, S, H, d = q.shape
    nb = S // Q_BLOCK
    qb = jnp.moveaxis(q.reshape(B, nb, Q_BLOCK, H, d), 1, 0)
    kf = k.astype(F32)
    scale = d ** -0.5

    def one(q_blk):
        s = jnp.einsum('bqhd,bkhd->bhqk', q_blk.astype(F32), kf) * scale
        p = jax.nn.softmax(s, axis=-1)
        return jnp.einsum('bhqk,bkhe->bqhe', p.astype(v.dtype), v)

    o = lax.map(one, qb)
    return jnp.moveaxis(o, 0, 1).reshape(B, S, H, v.shape[-1])


def mixer_ab(h, hc, w_in, w_out, decay_logit, ret_gn_g, gate_b, mlstm_gn_g, need_ctx):
    dt = h.dtype
    B = h.shape[0]
    lat = split_cols(h @ w_in, AB_SPLITS)
    ctx = split_cols(hc @ w_in, AB_SPLITS)
    ang = retnet_angles(h.shape[1])
    log_gamma = jax.nn.log_sigmoid(decay_logit.astype(F32))

    def ret_seq(parts, rotate):
        q = split_heads(parts[0], RET_HEADS)
        k = split_heads(parts[1], RET_HEADS)
        v = split_heads(parts[2], RET_HEADS)
        if rotate:
            q, k = rope_rotate(q, ang), rope_rotate(k, ang)
        return (q, k * RET_DK ** -0.5, v)

    r_lat, r_ctx = ret_seq(lat, True), ret_seq(ctx, False)
    s0 = jnp.zeros((B, RET_HEADS, RET_DK, RET_DV), F32)
    ret_l, ret_c = bidirectional(functools.partial(retention_chunkwise, log_gamma=log_gamma[0]),
                                 functools.partial(retention_chunkwise, log_gamma=log_gamma[1]),
                                 r_ctx, r_lat, r_ctx, r_lat, s0)

    def mlstm_seqs(parts):
        q = split_heads(parts[4], MLSTM_HEADS)
        k = split_heads(parts[5], MLSTM_HEADS)
        v = split_heads(parts[6], MLSTM_HEADS)
        g = parts[8].astype(F32).reshape(q.shape[0], q.shape[1], 4, MLSTM_HEADS) + gate_b.astype(F32)
        fwd = (q, k, v, g[:, :, 0], jax.nn.log_sigmoid(g[:, :, 1]))
        bwd = (q, k, v, g[:, :, 2], jax.nn.log_sigmoid(g[:, :, 3]))
        return fwd, bwd

    ml_f, ml_b = mlstm_seqs(lat)
    mc_f, mc_b = mlstm_seqs(ctx)
    st0 = (jnp.zeros((B, MLSTM_HEADS, MLSTM_DK, MLSTM_DV), F32),
           jnp.zeros((B, MLSTM_HEADS, MLSTM_DK), F32),
           jnp.full((B, MLSTM_HEADS), NEG_BIG, F32))
    ml_l, ml_c = bidirectional(mlstm_chunkwise, mlstm_chunkwise, mc_f, ml_f, mc_b, ml_b, st0)

    def merge(ret, ml, parts):
        ret_y = jax.nn.silu(parts[3]) * head_layernorm(ret, ret_gn_g).astype(dt)
        ml_y = jax.nn.sigmoid(parts[7]) * head_layernorm(ml, mlstm_gn_g).astype(dt)
        return jnp.concatenate([ret_y, ml_y], axis=-1) @ w_out

    y = merge(ret_l, ml_l, lat)
    yc = merge(ret_c, ml_c, ctx) if need_ctx else None
    return y, yc


def mla_qkv(cq, ckv, kr, q_norm_g, kv_norm_g, w_uq, w_uk, w_uv, qk_norm_g, rotate):
    B, T, _ = cq.shape
    q = (rms_norm(cq, q_norm_g) @ w_uq).reshape(B, T, MLA_HEADS, MLA_NOPE + MLA_ROPE)
    ckv = rms_norm(ckv, kv_norm_g)
    k_nope = (ckv @ w_uk).reshape(B, T, MLA_HEADS, MLA_NOPE)
    v = (ckv @ w_uv).reshape(B, T, MLA_HEADS, MLA_V)
    k_rope = jnp.broadcast_to(kr[:, :, None, :], (B, T, MLA_HEADS, MLA_ROPE))
    k = jnp.concatenate([k_nope, k_rope], axis=-1)
    q = rms_norm(q, qk_norm_g[0])
    k = rms_norm(k, qk_norm_g[1])
    if rotate:
        q, k = axial_rope(q), axial_rope(k)
    return q, k, v


def mixer_cd(h, hc, w_in, w_out, conv_w, conv_b, wa, ba, wx, bx, lam,
             q_norm_g, kv_norm_g, w_uq, w_uk, w_uv, qk_norm_g, need_ctx):
    dt = h.dtype
    B, S, _ = h.shape
    yb, xb, cq, ckv, kr = split_cols(h @ w_in, CD_SPLITS)
    ybc, xbc, cqc, ckvc, krc = split_cols(hc @ w_in, CD_SPLITS)
    xl = dwconv_centred(xb, conv_w, conv_b).astype(F32)
    xcx = dwconv_centred(xbc, conv_w, conv_b).astype(F32)
    lat_f = lru_coeffs(xl, wa[0], ba[0], wx[0], bx[0], lam[0])
    lat_b = lru_coeffs(xl, wa[1], ba[1], wx[1], bx[1], lam[1])
    ctx_f = lru_coeffs(xcx, wa[0], ba[0], wx[0], bx[0], lam[0])
    ctx_b = lru_coeffs(xcx, wa[1], ba[1], wx[1], bx[1], lam[1])
    h0 = jnp.zeros((B, LRU_WIDTH), F32)
    rl, rc = bidirectional(lru_scan, lru_scan, ctx_f, lat_f, ctx_b, lat_b, h0)
    q, k, v = mla_qkv(cq, ckv, kr, q_norm_g, kv_norm_g, w_uq, w_uk, w_uv, qk_norm_g, True)
    qc, kc, vc = mla_qkv(cqc, ckvc, krc, q_norm_g, kv_norm_g, w_uq, w_uk, w_uv, qk_norm_g, False)
    att = attend_blocks(q, jnp.concatenate([kc, k], axis=1), jnp.concatenate([vc, v], axis=1))
    y = jnp.concatenate([jax.nn.gelu(yb) * rl.astype(dt), att.reshape(B, S, MLA_HEADS * MLA_V)], axis=-1) @ w_out
    yc = None
    if need_ctx:
        att_c = attend_blocks(qc, kc, vc)
        yc = jnp.concatenate([jax.nn.gelu(ybc) * rc.astype(dt),
                              att_c.reshape(B, hc.shape[1], MLA_HEADS * MLA_V)], axis=-1) @ w_out
    return y, yc


def setup_inputs(seed: int = 0) -> dict:
    key = jax.random.key(seed)
    keys = iter(jax.random.split(key, 40))

    def nrm(shape, std):
        return jax.random.normal(next(keys), shape, F32) * std

    D = D_MODEL
    x = nrm((BATCH, SEQ, D), 1.0)
    c = nrm((BATCH, D), 1.0)
    ctx = nrm((BATCH, CTX_LEN, D), 1.0)
    c_ctx = nrm((D,), 1.0)
    ada_w = nrm((DEPTH, D, N_MOD * D), 0.5 * D ** -0.5)
    ada_b = nrm((DEPTH, N_MOD * D), 0.01)
    norm_g = 1.0 + nrm((DEPTH, 3, D), 0.01)
    ffn_wg = nrm((DEPTH, 2, D, D_FF), D ** -0.5)
    ffn_wu = nrm((DEPTH, 2, D, D_FF), D ** -0.5)
    ffn_wd = nrm((DEPTH, 2, D_FF, D), D_FF ** -0.5)
    ab_w_in = nrm((N_EVEN, D, AB_IN), D ** -0.5)
    ab_w_out = nrm((N_EVEN, AB_OUT, D), AB_OUT ** -0.5)
    gamma = 1.0 - 2.0 ** (-jnp.linspace(5.0, 12.0, RET_HEADS))
    ret_decay_logit = (jnp.log(gamma) - jnp.log1p(-gamma)) + nrm((N_EVEN, 2, RET_HEADS), 0.1)
    ret_gn_g = 1.0 + nrm((N_EVEN, RET_HEADS * RET_DV), 0.01)
    f_bias = jnp.linspace(3.0, 6.0, MLSTM_HEADS)
    mlstm_gate_b = jnp.stack([nrm((N_EVEN, MLSTM_HEADS), 0.1), f_bias + nrm((N_EVEN, MLSTM_HEADS), 0.1),
                              nrm((N_EVEN, MLSTM_HEADS), 0.1), f_bias + nrm((N_EVEN, MLSTM_HEADS), 0.1)], axis=1)
    mlstm_gn_g = 1.0 + nrm((N_EVEN, MLSTM_HEADS * MLSTM_DV), 0.01)
    cd_w_in = nrm((N_ODD, D, CD_IN), D ** -0.5)
    cd_w_out = nrm((N_ODD, CD_OUT, D), CD_OUT ** -0.5)
    lru_conv_w = nrm((N_ODD, LRU_CONV, LRU_WIDTH), LRU_CONV ** -0.5)
    lru_conv_b = nrm((N_ODD, LRU_WIDTH), 0.01)
    lru_wa = nrm((N_ODD, 2, LRU_BLOCKS, LRU_BLOCK, LRU_BLOCK), LRU_BLOCK ** -0.5)
    lru_ba = nrm((N_ODD, 2, LRU_WIDTH), 0.01)
    lru_wx = nrm((N_ODD, 2, LRU_BLOCKS, LRU_BLOCK, LRU_BLOCK), LRU_BLOCK ** -0.5)
    lru_bx = nrm((N_ODD, 2, LRU_WIDTH), 0.01)
    a_init = jax.random.uniform(next(keys), (N_ODD, 2, LRU_WIDTH), F32, 0.9, 0.999) ** (1.0 / LRU_C)
    lru_lambda = jnp.log(a_init) - jnp.log1p(-a_init)
    mla_q_norm_g = 1.0 + nrm((N_ODD, MLA_Q_RANK), 0.01)
    mla_kv_norm_g = 1.0 + nrm((N_ODD, MLA_KV_RANK), 0.01)
    mla_w_uq = nrm((N_ODD, MLA_Q_RANK, MLA_HEADS * (MLA_NOPE + MLA_ROPE)), MLA_Q_RANK ** -0.5)
    mla_w_uk = nrm((N_ODD, MLA_KV_RANK, MLA_HEADS * MLA_NOPE), MLA_KV_RANK ** -0.5)
    mla_w_uv = nrm((N_ODD, MLA_KV_RANK, MLA_HEADS * MLA_V), MLA_KV_RANK ** -0.5)
    mla_qk_norm_g = 1.0 + nrm((N_ODD, 2, MLA_NOPE + MLA_ROPE), 0.01)
    return {"x": x, "c": c, "ctx": ctx, "c_ctx": c_ctx, "ada_w": ada_w, "ada_b": ada_b, "norm_g": norm_g,
            "ffn_wg": ffn_wg, "ffn_wu": ffn_wu, "ffn_wd": ffn_wd, "ab_w_in": ab_w_in, "ab_w_out": ab_w_out,
            "ret_decay_logit": ret_decay_logit, "ret_gn_g": ret_gn_g, "mlstm_gate_b": mlstm_gate_b,
            "mlstm_gn_g": mlstm_gn_g, "cd_w_in": cd_w_in, "cd_w_out": cd_w_out, "lru_conv_w": lru_conv_w,
            "lru_conv_b": lru_conv_b, "lru_wa": lru_wa, "lru_ba": lru_ba, "lru_wx": lru_wx, "lru_bx": lru_bx,
            "lru_lambda": lru_lambda, "mla_q_norm_g": mla_q_norm_g, "mla_kv_norm_g": mla_kv_norm_g,
            "mla_w_uq": mla_w_uq, "mla_w_uk": mla_w_uk, "mla_w_uv": mla_w_uv, "mla_qk_norm_g": mla_qk_norm_g}


def reference(x, c, ctx, c_ctx, ada_w, ada_b, norm_g, ffn_wg, ffn_wu, ffn_wd, ab_w_in, ab_w_out,
              ret_decay_logit, ret_gn_g, mlstm_gate_b, mlstm_gn_g, cd_w_in, cd_w_out, lru_conv_w,
              lru_conv_b, lru_wa, lru_ba, lru_wx, lru_bx, lru_lambda, mla_q_norm_g, mla_kv_norm_g,
              mla_w_uq, mla_w_uk, mla_w_uv, mla_qk_norm_g):
    for l in range(DEPTH):
        need_ctx = l < DEPTH - 1
        ml = [m[:, None, :] for m in adaln(c, ada_w[l], ada_b[l])]
        mc = adaln(c_ctx, ada_w[l], ada_b[l])
        x = half_ffn(x, norm_g[l, 0], ml[0], ml[1], ml[2], ffn_wg[l, 0], ffn_wu[l, 0], ffn_wd[l, 0])
        ctx = half_ffn(ctx, norm_g[l, 0], mc[0], mc[1], mc[2], ffn_wg[l, 0], ffn_wu[l, 0], ffn_wd[l, 0])
        h = modulate(rms_norm(x, norm_g[l, 1]), ml[3], ml[4])
        hc = modulate(rms_norm(ctx, norm_g[l, 1]), mc[3], mc[4])
        j = l // 2
        if l % 2 == 0:
            y, yc = mixer_ab(h, hc, ab_w_in[j], ab_w_out[j], ret_decay_logit[j], ret_gn_g[j],
                             mlstm_gate_b[j], mlstm_gn_g[j], need_ctx)
        else:
            y, yc = mixer_cd(h, hc, cd_w_in[j], cd_w_out[j], lru_conv_w[j], lru_conv_b[j], lru_wa[j], lru_ba[j],
                             lru_wx[j], lru_bx[j], lru_lambda[j], mla_q_norm_g[j], mla_kv_norm_g[j],
                             mla_w_uq[j], mla_w_uk[j], mla_w_uv[j], mla_qk_norm_g[j], need_ctx)
        x = x + ml[5] * y
        x = half_ffn(x, norm_g[l, 2], ml[6], ml[7], ml[8], ffn_wg[l, 1], ffn_wu[l, 1], ffn_wd[l, 1])
        if need_ctx:
            ctx = ctx + mc[5] * yc
            ctx = half_ffn(ctx, norm_g[l, 2], mc[6], mc[7], mc[8], ffn_wg[l, 1], ffn_wu[l, 1], ffn_wd[l, 1])
    return x
```

```python
import functools
import math

import jax
import jax.numpy as jnp
import numpy as np
from jax import lax
from jax.experimental import pallas as pl
from jax.experimental.pallas import tpu as pltpu

F32 = jnp.float32
BF16 = jnp.bfloat16

N_MOD = 9
EPS = 1e-6
CHUNK = 128
GRID_W = 64
ROPE_BASE = 10000.0
NEG_BIG = -1e30

RET_HEADS = 4
RET_DK = 128
RET_DV = 256
MLSTM_HEADS = 4
MLSTM_DK = 128
MLSTM_DV = 256
LRU_WIDTH = 1024
LRU_BLOCKS = 8
LRU_BLOCK = LRU_WIDTH // LRU_BLOCKS
LRU_C = 8.0
MLA_HEADS = 8
MLA_Q_RANK = 512
MLA_KV_RANK = 256
MLA_NOPE = 128
MLA_ROPE = 64
MLA_V = 128
MLA_QK = MLA_NOPE + MLA_ROPE
MLA_PAD = 256

AB_SPLITS = (RET_HEADS * RET_DK, RET_HEADS * RET_DK, RET_HEADS * RET_DV, RET_HEADS * RET_DV,
             MLSTM_HEADS * MLSTM_DK, MLSTM_HEADS * MLSTM_DK, MLSTM_HEADS * MLSTM_DV, MLSTM_HEADS * MLSTM_DV,
             4 * MLSTM_HEADS)
AB_IN = sum(AB_SPLITS)
CD_SPLITS = (LRU_WIDTH, LRU_WIDTH, MLA_Q_RANK, MLA_KV_RANK, MLA_ROPE)
CD_IN = sum(CD_SPLITS)

LANE = 128
SUBLANE = 8
VMEM_LIMIT = 56 * 1024 * 1024

AB_QR, AB_KR, AB_VR, AB_GR, AB_QM, AB_KM, AB_VM, AB_GM, AB_GATES = np.cumsum((0,) + AB_SPLITS[:-1]).tolist()
AB_PAD = 6272
CD_YB, CD_XB, CD_CQ, CD_CKV, CD_KR = np.cumsum((0,) + CD_SPLITS[:-1]).tolist()
CD_PAD = 3072


def _params(sem):
    return pltpu.CompilerParams(dimension_semantics=sem, vmem_limit_bytes=VMEM_LIMIT)


def _round_up(n, m):
    return (n + m - 1) // m * m


def _sigmoid(x):
    return 1.0 / (1.0 + jnp.exp(-x))


def _log_sigmoid(x):
    return jnp.minimum(x, 0.0) - jnp.log(1.0 + jnp.exp(-jnp.abs(x)))


def _rms(x, g):
    return x * lax.rsqrt(jnp.mean(x * x, axis=-1, keepdims=True) + EPS) * g


def _adaln_kernel(c_ref, w_ref, b_ref, o_ref):
    c = c_ref[...]
    s = (c * _sigmoid(c)).astype(BF16)
    o_ref[...] = jnp.dot(s, w_ref[...].astype(BF16), preferred_element_type=F32) + b_ref[...]


def _adaln(cond, ada_w, ada_b):
    L, D, N = ada_w.shape
    tn = 1024 if N % 1024 == 0 else N
    out = pl.pallas_call(
        _adaln_kernel,
        grid=(L, N // tn),
        in_specs=[pl.BlockSpec((SUBLANE, D), lambda l, n: (0, 0)),
                  pl.BlockSpec((None, D, tn), lambda l, n: (l, 0, n)),
                  pl.BlockSpec((None, 1, tn), lambda l, n: (l, 0, n))],
        out_specs=pl.BlockSpec((None, SUBLANE, tn), lambda l, n: (l, 0, n)),
        out_shape=jax.ShapeDtypeStruct((L, SUBLANE, N), F32),
        compiler_params=_params(("arbitrary", "arbitrary")),
        name="adaln",
    )(cond, ada_w, ada_b.reshape(L, 1, N))
    return out.reshape(L, SUBLANE, N_MOD, D)


class Rows:
    def __init__(self, B, Lc, S):
        self.B, self.Lc, self.S = B, Lc, S
        self.n_ctx = B * Lc
        self.n_lat = B * S
        self.n_all = self.n_ctx + self.n_lat

    def mod_row(self, tile, tm, with_ctx):
        if not with_ctx:
            return (tile * tm) // self.S
        nct = self.n_ctx // tm
        return jnp.where(tile < nct, self.B, (jnp.maximum(tile - nct, 0) * tm) // self.S)


def _ffn_kernel(x_ref, mod_ref, g_ref, wg_ref, wu_ref, wd_ref, o_ref, h_scr, acc_scr, *, mod_base):
    f = pl.program_id(1)

    @pl.when(f == 0)
    def _():
        x = x_ref[...]
        y = _rms(x, g_ref[...])
        shift = mod_ref[mod_base:mod_base + 1, :]
        scale = mod_ref[mod_base + 1:mod_base + 2, :]
        h_scr[...] = (y * (1.0 + scale) + shift).astype(BF16)
        acc_scr[...] = jnp.zeros_like(acc_scr)

    h = h_scr[...]
    a = jnp.dot(h, wg_ref[...], preferred_element_type=F32)
    u = jnp.dot(h, wu_ref[...], preferred_element_type=F32)
    act = (a * _sigmoid(a) * u).astype(BF16)
    acc_scr[...] += jnp.dot(act, wd_ref[...], preferred_element_type=F32)

    @pl.when(f == pl.num_programs(1) - 1)
    def _():
        gate = mod_ref[mod_base + 2:mod_base + 3, :]
        o_ref[...] = x_ref[...] + 0.5 * gate * acc_scr[...]


def _ffn(x, mods, g, wg, wu, wd, rows, *, mod_base, with_ctx, tm, tf):
    n, D = x.shape
    Fp = wg.shape[1]
    kern = functools.partial(_ffn_kernel, mod_base=mod_base)
    return pl.pallas_call(
        kern,
        grid=(n // tm, Fp // tf),
        in_specs=[pl.BlockSpec((tm, D), lambda i, f: (i, 0)),
                  pl.BlockSpec((None, N_MOD, D), lambda i, f: (rows.mod_row(i, tm, with_ctx), 0, 0)),
                  pl.BlockSpec((1, D), lambda i, f: (0, 0)),
                  pl.BlockSpec((D, tf), lambda i, f: (0, f)),
                  pl.BlockSpec((D, tf), lambda i, f: (0, f)),
                  pl.BlockSpec((tf, D), lambda i, f: (f, 0))],
        out_specs=pl.BlockSpec((tm, D), lambda i, f: (i, 0)),
        out_shape=jax.ShapeDtypeStruct((n, D), F32),
        scratch_shapes=[pltpu.VMEM((tm, D), BF16), pltpu.VMEM((tm, D), F32)],
        compiler_params=_params(("arbitrary", "arbitrary")),
        name="ffn",
    )(x, mods, g.reshape(1, D), wg, wu, wd)


def _proj_kernel(x_ref, mod_ref, g_ref, w_ref, o_ref, h_scr):
    @pl.when(pl.program_id(1) == 0)
    def _():
        y = _rms(x_ref[...], g_ref[...])
        h_scr[...] = (y * (1.0 + mod_ref[4:5, :]) + mod_ref[3:4, :]).astype(BF16)

    o_ref[...] = jnp.dot(h_scr[...], w_ref[...], preferred_element_type=F32)


def _proj(x, mods, g, w, rows, *, tm, tn):
    n, D = x.shape
    Np = w.shape[1]
    return pl.pallas_call(
        _proj_kernel,
        grid=(n // tm, Np // tn),
        in_specs=[pl.BlockSpec((tm, D), lambda i, j: (i, 0)),
                  pl.BlockSpec((None, N_MOD, D), lambda i, j: (rows.mod_row(i, tm, True), 0, 0)),
                  pl.BlockSpec((1, D), lambda i, j: (0, 0)),
                  pl.BlockSpec((D, tn), lambda i, j: (0, j))],
        out_specs=pl.BlockSpec((tm, tn), lambda i, j: (i, j)),
        out_shape=jax.ShapeDtypeStruct((n, Np), F32),
        scratch_shapes=[pltpu.VMEM((tm, D), BF16)],
        compiler_params=_params(("arbitrary", "arbitrary")),
        name="proj",
    )(x, mods, g.reshape(1, D), w)


def _chunk_maps(rows):
    B = rows.B
    nc = rows.Lc // CHUNK
    nl = rows.S // CHUNK

    def seq_f(j):
        return j

    def seq_b(j):
        return jnp.where(j < nc, nc - 1 - j, nc + nl - 1 - (j - nc))

    def unit(b, s):
        return jnp.where(s < nc, b * nc + s, B * nc + b * nl + (s - nc))

    return nc + nl, seq_f, seq_b, unit


def _rope128(x, cos, sin):
    return x * cos + pltpu.roll(x, 64, axis=1) * sin


def _tri_masks():
    r = lax.broadcasted_iota(jnp.int32, (CHUNK, CHUNK), 0)
    c = lax.broadcasted_iota(jnp.int32, (CHUNK, CHUNK), 1)
    return r - c


def _ret_kernel(dl_ref, qf_ref, kf_ref, vf_ref, cf_ref, sf_ref, qb_ref, kb_ref, vb_ref, cb_ref, sb_ref,
                of_ref, ob_ref, s_scr):
    j = pl.program_id(1)

    @pl.when(j == 0)
    def _():
        s_scr[...] = jnp.zeros_like(s_scr)

    lg_all = _log_sigmoid(dl_ref[...])
    rel = _tri_masks().astype(F32)
    pos_c = lax.broadcasted_iota(jnp.int32, (CHUNK, 1), 0).astype(F32)
    dirs = ((qf_ref, kf_ref, vf_ref, cf_ref, sf_ref, of_ref, False),
            (qb_ref, kb_ref, vb_ref, cb_ref, sb_ref, ob_ref, True))
    for d, (q_ref, k_ref, v_ref, c_ref, sn_ref, o_ref, rev) in enumerate(dirs):
        cos = c_ref[...]
        sin = sn_ref[...]
        dist = -rel if rev else rel
        step = (CHUNK - 1.0 - pos_c) if rev else pos_c
        for h in range(RET_HEADS):
            r = d * RET_HEADS + h
            lg = lg_all[r:r + 1, 0:1]
            q = _rope128(q_ref[:, h * RET_DK:(h + 1) * RET_DK], cos, sin)
            k = _rope128(k_ref[:, h * RET_DK:(h + 1) * RET_DK], cos, sin) * (RET_DK ** -0.5)
            v = v_ref[:, h * RET_DV:(h + 1) * RET_DV].astype(BF16)
            qb16 = q.astype(BF16)
            decay = jnp.where(dist >= 0, jnp.exp(jnp.maximum(dist, 0.0) * lg), 0.0)
            sc = lax.dot_general(qb16, k.astype(BF16), (((1,), (1,)), ((), ())), preferred_element_type=F32)
            inner = jnp.dot((sc * decay).astype(BF16), v, preferred_element_type=F32)
            s_prev = s_scr[r]
            xi = jnp.exp((step + 1.0) * lg)
            cross = jnp.dot(qb16, s_prev.astype(BF16), preferred_element_type=F32) * xi
            o_ref[:, h * RET_DV:(h + 1) * RET_DV] = inner + cross
            zeta = jnp.exp((CHUNK - 1.0 - step) * lg)
            kz_t = jnp.transpose(k * zeta).astype(BF16)
            u = jnp.dot(kz_t, v, preferred_element_type=F32)
            s_scr[r] = jnp.exp(CHUNK * lg) * s_prev + u


def _retention(p, decay_logit, cos_t, sin_t, rows):
    n = p.shape[0]
    B = rows.B
    n_steps, seq_f, seq_b, unit = _chunk_maps(rows)
    qw = RET_HEADS * RET_DK
    vw = RET_HEADS * RET_DV
    dl = jnp.broadcast_to(decay_logit.astype(F32).reshape(2 * RET_HEADS, 1), (2 * RET_HEADS, LANE))

    def in_specs(seq):
        return [pl.BlockSpec((CHUNK, qw), lambda b, j: (unit(b, seq(j)), AB_QR // qw)),
                pl.BlockSpec((CHUNK, qw), lambda b, j: (unit(b, seq(j)), AB_KR // qw)),
                pl.BlockSpec((CHUNK, vw), lambda b, j: (unit(b, seq(j)), AB_VR // vw)),
                pl.BlockSpec((CHUNK, RET_DK), lambda b, j: (seq(j), 0)),
                pl.BlockSpec((CHUNK, RET_DK), lambda b, j: (seq(j), 0))]

    def out_spec(seq):
        return pl.BlockSpec((CHUNK, vw), lambda b, j: (unit(b, seq(j)), 0))

    return pl.pallas_call(
        _ret_kernel,
        grid=(B, n_steps),
        in_specs=[pl.BlockSpec((2 * RET_HEADS, LANE), lambda b, j: (0, 0))] + in_specs(seq_f) + in_specs(seq_b),
        out_specs=[out_spec(seq_f), out_spec(seq_b)],
        out_shape=[jax.ShapeDtypeStruct((n, vw), F32)] * 2,
        scratch_shapes=[pltpu.VMEM((2 * RET_HEADS, RET_DK, RET_DV), F32)],
        compiler_params=_params(("arbitrary", "arbitrary")),
        name="retention",
    )(dl, p, p, p, cos_t, sin_t, p, p, p, cos_t, sin_t)


def _scan_lanes(x, reverse):
    n = x.shape[-1]
    lane = lax.broadcasted_iota(jnp.int32, x.shape, 1)
    d = 1
    while d < n:
        if reverse:
            x = x + jnp.where(lane < n - d, pltpu.roll(x, n - d, axis=1), 0.0)
        else:
            x = x + jnp.where(lane >= d, pltpu.roll(x, d, axis=1), 0.0)
        d *= 2
    return x


def _mlstm_kernel(gb_ref, qf_ref, kf_ref, vf_ref, gf_ref, qb_ref, kb_ref, vb_ref, gbk_ref,
                  of_ref, ob_ref, c_scr, m_scr):
    j = pl.program_id(1)
    H = MLSTM_HEADS
    dv = MLSTM_DV
    ext = dv + LANE

    @pl.when(j == 0)
    def _():
        c_scr[...] = jnp.zeros_like(c_scr)
        m_scr[...] = jnp.full_like(m_scr, NEG_BIG)

    rel = _tri_masks()
    zpad = jnp.zeros((CHUNK - SUBLANE, CHUNK), F32)
    ones = jnp.ones((CHUNK, LANE), BF16)
    dirs = ((qf_ref, kf_ref, vf_ref, gf_ref, of_ref, False),
            (qb_ref, kb_ref, vb_ref, gbk_ref, ob_ref, True))
    for d, (q_ref, k_ref, v_ref, g_ref, o_ref, rev) in enumerate(dirs):
        gt = jnp.transpose(g_ref[...] + gb_ref[...])
        g8 = gt[2 * d * H:2 * d * H + SUBLANE, :]
        cs = _scan_lanes(_log_sigmoid(g8), rev)
        cs_t = jnp.transpose(jnp.concatenate([cs, zpad], axis=0))
        mask = (rel <= 0) if rev else (rel >= 0)
        for h in range(H):
            r = d * H + h
            i_row = g8[h:h + 1, :]
            b_row = cs[H + h:H + h + 1, :]
            b_col = cs_t[:, H + h:H + h + 1]
            q = (q_ref[:, h * MLSTM_DK:(h + 1) * MLSTM_DK] * (MLSTM_DK ** -0.5)).astype(BF16)
            k = k_ref[:, h * MLSTM_DK:(h + 1) * MLSTM_DK]
            v_ext = jnp.concatenate([v_ref[:, h * dv:(h + 1) * dv].astype(BF16), ones], axis=1)
            m_prev = m_scr[r:r + 1, 0:1]
            c_prev = c_scr[r]

            log_d = jnp.where(mask, b_col + (i_row - b_row), -jnp.inf)
            m_intra = jnp.max(log_d, axis=1, keepdims=True)
            log_inter = b_col + m_prev
            m_t = jnp.maximum(log_inter, m_intra)
            dmat = jnp.exp(log_d - m_t)
            s = lax.dot_general(q, k.astype(BF16), (((1,), (1,)), ((), ())), preferred_element_type=F32) * dmat
            inter = jnp.exp(log_inter - m_t)
            tot = (jnp.dot(s.astype(BF16), v_ext, preferred_element_type=F32)
                   + jnp.dot(q, c_prev.astype(BF16), preferred_element_type=F32) * inter)
            den = tot[:, dv:dv + 1]
            o_ref[:, h * dv:(h + 1) * dv] = tot[:, :dv] / jnp.maximum(jnp.abs(den), jnp.exp(-m_t))

            b_last = b_row[:, 0:1] if rev else b_row[:, CHUNK - 1:CHUNK]
            log_w = b_last - b_row + i_row
            m_loc = jnp.max(log_w, axis=1, keepdims=True)
            kw_t = (jnp.transpose(k) * jnp.exp(log_w - m_loc)).astype(BF16)
            u = jnp.dot(kw_t, v_ext, preferred_element_type=F32)
            m_new = jnp.maximum(b_last + m_prev, m_loc)
            c_scr[r] = jnp.exp(b_last + m_prev - m_new) * c_prev + jnp.exp(m_loc - m_new) * u
            m_scr[r:r + 1, :] = jnp.broadcast_to(m_new, (1, LANE))


def _mlstm(p, gate_b, rows):
    n = p.shape[0]
    B = rows.B
    H = MLSTM_HEADS
    n_steps, seq_f, seq_b, unit = _chunk_maps(rows)
    qw = H * MLSTM_DK
    vw = H * MLSTM_DV
    gb = jnp.zeros((1, LANE), F32).at[0, :4 * H].set(gate_b.astype(F32).reshape(4 * H))

    def in_specs(seq):
        return [pl.BlockSpec((CHUNK, qw), lambda b, j: (unit(b, seq(j)), AB_QM // qw)),
                pl.BlockSpec((CHUNK, qw), lambda b, j: (unit(b, seq(j)), AB_KM // qw)),
                pl.BlockSpec((CHUNK, vw), lambda b, j: (unit(b, seq(j)), AB_VM // vw)),
                pl.BlockSpec((CHUNK, LANE), lambda b, j: (unit(b, seq(j)), AB_GATES // LANE))]

    def out_spec(seq):
        return pl.BlockSpec((CHUNK, vw), lambda b, j: (unit(b, seq(j)), 0))

    return pl.pallas_call(
        _mlstm_kernel,
        grid=(B, n_steps),
        in_specs=[pl.BlockSpec((1, LANE), lambda b, j: (0, 0))] + in_specs(seq_f) + in_specs(seq_b),
        out_specs=[out_spec(seq_f), out_spec(seq_b)],
        out_shape=[jax.ShapeDtypeStruct((n, vw), F32)] * 2,
        scratch_shapes=[pltpu.VMEM((2 * H, MLSTM_DK, MLSTM_DV + LANE), F32),
                        pltpu.VMEM((2 * SUBLANE, LANE), F32)],
        compiler_params=_params(("arbitrary", "arbitrary")),
        name="mlstm",
    )(gb, p, p, p, p, p, p, p, p)


def _head_ln(y, g, heads, width):
    outs = []
    for h in range(heads):
        yh = y[:, h * width:(h + 1) * width]
        mu = jnp.mean(yh, axis=-1, keepdims=True)
        yc = yh - mu
        var = jnp.mean(yc * yc, axis=-1, keepdims=True)
        outs.append(yc * lax.rsqrt(var + EPS))
    return jnp.concatenate(outs, axis=1) * g


def _merge_ab_kernel(x_ref, mod_ref, rf_ref, rb_ref, mf_ref, mb_ref, gr_ref, gm_ref, rg_ref, mg_ref,
                     w1_ref, w2_ref, o_ref):
    gr = gr_ref[...]
    ret_y = (gr * _sigmoid(gr)) * _head_ln(rf_ref[...] + rb_ref[...], rg_ref[...], RET_HEADS, RET_DV)
    ml_y = _sigmoid(gm_ref[...]) * _head_ln(mf_ref[...] + mb_ref[...], mg_ref[...], MLSTM_HEADS, MLSTM_DV)
    y = (jnp.dot(ret_y.astype(BF16), w1_ref[...], preferred_element_type=F32)
         + jnp.dot(ml_y.astype(BF16), w2_ref[...], preferred_element_type=F32))
    o_ref[...] = x_ref[...] + mod_ref[5:6, :] * y


def _merge_ab(x, mods, ret_f, ret_b, ml_f, ml_b, p, ret_g, ml_g, w_out, rows, *, tm):
    n, D = x.shape
    rw = RET_HEADS * RET_DV
    mw = MLSTM_HEADS * MLSTM_DV
    row = lambda i: (i, 0)
    return pl.pallas_call(
        _merge_ab_kernel,
        grid=(n // tm,),
        in_specs=[pl.BlockSpec((tm, D), row),
                  pl.BlockSpec((None, N_MOD, D), lambda i: (rows.mod_row(i, tm, True), 0, 0)),
                  pl.BlockSpec((tm, rw), row), pl.BlockSpec((tm, rw), row),
                  pl.BlockSpec((tm, mw), row), pl.BlockSpec((tm, mw), row),
                  pl.BlockSpec((tm, rw), lambda i: (i, AB_GR // rw)),
                  pl.BlockSpec((tm, mw), lambda i: (i, AB_GM // mw)),
                  pl.BlockSpec((1, rw), lambda i: (0, 0)),
                  pl.BlockSpec((1, mw), lambda i: (0, 0)),
                  pl.BlockSpec((rw, D), lambda i: (0, 0)),
                  pl.BlockSpec((mw, D), lambda i: (rw // mw, 0))],
        out_specs=pl.BlockSpec((tm, D), row),
        out_shape=jax.ShapeDtypeStruct((n, D), F32),
        compiler_params=_params(("arbitrary",)),
        name="merge_ab",
    )(x, mods, ret_f, ret_b, ml_f, ml_b, p, p, ret_g.reshape(1, rw), ml_g.reshape(1, mw), w_out, w_out)


def _lru_coef_kernel(x_ref, xp_ref, xn_ref, cw_ref, cb_ref, wa_ref, wx_ref, ba_ref, bx_ref, lam_ref,
                     a_ref, b_ref, *, tm, ctx_tiles, ctx_seg, lat_seg):
    i = pl.program_id(0)
    seg_pos = jnp.where(i < ctx_tiles, i % ctx_seg, (i - ctx_tiles) % lat_seg)
    seg_len = jnp.where(i < ctx_tiles, ctx_seg, lat_seg)
    keep_prev = (seg_pos != 0).astype(F32)
    keep_next = (seg_pos != seg_len - 1).astype(F32)
    xe = jnp.concatenate([xp_ref[...] * keep_prev, x_ref[...], xn_ref[...] * keep_next], axis=0)
    ne = tm + 2 * SUBLANE
    xc = cb_ref[...] + cw_ref[2:3, :] * x_ref[...]
    for tap, off in ((0, -2), (1, -1), (3, 1)):
        shifted = pltpu.roll(xe, (-off) % ne, axis=0)[SUBLANE:SUBLANE + tm, :]
        xc = xc + cw_ref[tap:tap + 1, :] * shifted
    for d in range(2):
        lam = lam_ref[d:d + 1, :]
        sp = jnp.maximum(-lam, 0.0) + jnp.log(1.0 + jnp.exp(-jnp.abs(lam)))
        for g in range(LRU_BLOCKS):
            sl = slice(g * LRU_BLOCK, (g + 1) * LRU_BLOCK)
            xg = xc[:, sl]
            xg16 = xg.astype(BF16)
            r = _sigmoid(jnp.dot(xg16, wa_ref[d, g], preferred_element_type=F32) + ba_ref[d:d + 1, sl])
            ig = _sigmoid(jnp.dot(xg16, wx_ref[d, g], preferred_element_type=F32) + bx_ref[d:d + 1, sl])
            log_a = -LRU_C * r * sp[:, sl]
            a_ref[d, :, sl] = jnp.exp(log_a)
            b_ref[d, :, sl] = jnp.sqrt(1.0 - jnp.exp(2.0 * log_a)) * (ig * xg)


def _lru_coef(p, conv_w, conv_b, wa, wx, ba, bx, lam, rows, *, tm):
    n = p.shape[0]
    W = LRU_WIDTH
    tpb = tm // SUBLANE
    n8 = n // SUBLANE
    kern = functools.partial(_lru_coef_kernel, tm=tm, ctx_tiles=rows.n_ctx // tm,
                             ctx_seg=rows.Lc // tm, lat_seg=rows.S // tm)
    full = lambda *s: pl.BlockSpec(s, lambda i: (0,) * len(s))
    return pl.pallas_call(
        kern,
        grid=(n // tm,),
        in_specs=[pl.BlockSpec((tm, W), lambda i: (i, CD_XB // W)),
                  pl.BlockSpec((SUBLANE, W), lambda i: (jnp.maximum(i * tpb - 1, 0), CD_XB // W)),
                  pl.BlockSpec((SUBLANE, W), lambda i: (jnp.minimum((i + 1) * tpb, n8 - 1), CD_XB // W)),
                  full(4, W), full(1, W), full(2, LRU_BLOCKS, LRU_BLOCK, LRU_BLOCK),
                  full(2, LRU_BLOCKS, LRU_BLOCK, LRU_BLOCK), full(2, W), full(2, W), full(2, W)],
        out_specs=[pl.BlockSpec((2, tm, W), lambda i: (0, i, 0))] * 2,
        out_shape=[jax.ShapeDtypeStruct((2, n, W), F32)] * 2,
        compiler_params=_params(("arbitrary",)),
        name="lru_coef",
    )(p, p, p, conv_w, conv_b.reshape(1, W), wa, wx, ba, bx, lam)


def _lru_scan_kernel(af_ref, bf_ref, ab_ref, bb_ref, of_ref, ob_ref, h_scr, *, tb, lw):
    @pl.when(pl.program_id(1) == 0)
    def _():
        h_scr[...] = jnp.zeros_like(h_scr)

    row = lax.broadcasted_iota(jnp.int32, (SUBLANE, lw), 0)
    ng = tb // SUBLANE

    def scan8(a, b, rev):
        d = 1
        while d < SUBLANE:
            if rev:
                keep = row < SUBLANE - d
                sh = SUBLANE - d
            else:
                keep = row >= d
                sh = d
            a_sh = jnp.where(keep, pltpu.roll(a, sh, axis=0), 1.0)
            b_sh = jnp.where(keep, pltpu.roll(b, sh, axis=0), 0.0)
            b = a * b_sh + b
            a = a * a_sh
            d *= 2
        return a, b

    for c in range(LRU_WIDTH // lw):
        cs = slice(c * lw, (c + 1) * lw)

        def body(g, carry):
            hf, hb = carry
            rf = pl.multiple_of(g * SUBLANE, SUBLANE)
            a, b = scan8(af_ref[pl.ds(rf, SUBLANE), cs], bf_ref[pl.ds(rf, SUBLANE), cs], False)
            out = a * hf + b
            of_ref[pl.ds(rf, SUBLANE), cs] = out
            hf = jnp.broadcast_to(out[SUBLANE - 1:SUBLANE, :], (SUBLANE, lw))
            rb = pl.multiple_of((ng - 1 - g) * SUBLANE, SUBLANE)
            a, b = scan8(ab_ref[pl.ds(rb, SUBLANE), cs], bb_ref[pl.ds(rb, SUBLANE), cs], True)
            out = a * hb + b
            ob_ref[pl.ds(rb, SUBLANE), cs] = out
            hb = jnp.broadcast_to(out[0:1, :], (SUBLANE, lw))
            return hf, hb

        hf, hb = lax.fori_loop(0, ng, body, (h_scr[0, :, cs], h_scr[1, :, cs]))
        h_scr[0, :, cs] = hf
        h_scr[1, :, cs] = hb


def _lru_scan(a, b, rows):
    n = a.shape[1]
    W = LRU_WIDTH
    B = rows.B
    tb = rows.Lc
    nlb = rows.S // tb

    def blk_f(b_, j):
        return jnp.where(j == 0, b_, B + b_ * nlb + (j - 1))

    def blk_b(b_, j):
        return jnp.where(j == 0, b_, B + b_ * nlb + (nlb - j))

    def spec(d, blk):
        return pl.BlockSpec((None, tb, W), lambda b_, j: (d, blk(b_, j), 0))

    kern = functools.partial(_lru_scan_kernel, tb=tb, lw=512)
    return pl.pallas_call(
        kern,
        grid=(B, 1 + nlb),
        in_specs=[spec(0, blk_f), spec(0, blk_f), spec(1, blk_b), spec(1, blk_b)],
        out_specs=[pl.BlockSpec((tb, W), lambda b_, j: (blk_f(b_, j), 0)),
                   pl.BlockSpec((tb, W), lambda b_, j: (blk_b(b_, j), 0))],
        out_shape=[jax.ShapeDtypeStruct((n, W), F32)] * 2,
        scratch_shapes=[pltpu.VMEM((2, SUBLANE, W), F32)],
        compiler_params=_params(("arbitrary", "arbitrary")),
        name="lru_scan",
    )(a, b, a, b)


def _mla_qkv_kernel(cq_ref, ckv_ref, kr_ref, qg_ref, kvg_ref, wq_ref, wk_ref, wv_ref, qkg_ref, cos_ref, sin_ref,
                    q_ref, k_ref, v_ref):
    cqn = _rms(cq_ref[...], qg_ref[...]).astype(BF16)
    ckvn = _rms(ckv_ref[...], kvg_ref[...]).astype(BF16)
    q_all = jnp.dot(cqn, wq_ref[...], preferred_element_type=F32)
    kn_all = jnp.dot(ckvn, wk_ref[...], preferred_element_type=F32)
    v_ref[...] = jnp.dot(ckvn, wv_ref[...], preferred_element_type=F32).astype(BF16)
    kr = kr_ref[...]
    cos = cos_ref[...]
    sin = sin_ref[...]
    lane = lax.broadcasted_iota(jnp.int32, cos.shape, 1)
    first_half = (lane % (MLA_ROPE // 2)) < (MLA_ROPE // 4)

    def norm_rope(x, g):
        x = x * lax.rsqrt(jnp.sum(x * x, axis=-1, keepdims=True) * (1.0 / MLA_QK) + EPS) * g
        rot = jnp.where(first_half, pltpu.roll(x, MLA_PAD - MLA_ROPE // 4, axis=1),
                        pltpu.roll(x, MLA_ROPE // 4, axis=1))
        return x * cos + rot * sin

    for h in range(MLA_HEADS):
        sl = slice(h * MLA_PAD, (h + 1) * MLA_PAD)
        q_ref[:, sl] = (norm_rope(q_all[:, sl], qkg_ref[0:1, :]) * (MLA_QK ** -0.5)).astype(BF16)
        kh = jnp.concatenate([kn_all[:, h * MLA_NOPE:(h + 1) * MLA_NOPE], kr], axis=1)
        k_ref[:, sl] = norm_rope(kh, qkg_ref[1:2, :]).astype(BF16)


def _mla_qkv(p, q_norm_g, kv_norm_g, wq, wk, wv, qk_g, cos_t, sin_t, rows, *, tm):
    B, Lc, S = rows.B, rows.Lc, rows.S
    n = p.shape[0]
    nct = rows.n_ctx // tm
    cpb = Lc // tm
    lpb = S // tm

    def seq_blk(i):
        il = jnp.maximum(i - nct, 0)
        return (jnp.where(i < nct, i // cpb, il // lpb), jnp.where(i < nct, i % cpb, cpb + il % lpb), 0)

    def pos_blk(i):
        return (jnp.where(i < nct, i % cpb, cpb + jnp.maximum(i - nct, 0) % lpb), 0)

    full = lambda *s: pl.BlockSpec(s, lambda i: (0,) * len(s))
    qkw = MLA_HEADS * MLA_PAD
    vw = MLA_HEADS * MLA_V
    return pl.pallas_call(
        _mla_qkv_kernel,
        grid=(n // tm,),
        in_specs=[pl.BlockSpec((tm, MLA_Q_RANK), lambda i: (i, CD_CQ // MLA_Q_RANK)),
                  pl.BlockSpec((tm, MLA_KV_RANK), lambda i: (i, CD_CKV // MLA_KV_RANK)),
                  pl.BlockSpec((tm, LANE), lambda i: (i, CD_KR // LANE)),
                  full(1, MLA_Q_RANK), full(1, MLA_KV_RANK), full(MLA_Q_RANK, qkw),
                  full(MLA_KV_RANK, MLA_HEADS * MLA_NOPE), full(MLA_KV_RANK, vw), full(2, MLA_PAD),
                  pl.BlockSpec((tm, MLA_PAD), pos_blk), pl.BlockSpec((tm, MLA_PAD), pos_blk)],
        out_specs=[pl.BlockSpec((None, tm, qkw), seq_blk), pl.BlockSpec((None, tm, qkw), seq_blk),
                   pl.BlockSpec((None, tm, vw), seq_blk)],
        out_shape=[jax.ShapeDtypeStruct((B, Lc + S, qkw), BF16), jax.ShapeDtypeStruct((B, Lc + S, qkw), BF16),
                   jax.ShapeDtypeStruct((B, Lc + S, vw), BF16)],
        compiler_params=_params(("arbitrary",)),
        name="mla_qkv",
    )(p, p, p, q_norm_g.reshape(1, -1), kv_norm_g.reshape(1, -1), wq, wk, wv, qk_g, cos_t, sin_t)


def _attn_kernel(q_ref, k_ref, v_ref, o_ref):
    s = lax.dot_general(q_ref[...], k_ref[...], (((1,), (1,)), ((), ())), preferred_element_type=F32)
    m = jnp.max(s, axis=-1, keepdims=True)
    e = jnp.exp(s - m)
    l = jnp.sum(e, axis=-1, keepdims=True)
    o = jnp.dot(e.astype(BF16), v_ref[...], preferred_element_type=F32)
    o_ref[...] = (o / l).astype(BF16)


def _attention(q, k, v, rows, *, tq):
    B, Lc, S = rows.B, rows.Lc, rows.S
    nq = S // tq
    return pl.pallas_call(
        _attn_kernel,
        grid=(B, MLA_HEADS, nq),
        in_specs=[pl.BlockSpec((None, tq, MLA_PAD), lambda b, h, i: (b, Lc // tq + i, h)),
                  pl.BlockSpec((None, Lc + S, MLA_PAD), lambda b, h, i: (b, 0, h)),
                  pl.BlockSpec((None, Lc + S, MLA_V), lambda b, h, i: (b, 0, h))],
        out_specs=pl.BlockSpec((tq, MLA_V), lambda b, h, i: (b * nq + i, h)),
        out_shape=jax.ShapeDtypeStruct((B * S, MLA_HEADS * MLA_V), BF16),
        compiler_params=_params(("arbitrary", "arbitrary", "arbitrary")),
        name="attention",
    )(q, k, v)


def _gelu_tanh(x):
    return 0.5 * x * (1.0 + jnp.tanh(math.sqrt(2.0 / math.pi) * (x + 0.044715 * (x * x * x))))


def _merge_cd_kernel(x_ref, mod_ref, yb_ref, hf_ref, hb_ref, att_ref, w1_ref, w2_ref, o_ref):
    y1 = _gelu_tanh(yb_ref[...]) * (hf_ref[...] + hb_ref[...])
    y = (jnp.dot(y1.astype(BF16), w1_ref[...], preferred_element_type=F32)
         + jnp.dot(att_ref[...], w2_ref[...], preferred_element_type=F32))
    o_ref[...] = x_ref[...] + mod_ref[5:6, :] * y


def _merge_cd(x, mods, p, h_f, h_b, att, w_out, rows, *, tm):
    D = x.shape[1]
    W = LRU_WIDTH
    aw = MLA_HEADS * MLA_V
    off = rows.n_ctx // tm
    lat = lambda i: (i + off, 0)
    return pl.pallas_call(
        _merge_cd_kernel,
        grid=(rows.n_lat // tm,),
        in_specs=[pl.BlockSpec((tm, D), lat),
                  pl.BlockSpec((None, N_MOD, D), lambda i: (rows.mod_row(i, tm, False), 0, 0)),
                  pl.BlockSpec((tm, W), lambda i: (i + off, CD_YB // W)),
                  pl.BlockSpec((tm, W), lat), pl.BlockSpec((tm, W), lat),
                  pl.BlockSpec((tm, aw), lambda i: (i, 0)),
                  pl.BlockSpec((W, D), lambda i: (0, 0)),
                  pl.BlockSpec((aw, D), lambda i: (W // aw, 0))],
        out_specs=pl.BlockSpec((tm, D), lambda i: (i, 0)),
        out_shape=jax.ShapeDtypeStruct((rows.n_lat, D), F32),
        compiler_params=_params(("arbitrary",)),
        name="merge_cd",
    )(x, mods, p, h_f, h_b, att, w_out, w_out)


def _ret_tables(rows):
    half = RET_DK // 2
    freqs = ROPE_BASE ** (-jnp.arange(half, dtype=F32) / half)
    ang = jnp.arange(rows.S, dtype=F32)[:, None] * freqs
    cos = jnp.concatenate([jnp.cos(ang), jnp.cos(ang)], axis=-1)
    sin = jnp.concatenate([-jnp.sin(ang), jnp.sin(ang)], axis=-1)
    cos = jnp.concatenate([jnp.ones((rows.Lc, RET_DK), F32), cos], axis=0)
    sin = jnp.concatenate([jnp.zeros((rows.Lc, RET_DK), F32), sin], axis=0)
    return cos, sin


def _mla_tables(rows):
    S = rows.S
    quarter = MLA_ROPE // 4
    freqs = ROPE_BASE ** (-jnp.arange(quarter, dtype=F32) / quarter)
    t = jnp.arange(S)
    row = (t // GRID_W).astype(F32)
    col = (t % GRID_W).astype(F32)

    def part(pos):
        ang = pos[:, None] * freqs
        return (jnp.concatenate([jnp.cos(ang), jnp.cos(ang)], axis=-1),
                jnp.concatenate([-jnp.sin(ang), jnp.sin(ang)], axis=-1))

    rc, rs = part(row)
    cc, cs = part(col)
    tail = MLA_PAD - MLA_QK
    cos = jnp.concatenate([jnp.ones((S, MLA_NOPE), F32), rc, cc, jnp.ones((S, tail), F32)], axis=-1)
    sin = jnp.concatenate([jnp.zeros((S, MLA_NOPE), F32), rs, cs, jnp.zeros((S, tail), F32)], axis=-1)
    cos = jnp.concatenate([jnp.ones((rows.Lc, MLA_PAD), F32), cos], axis=0)
    sin = jnp.concatenate([jnp.zeros((rows.Lc, MLA_PAD), F32), sin], axis=0)
    return cos, sin


def _pad_cols(w, n):
    return jnp.pad(w, ((0, 0), (0, n - w.shape[1])))


def _head_pad(w, heads, width, padded):
    K = w.shape[0]
    return jnp.pad(w.reshape(K, heads, width), ((0, 0), (0, 0), (0, padded - width))).reshape(K, heads * padded)


def kernel(x, c, ctx, c_ctx, ada_w, ada_b, norm_g, ffn_wg, ffn_wu, ffn_wd, ab_w_in, ab_w_out, ret_decay_logit, ret_gn_g, mlstm_gate_b, mlstm_gn_g, cd_w_in, cd_w_out, lru_conv_w, lru_conv_b, lru_wa, lru_ba, lru_wx, lru_bx, lru_lambda, mla_q_norm_g, mla_kv_norm_g, mla_w_uq, mla_w_uk, mla_w_uv, mla_qk_norm_g):
    B, S, D = x.shape
    Lc = ctx.shape[1]
    depth = ada_w.shape[0]
    F = ffn_wg.shape[-1]
    rows = Rows(B, Lc, S)
    assert B < SUBLANE and Lc % CHUNK == 0 and S % Lc == 0 and S % GRID_W == 0

    tm = min(512, rows.n_ctx)
    tf = 512
    Fp = _round_up(F, tf)

    cond = jnp.zeros((SUBLANE, D), F32).at[:B].set(c.astype(F32)).at[B].set(c_ctx.astype(F32))
    mods = _adaln(cond, ada_w, ada_b)

    xa = jnp.concatenate([ctx.reshape(B * Lc, D), x.reshape(B * S, D)], axis=0).astype(F32)

    for l in range(depth):
        last = l == depth - 1
        j = l // 2
        m_l = mods[l]
        wg = [jnp.pad(ffn_wg[l, i].astype(BF16), ((0, 0), (0, Fp - F))) for i in range(2)]
        wu = [jnp.pad(ffn_wu[l, i].astype(BF16), ((0, 0), (0, Fp - F))) for i in range(2)]
        wd = [jnp.pad(ffn_wd[l, i].astype(BF16), ((0, Fp - F), (0, 0))) for i in range(2)]

        xa = _ffn(xa, m_l, norm_g[l, 0], wg[0], wu[0], wd[0], rows, mod_base=0, with_ctx=True, tm=tm, tf=tf)

        if l % 2 == 0:
            w_in = _pad_cols(ab_w_in[j].astype(BF16), AB_PAD)
            p = _proj(xa, m_l, norm_g[l, 1], w_in, rows, tm=tm, tn=AB_PAD // 7)
            cos_t, sin_t = _ret_tables(rows)
            ret_f, ret_b = _retention(p, ret_decay_logit[j], cos_t, sin_t, rows)
            ml_f, ml_b = _mlstm(p, mlstm_gate_b[j], rows)
            xa = _merge_ab(xa, m_l, ret_f, ret_b, ml_f, ml_b, p, ret_gn_g[j], mlstm_gn_g[j],
                           ab_w_out[j].astype(BF16), rows, tm=min(256, tm))
            if last:
                xa = xa[rows.n_ctx:]
        else:
            w_in = _pad_cols(cd_w_in[j].astype(BF16), CD_PAD)
            p = _proj(xa, m_l, norm_g[l, 1], w_in, rows, tm=tm, tn=CD_PAD // 3)
            a_c, b_c = _lru_coef(p, lru_conv_w[j], lru_conv_b[j], lru_wa[j].astype(BF16), lru_wx[j].astype(BF16),
                                 lru_ba[j], lru_bx[j], lru_lambda[j], rows, tm=min(256, Lc))
            h_f, h_b = _lru_scan(a_c, b_c, rows)
            cos_t, sin_t = _mla_tables(rows)
            wq = _head_pad(mla_w_uq[j].astype(BF16), MLA_HEADS, MLA_QK, MLA_PAD)
            qk_g = jnp.pad(mla_qk_norm_g[j].astype(F32), ((0, 0), (0, MLA_PAD - MLA_QK)))
            q, k, v = _mla_qkv(p, mla_q_norm_g[j], mla_kv_norm_g[j], wq, mla_w_uk[j].astype(BF16),
                               mla_w_uv[j].astype(BF16), qk_g, cos_t, sin_t, rows, tm=min(256, Lc))
            if last:
                att = _attention(q, k, v, rows, tq=256)
                xa = _merge_cd(xa, m_l, p, h_f, h_b, att, cd_w_out[j].astype(BF16), rows, tm=min(256, tm))
            else:
                raise NotImplementedError("context outputs of the recurrent/attention layer are only "
                                          "needed when it is not the last layer")

        xa = _ffn(xa, m_l, norm_g[l, 2], wg[1], wu[1], wd[1], rows, mod_base=6,
                  with_ctx=not last, tm=tm, tf=tf)

    if xa.shape[0] != rows.n_lat:
        xa = xa[rows.n_ctx:]
    return xa.reshape(B, S, D).astype(x.dtype)
```

```python
import functools
import math

import jax
import jax.numpy as jnp
import numpy as np
from jax import lax
from jax.experimental import pallas as pl
from jax.experimental.pallas import tpu as pltpu

F32 = jnp.float32
BF16 = jnp.bfloat16

N_MOD = 9
EPS = 1e-6
CHUNK = 128
GRID_W = 64
ROPE_BASE = 10000.0
NEG_BIG = -1e30

RET_HEADS = 4
RET_DK = 128
RET_DV = 256
MLSTM_HEADS = 4
MLSTM_DK = 128
MLSTM_DV = 256
LRU_WIDTH = 1024
LRU_BLOCKS = 8
LRU_BLOCK = LRU_WIDTH // LRU_BLOCKS
LRU_C = 8.0
MLA_HEADS = 8
MLA_Q_RANK = 512
MLA_KV_RANK = 256
MLA_NOPE = 128
MLA_ROPE = 64
MLA_V = 128
MLA_QK = MLA_NOPE + MLA_ROPE
LOG2E = math.log2(math.e)
MLA_PAD = 256

AB_SPLITS = (RET_HEADS * RET_DK, RET_HEADS * RET_DK, RET_HEADS * RET_DV, RET_HEADS * RET_DV,
             MLSTM_HEADS * MLSTM_DK, MLSTM_HEADS * MLSTM_DK, MLSTM_HEADS * MLSTM_DV, MLSTM_HEADS * MLSTM_DV,
             4 * MLSTM_HEADS)
AB_IN = sum(AB_SPLITS)
CD_SPLITS = (LRU_WIDTH, LRU_WIDTH, MLA_Q_RANK, MLA_KV_RANK, MLA_ROPE)
CD_IN = sum(CD_SPLITS)

LANE = 128
SUBLANE = 8
VMEM_LIMIT = 56 * 1024 * 1024
FFN_VMEM_LIMIT = 60 * 1024 * 1024

AB_QR, AB_KR, AB_VR, AB_GR, AB_QM, AB_KM, AB_VM, AB_GM, AB_GATES = np.cumsum((0,) + AB_SPLITS[:-1]).tolist()
AB_PAD = 6272
CD_YB, CD_XB, CD_CQ, CD_CKV, CD_KR = np.cumsum((0,) + CD_SPLITS[:-1]).tolist()
CD_PAD = 3072


def _params(sem, vmem=VMEM_LIMIT):
    return pltpu.CompilerParams(dimension_semantics=sem, vmem_limit_bytes=vmem)


def _round_up(n, m):
    return (n + m - 1) // m * m


def _sigmoid(x):
    return 1.0 / (1.0 + jnp.exp(-x))


def _log_sigmoid(x):
    return jnp.minimum(x, 0.0) - jnp.log(1.0 + jnp.exp(-jnp.abs(x)))


def _rms(x, g):
    return x * lax.rsqrt(jnp.mean(x * x, axis=-1, keepdims=True) + EPS) * g


def _adaln_kernel(c_ref, w_ref, b_ref, o_ref):
    c = c_ref[...]
    s = (c * _sigmoid(c)).astype(BF16)
    o_ref[...] = jnp.dot(s, w_ref[...].astype(BF16), preferred_element_type=F32) + b_ref[...]


def _adaln(cond, ada_w, ada_b):
    L, D, N = ada_w.shape
    tn = 1024 if N % 1024 == 0 else N
    out = pl.pallas_call(
        _adaln_kernel,
        grid=(L, N // tn),
        in_specs=[pl.BlockSpec((SUBLANE, D), lambda l, n: (0, 0)),
                  pl.BlockSpec((None, D, tn), lambda l, n: (l, 0, n)),
                  pl.BlockSpec((None, 1, tn), lambda l, n: (l, 0, n))],
        out_specs=pl.BlockSpec((None, SUBLANE, tn), lambda l, n: (l, 0, n)),
        out_shape=jax.ShapeDtypeStruct((L, SUBLANE, N), F32),
        compiler_params=_params(("arbitrary", "arbitrary")),
        name="adaln",
    )(cond, ada_w, ada_b.reshape(L, 1, N))
    return out.reshape(L, SUBLANE, N_MOD, D)


class Rows:
    def __init__(self, B, Lc, S):
        self.B, self.Lc, self.S = B, Lc, S
        self.n_ctx = B * Lc
        self.n_lat = B * S
        self.n_all = self.n_ctx + self.n_lat

    def mod_row(self, tile, tm, with_ctx):
        if not with_ctx:
            return (tile * tm) // self.S
        nct = self.n_ctx // tm
        return jnp.where(tile < nct, self.B, (jnp.maximum(tile - nct, 0) * tm) // self.S)


def _ffn_kernel(x_ref, mod_ref, g_ref, wg_ref, wu_ref, wd_ref, o_ref, h_scr, *, mod_base, sub):
    f = pl.program_id(1)
    tm = x_ref.shape[0]

    @pl.when(f == 0)
    def _():
        shift = mod_ref[mod_base:mod_base + 1, :]
        scale = mod_ref[mod_base + 1:mod_base + 2, :]
        for r in range(0, tm, sub):
            y = _rms(x_ref[r:r + sub, :], g_ref[...])
            h_scr[r:r + sub, :] = (y * (1.0 + scale) + shift).astype(BF16)
        o_ref[...] = jnp.zeros_like(o_ref)

    for r in range(0, tm, sub):
        h = h_scr[r:r + sub, :]
        a = jnp.dot(h, wg_ref[...], preferred_element_type=F32)
        u = jnp.dot(h, wu_ref[...], preferred_element_type=F32)
        act = (a * _sigmoid(a) * u).astype(BF16)
        o_ref[r:r + sub, :] += jnp.dot(act, wd_ref[...], preferred_element_type=F32)

    @pl.when(f == pl.num_programs(1) - 1)
    def _():
        gate = 0.5 * mod_ref[mod_base + 2:mod_base + 3, :]
        for r in range(0, tm, sub):
            o_ref[r:r + sub, :] = x_ref[r:r + sub, :] + gate * o_ref[r:r + sub, :]


def _ffn(x, mods, g, wg, wu, wd, rows, *, layer, half, mod_base, with_ctx, tm, tf):
    n, D = x.shape
    Fp = wg.shape[-1]
    kern = functools.partial(_ffn_kernel, mod_base=mod_base, sub=min(tm, 512))
    return pl.pallas_call(
        kern,
        grid=(n // tm, Fp // tf),
        in_specs=[pl.BlockSpec((tm, D), lambda i, f: (i, 0)),
                  pl.BlockSpec((None, N_MOD, D), lambda i, f: (rows.mod_row(i, tm, with_ctx), 0, 0)),
                  pl.BlockSpec((1, D), lambda i, f: (0, 0)),
                  pl.BlockSpec((None, None, D, tf), lambda i, f: (layer, half, 0, f)),
                  pl.BlockSpec((None, None, D, tf), lambda i, f: (layer, half, 0, f)),
                  pl.BlockSpec((None, None, tf, D), lambda i, f: (layer, half, f, 0))],
        out_specs=pl.BlockSpec((tm, D), lambda i, f: (i, 0)),
        out_shape=jax.ShapeDtypeStruct((n, D), F32),
        scratch_shapes=[pltpu.VMEM((tm, D), BF16)],
        compiler_params=_params(("arbitrary", "arbitrary"), FFN_VMEM_LIMIT),
        name="ffn",
    )(x, mods, g.reshape(1, D), wg, wu, wd)


def _proj_kernel(x_ref, mod_ref, g_ref, w_ref, o_ref, h_scr):
    @pl.when(pl.program_id(1) == 0)
    def _():
        y = _rms(x_ref[...], g_ref[...])
        h_scr[...] = (y * (1.0 + mod_ref[4:5, :]) + mod_ref[3:4, :]).astype(BF16)

    o_ref[...] = jnp.dot(h_scr[...], w_ref[...], preferred_element_type=F32)


def _proj(x, mods, g, w, rows, *, tm, tn):
    n, D = x.shape
    Np = w.shape[1]
    return pl.pallas_call(
        _proj_kernel,
        grid=(n // tm, Np // tn),
        in_specs=[pl.BlockSpec((tm, D), lambda i, j: (i, 0)),
                  pl.BlockSpec((None, N_MOD, D), lambda i, j: (rows.mod_row(i, tm, True), 0, 0)),
                  pl.BlockSpec((1, D), lambda i, j: (0, 0)),
                  pl.BlockSpec((D, tn), lambda i, j: (0, j))],
        out_specs=pl.BlockSpec((tm, tn), lambda i, j: (i, j)),
        out_shape=jax.ShapeDtypeStruct((n, Np), F32),
        scratch_shapes=[pltpu.VMEM((tm, D), BF16)],
        compiler_params=_params(("arbitrary", "arbitrary")),
        name="proj",
    )(x, mods, g.reshape(1, D), w)


def _chunk_maps(rows):
    B = rows.B
    nc = rows.Lc // CHUNK
    nl = rows.S // CHUNK

    def seq_f(j):
        return j

    def seq_b(j):
        return jnp.where(j < nc, nc - 1 - j, nc + nl - 1 - (j - nc))

    def unit(b, s):
        return jnp.where(s < nc, b * nc + s, B * nc + b * nl + (s - nc))

    return nc + nl, seq_f, seq_b, unit


def _rope128(x, cos, sin):
    return x * cos + pltpu.roll(x, 64, axis=1) * sin


def _tri_masks():
    r = lax.broadcasted_iota(jnp.int32, (CHUNK, CHUNK), 0)
    c = lax.broadcasted_iota(jnp.int32, (CHUNK, CHUNK), 1)
    return r - c


def _ret_kernel(dl_ref, qf_ref, kf_ref, vf_ref, cf_ref, sf_ref, qb_ref, kb_ref, vb_ref, cb_ref, sb_ref,
                of_ref, ob_ref, s_scr):
    j = pl.program_id(1)

    @pl.when(j == 0)
    def _():
        s_scr[...] = jnp.zeros_like(s_scr)

    lg_all = _log_sigmoid(dl_ref[...])
    rel = _tri_masks().astype(F32)
    pos_c = lax.broadcasted_iota(jnp.int32, (CHUNK, 1), 0).astype(F32)
    dirs = ((qf_ref, kf_ref, vf_ref, cf_ref, sf_ref, of_ref, False),
            (qb_ref, kb_ref, vb_ref, cb_ref, sb_ref, ob_ref, True))
    for d, (q_ref, k_ref, v_ref, c_ref, sn_ref, o_ref, rev) in enumerate(dirs):
        cos = c_ref[...]
        sin = sn_ref[...]
        dist = -rel if rev else rel
        step = (CHUNK - 1.0 - pos_c) if rev else pos_c
        for h in range(RET_HEADS):
            r = d * RET_HEADS + h
            lg = lg_all[r:r + 1, 0:1]
            q = _rope128(q_ref[:, h * RET_DK:(h + 1) * RET_DK], cos, sin)
            k = _rope128(k_ref[:, h * RET_DK:(h + 1) * RET_DK], cos, sin) * (RET_DK ** -0.5)
            v = v_ref[:, h * RET_DV:(h + 1) * RET_DV].astype(BF16)
            qb16 = q.astype(BF16)
            decay = jnp.where(dist >= 0, jnp.exp(jnp.maximum(dist, 0.0) * lg), 0.0)
            sc = lax.dot_general(qb16, k.astype(BF16), (((1,), (1,)), ((), ())), preferred_element_type=F32)
            inner = jnp.dot((sc * decay).astype(BF16), v, preferred_element_type=F32)
            s_prev = s_scr[r]
            xi = jnp.exp((step + 1.0) * lg)
            cross = jnp.dot(qb16, s_prev.astype(BF16), preferred_element_type=F32) * xi
            o_ref[:, h * RET_DV:(h + 1) * RET_DV] = inner + cross
            zeta = jnp.exp((CHUNK - 1.0 - step) * lg)
            kz_t = jnp.transpose(k * zeta).astype(BF16)
            u = jnp.dot(kz_t, v, preferred_element_type=F32)
            s_scr[r] = jnp.exp(CHUNK * lg) * s_prev + u


def _retention(p, decay_logit, cos_t, sin_t, rows):
    n = p.shape[0]
    B = rows.B
    n_steps, seq_f, seq_b, unit = _chunk_maps(rows)
    qw = RET_HEADS * RET_DK
    vw = RET_HEADS * RET_DV
    dl = jnp.broadcast_to(decay_logit.astype(F32).reshape(2 * RET_HEADS, 1), (2 * RET_HEADS, LANE))

    def in_specs(seq):
        return [pl.BlockSpec((CHUNK, qw), lambda b, j: (unit(b, seq(j)), AB_QR // qw)),
                pl.BlockSpec((CHUNK, qw), lambda b, j: (unit(b, seq(j)), AB_KR // qw)),
                pl.BlockSpec((CHUNK, vw), lambda b, j: (unit(b, seq(j)), AB_VR // vw)),
                pl.BlockSpec((CHUNK, RET_DK), lambda b, j: (seq(j), 0)),
                pl.BlockSpec((CHUNK, RET_DK), lambda b, j: (seq(j), 0))]

    def out_spec(seq):
        return pl.BlockSpec((CHUNK, vw), lambda b, j: (unit(b, seq(j)), 0))

    return pl.pallas_call(
        _ret_kernel,
        grid=(B, n_steps),
        in_specs=[pl.BlockSpec((2 * RET_HEADS, LANE), lambda b, j: (0, 0))] + in_specs(seq_f) + in_specs(seq_b),
        out_specs=[out_spec(seq_f), out_spec(seq_b)],
        out_shape=[jax.ShapeDtypeStruct((n, vw), F32)] * 2,
        scratch_shapes=[pltpu.VMEM((2 * RET_HEADS, RET_DK, RET_DV), F32)],
        compiler_params=_params(("arbitrary", "arbitrary")),
        name="retention",
    )(dl, p, p, p, cos_t, sin_t, p, p, p, cos_t, sin_t)


def _scan_lanes(x, reverse):
    n = x.shape[-1]
    lane = lax.broadcasted_iota(jnp.int32, x.shape, 1)
    d = 1
    while d < n:
        if reverse:
            x = x + jnp.where(lane < n - d, pltpu.roll(x, n - d, axis=1), 0.0)
        else:
            x = x + jnp.where(lane >= d, pltpu.roll(x, d, axis=1), 0.0)
        d *= 2
    return x


def _mlstm_kernel(gb_ref, qf_ref, kf_ref, vf_ref, gf_ref, qb_ref, kb_ref, vb_ref, gbk_ref,
                  of_ref, ob_ref, c_scr, m_scr):
    j = pl.program_id(1)
    H = MLSTM_HEADS
    dv = MLSTM_DV
    ext = dv + LANE

    @pl.when(j == 0)
    def _():
        c_scr[...] = jnp.zeros_like(c_scr)
        m_scr[...] = jnp.full_like(m_scr, NEG_BIG)

    rel = _tri_masks()
    zpad = jnp.zeros((CHUNK - SUBLANE, CHUNK), F32)
    ones = jnp.ones((CHUNK, LANE), BF16)
    dirs = ((qf_ref, kf_ref, vf_ref, gf_ref, of_ref, False),
            (qb_ref, kb_ref, vb_ref, gbk_ref, ob_ref, True))
    for d, (q_ref, k_ref, v_ref, g_ref, o_ref, rev) in enumerate(dirs):
        gt = jnp.transpose(g_ref[...] + gb_ref[...])
        g8 = gt[2 * d * H:2 * d * H + SUBLANE, :]
        cs = _scan_lanes(_log_sigmoid(g8), rev)
        cs_t = jnp.transpose(jnp.concatenate([cs, zpad], axis=0))
        mask = (rel <= 0) if rev else (rel >= 0)
        for h in range(H):
            r = d * H + h
            i_row = g8[h:h + 1, :]
            b_row = cs[H + h:H + h + 1, :]
            b_col = cs_t[:, H + h:H + h + 1]
            q = (q_ref[:, h * MLSTM_DK:(h + 1) * MLSTM_DK] * (MLSTM_DK ** -0.5)).astype(BF16)
            k = k_ref[:, h * MLSTM_DK:(h + 1) * MLSTM_DK]
            v_ext = jnp.concatenate([v_ref[:, h * dv:(h + 1) * dv].astype(BF16), ones], axis=1)
            m_prev = m_scr[r:r + 1, 0:1]
            c_prev = c_scr[r]

            log_d = jnp.where(mask, b_col + (i_row - b_row), -jnp.inf)
            m_intra = jnp.max(log_d, axis=1, keepdims=True)
            log_inter = b_col + m_prev
            m_t = jnp.maximum(log_inter, m_intra)
            dmat = jnp.exp(log_d - m_t)
            s = lax.dot_general(q, k.astype(BF16), (((1,), (1,)), ((), ())), preferred_element_type=F32) * dmat
            inter = jnp.exp(log_inter - m_t)
            tot = (jnp.dot(s.astype(BF16), v_ext, preferred_element_type=F32)
                   + jnp.dot(q, c_prev.astype(BF16), preferred_element_type=F32) * inter)
            den = tot[:, dv:dv + 1]
            o_ref[:, h * dv:(h + 1) * dv] = tot[:, :dv] / jnp.maximum(jnp.abs(den), jnp.exp(-m_t))

            b_last = b_row[:, 0:1] if rev else b_row[:, CHUNK - 1:CHUNK]
            log_w = b_last - b_row + i_row
            m_loc = jnp.max(log_w, axis=1, keepdims=True)
            kw_t = (jnp.transpose(k) * jnp.exp(log_w - m_loc)).astype(BF16)
            u = jnp.dot(kw_t, v_ext, preferred_element_type=F32)
            m_new = jnp.maximum(b_last + m_prev, m_loc)
            c_scr[r] = jnp.exp(b_last + m_prev - m_new) * c_prev + jnp.exp(m_loc - m_new) * u
            m_scr[r:r + 1, :] = jnp.broadcast_to(m_new, (1, LANE))


def _mlstm(p, gate_b, rows):
    n = p.shape[0]
    B = rows.B
    H = MLSTM_HEADS
    n_steps, seq_f, seq_b, unit = _chunk_maps(rows)
    qw = H * MLSTM_DK
    vw = H * MLSTM_DV
    gb = jnp.zeros((1, LANE), F32).at[0, :4 * H].set(gate_b.astype(F32).reshape(4 * H))

    def in_specs(seq):
        return [pl.BlockSpec((CHUNK, qw), lambda b, j: (unit(b, seq(j)), AB_QM // qw)),
                pl.BlockSpec((CHUNK, qw), lambda b, j: (unit(b, seq(j)), AB_KM // qw)),
                pl.BlockSpec((CHUNK, vw), lambda b, j: (unit(b, seq(j)), AB_VM // vw)),
                pl.BlockSpec((CHUNK, LANE), lambda b, j: (unit(b, seq(j)), AB_GATES // LANE))]

    def out_spec(seq):
        return pl.BlockSpec((CHUNK, vw), lambda b, j: (unit(b, seq(j)), 0))

    return pl.pallas_call(
        _mlstm_kernel,
        grid=(B, n_steps),
        in_specs=[pl.BlockSpec((1, LANE), lambda b, j: (0, 0))] + in_specs(seq_f) + in_specs(seq_b),
        out_specs=[out_spec(seq_f), out_spec(seq_b)],
        out_shape=[jax.ShapeDtypeStruct((n, vw), F32)] * 2,
        scratch_shapes=[pltpu.VMEM((2 * H, MLSTM_DK, MLSTM_DV + LANE), F32),
                        pltpu.VMEM((2 * SUBLANE, LANE), F32)],
        compiler_params=_params(("arbitrary", "arbitrary")),
        name="mlstm",
    )(gb, p, p, p, p, p, p, p, p)


def _head_ln(y, g, heads, width):
    outs = []
    for h in range(heads):
        yh = y[:, h * width:(h + 1) * width]
        mu = jnp.mean(yh, axis=-1, keepdims=True)
        yc = yh - mu
        var = jnp.mean(yc * yc, axis=-1, keepdims=True)
        outs.append(yc * lax.rsqrt(var + EPS))
    return jnp.concatenate(outs, axis=1) * g


def _merge_ab_kernel(x_ref, mod_ref, rf_ref, rb_ref, mf_ref, mb_ref, gr_ref, gm_ref, rg_ref, mg_ref,
                     w1_ref, w2_ref, o_ref):
    gr = gr_ref[...]
    ret_y = (gr * _sigmoid(gr)) * _head_ln(rf_ref[...] + rb_ref[...], rg_ref[...], RET_HEADS, RET_DV)
    ml_y = _sigmoid(gm_ref[...]) * _head_ln(mf_ref[...] + mb_ref[...], mg_ref[...], MLSTM_HEADS, MLSTM_DV)
    y = (jnp.dot(ret_y.astype(BF16), w1_ref[...], preferred_element_type=F32)
         + jnp.dot(ml_y.astype(BF16), w2_ref[...], preferred_element_type=F32))
    o_ref[...] = x_ref[...] + mod_ref[5:6, :] * y


def _merge_ab(x, mods, ret_f, ret_b, ml_f, ml_b, p, ret_g, ml_g, w_out, rows, *, tm):
    n, D = x.shape
    rw = RET_HEADS * RET_DV
    mw = MLSTM_HEADS * MLSTM_DV
    row = lambda i: (i, 0)
    return pl.pallas_call(
        _merge_ab_kernel,
        grid=(n // tm,),
        in_specs=[pl.BlockSpec((tm, D), row),
                  pl.BlockSpec((None, N_MOD, D), lambda i: (rows.mod_row(i, tm, True), 0, 0)),
                  pl.BlockSpec((tm, rw), row), pl.BlockSpec((tm, rw), row),
                  pl.BlockSpec((tm, mw), row), pl.BlockSpec((tm, mw), row),
                  pl.BlockSpec((tm, rw), lambda i: (i, AB_GR // rw)),
                  pl.BlockSpec((tm, mw), lambda i: (i, AB_GM // mw)),
                  pl.BlockSpec((1, rw), lambda i: (0, 0)),
                  pl.BlockSpec((1, mw), lambda i: (0, 0)),
                  pl.BlockSpec((rw, D), lambda i: (0, 0)),
                  pl.BlockSpec((mw, D), lambda i: (rw // mw, 0))],
        out_specs=pl.BlockSpec((tm, D), row),
        out_shape=jax.ShapeDtypeStruct((n, D), F32),
        compiler_params=_params(("arbitrary",)),
        name="merge_ab",
    )(x, mods, ret_f, ret_b, ml_f, ml_b, p, p, ret_g.reshape(1, rw), ml_g.reshape(1, mw), w_out, w_out)


def _lru_coef_kernel(x_ref, xp_ref, xn_ref, cw_ref, cb_ref, wa_ref, wx_ref, ba_ref, bx_ref, lam_ref,
                     a_ref, b_ref, *, tm, ctx_tiles, ctx_seg, lat_seg):
    i = pl.program_id(0)
    seg_pos = jnp.where(i < ctx_tiles, i % ctx_seg, (i - ctx_tiles) % lat_seg)
    seg_len = jnp.where(i < ctx_tiles, ctx_seg, lat_seg)
    keep_prev = (seg_pos != 0).astype(F32)
    keep_next = (seg_pos != seg_len - 1).astype(F32)
    xe = jnp.concatenate([xp_ref[...] * keep_prev, x_ref[...], xn_ref[...] * keep_next], axis=0)
    ne = tm + 2 * SUBLANE
    xc = cb_ref[...] + cw_ref[2:3, :] * x_ref[...]
    for tap, off in ((0, -2), (1, -1), (3, 1)):
        shifted = pltpu.roll(xe, (-off) % ne, axis=0)[SUBLANE:SUBLANE + tm, :]
        xc = xc + cw_ref[tap:tap + 1, :] * shifted
    for d in range(2):
        lam = lam_ref[d:d + 1, :]
        sp = jnp.maximum(-lam, 0.0) + jnp.log(1.0 + jnp.exp(-jnp.abs(lam)))
        for g in range(LRU_BLOCKS):
            sl = slice(g * LRU_BLOCK, (g + 1) * LRU_BLOCK)
            xg = xc[:, sl]
            xg16 = xg.astype(BF16)
            r = _sigmoid(jnp.dot(xg16, wa_ref[d, g], preferred_element_type=F32) + ba_ref[d:d + 1, sl])
            ig = _sigmoid(jnp.dot(xg16, wx_ref[d, g], preferred_element_type=F32) + bx_ref[d:d + 1, sl])
            log_a = -LRU_C * r * sp[:, sl]
            a_ref[d, :, sl] = jnp.exp(log_a)
            b_ref[d, :, sl] = jnp.sqrt(1.0 - jnp.exp(2.0 * log_a)) * (ig * xg)


def _lru_coef(p, conv_w, conv_b, wa, wx, ba, bx, lam, rows, *, tm):
    n = p.shape[0]
    W = LRU_WIDTH
    tpb = tm // SUBLANE
    n8 = n // SUBLANE
    kern = functools.partial(_lru_coef_kernel, tm=tm, ctx_tiles=rows.n_ctx // tm,
                             ctx_seg=rows.Lc // tm, lat_seg=rows.S // tm)
    full = lambda *s: pl.BlockSpec(s, lambda i: (0,) * len(s))
    return pl.pallas_call(
        kern,
        grid=(n // tm,),
        in_specs=[pl.BlockSpec((tm, W), lambda i: (i, CD_XB // W)),
                  pl.BlockSpec((SUBLANE, W), lambda i: (jnp.maximum(i * tpb - 1, 0), CD_XB // W)),
                  pl.BlockSpec((SUBLANE, W), lambda i: (jnp.minimum((i + 1) * tpb, n8 - 1), CD_XB // W)),
                  full(4, W), full(1, W), full(2, LRU_BLOCKS, LRU_BLOCK, LRU_BLOCK),
                  full(2, LRU_BLOCKS, LRU_BLOCK, LRU_BLOCK), full(2, W), full(2, W), full(2, W)],
        out_specs=[pl.BlockSpec((2, tm, W), lambda i: (0, i, 0))] * 2,
        out_shape=[jax.ShapeDtypeStruct((2, n, W), F32)] * 2,
        compiler_params=_params(("arbitrary",)),
        name="lru_coef",
    )(p, p, p, conv_w, conv_b.reshape(1, W), wa, wx, ba, bx, lam)


def _lru_scan_kernel(af_ref, bf_ref, ab_ref, bb_ref, of_ref, ob_ref, h_scr, *, tb, lw):
    @pl.when(pl.program_id(1) == 0)
    def _():
        h_scr[...] = jnp.zeros_like(h_scr)

    row = lax.broadcasted_iota(jnp.int32, (SUBLANE, lw), 0)
    ng = tb // SUBLANE

    def scan8(a, b, rev):
        d = 1
        while d < SUBLANE:
            if rev:
                keep = row < SUBLANE - d
                sh = SUBLANE - d
            else:
                keep = row >= d
                sh = d
            a_sh = jnp.where(keep, pltpu.roll(a, sh, axis=0), 1.0)
            b_sh = jnp.where(keep, pltpu.roll(b, sh, axis=0), 0.0)
            b = a * b_sh + b
            a = a * a_sh
            d *= 2
        return a, b

    for c in range(LRU_WIDTH // lw):
        cs = slice(c * lw, (c + 1) * lw)

        def body(g, carry):
            hf, hb = carry
            rf = pl.multiple_of(g * SUBLANE, SUBLANE)
            a, b = scan8(af_ref[pl.ds(rf, SUBLANE), cs], bf_ref[pl.ds(rf, SUBLANE), cs], False)
            out = a * hf + b
            of_ref[pl.ds(rf, SUBLANE), cs] = out
            hf = jnp.broadcast_to(out[SUBLANE - 1:SUBLANE, :], (SUBLANE, lw))
            rb = pl.multiple_of((ng - 1 - g) * SUBLANE, SUBLANE)
            a, b = scan8(ab_ref[pl.ds(rb, SUBLANE), cs], bb_ref[pl.ds(rb, SUBLANE), cs], True)
            out = a * hb + b
            ob_ref[pl.ds(rb, SUBLANE), cs] = out
            hb = jnp.broadcast_to(out[0:1, :], (SUBLANE, lw))
            return hf, hb

        hf, hb = lax.fori_loop(0, ng, body, (h_scr[0, :, cs], h_scr[1, :, cs]))
        h_scr[0, :, cs] = hf
        h_scr[1, :, cs] = hb


def _lru_scan(a, b, rows):
    n = a.shape[1]
    W = LRU_WIDTH
    B = rows.B
    tb = rows.Lc
    nlb = rows.S // tb

    def blk_f(b_, j):
        return jnp.where(j == 0, b_, B + b_ * nlb + (j - 1))

    def blk_b(b_, j):
        return jnp.where(j == 0, b_, B + b_ * nlb + (nlb - j))

    def spec(d, blk):
        return pl.BlockSpec((None, tb, W), lambda b_, j: (d, blk(b_, j), 0))

    kern = functools.partial(_lru_scan_kernel, tb=tb, lw=512)
    return pl.pallas_call(
        kern,
        grid=(B, 1 + nlb),
        in_specs=[spec(0, blk_f), spec(0, blk_f), spec(1, blk_b), spec(1, blk_b)],
        out_specs=[pl.BlockSpec((tb, W), lambda b_, j: (blk_f(b_, j), 0)),
                   pl.BlockSpec((tb, W), lambda b_, j: (blk_b(b_, j), 0))],
        out_shape=[jax.ShapeDtypeStruct((n, W), F32)] * 2,
        scratch_shapes=[pltpu.VMEM((2, SUBLANE, W), F32)],
        compiler_params=_params(("arbitrary", "arbitrary")),
        name="lru_scan",
    )(a, b, a, b)


def _mla_qkv_kernel(cq_ref, ckv_ref, kr_ref, qg_ref, kvg_ref, wq_ref, wk_ref, wv_ref, qkg_ref, cos_ref, sin_ref,
                    q_ref, k_ref, v_ref):
    cqn = _rms(cq_ref[...], qg_ref[...]).astype(BF16)
    ckvn = _rms(ckv_ref[...], kvg_ref[...]).astype(BF16)
    q_all = jnp.dot(cqn, wq_ref[...], preferred_element_type=F32)
    kn_all = jnp.dot(ckvn, wk_ref[...], preferred_element_type=F32)
    v_all = jnp.dot(ckvn, wv_ref[...], preferred_element_type=F32).astype(BF16)
    ones = jnp.ones((v_all.shape[0], LANE), BF16)
    for h in range(MLA_HEADS):
        v_ref[:, h * (MLA_V + LANE):(h + 1) * (MLA_V + LANE)] = jnp.concatenate(
            [v_all[:, h * MLA_V:(h + 1) * MLA_V], ones], axis=1)
    kr = kr_ref[...]
    cos = cos_ref[...]
    sin = sin_ref[...]
    lane = lax.broadcasted_iota(jnp.int32, cos.shape, 1)
    first_half = (lane % (MLA_ROPE // 2)) < (MLA_ROPE // 4)

    def norm_rope(x, g):
        x = x * lax.rsqrt(jnp.sum(x * x, axis=-1, keepdims=True) * (1.0 / MLA_QK) + EPS) * g
        rot = jnp.where(first_half, pltpu.roll(x, MLA_PAD - MLA_ROPE // 4, axis=1),
                        pltpu.roll(x, MLA_ROPE // 4, axis=1))
        return x * cos + rot * sin

    for h in range(MLA_HEADS):
        sl = slice(h * MLA_PAD, (h + 1) * MLA_PAD)
        q_ref[:, sl] = (norm_rope(q_all[:, sl], qkg_ref[0:1, :]) * (MLA_QK ** -0.5 * LOG2E)).astype(BF16)
        kh = jnp.concatenate([kn_all[:, h * MLA_NOPE:(h + 1) * MLA_NOPE], kr], axis=1)
        k_ref[:, sl] = norm_rope(kh, qkg_ref[1:2, :]).astype(BF16)


def _mla_qkv(p, q_norm_g, kv_norm_g, wq, wk, wv, qk_g, cos_t, sin_t, rows, *, tm):
    B, Lc, S = rows.B, rows.Lc, rows.S
    n = p.shape[0]
    nct = rows.n_ctx // tm
    cpb = Lc // tm
    lpb = S // tm

    def seq_blk(i):
        il = jnp.maximum(i - nct, 0)
        return (jnp.where(i < nct, i // cpb, il // lpb), jnp.where(i < nct, lpb + i % cpb, il % lpb), 0)

    def pos_blk(i):
        return (jnp.where(i < nct, i % cpb, cpb + jnp.maximum(i - nct, 0) % lpb), 0)

    full = lambda *s: pl.BlockSpec(s, lambda i: (0,) * len(s))
    qkw = MLA_HEADS * MLA_PAD
    vw = MLA_HEADS * MLA_V
    return pl.pallas_call(
        _mla_qkv_kernel,
        grid=(n // tm,),
        in_specs=[pl.BlockSpec((tm, MLA_Q_RANK), lambda i: (i, CD_CQ // MLA_Q_RANK)),
                  pl.BlockSpec((tm, MLA_KV_RANK), lambda i: (i, CD_CKV // MLA_KV_RANK)),
                  pl.BlockSpec((tm, LANE), lambda i: (i, CD_KR // LANE)),
                  full(1, MLA_Q_RANK), full(1, MLA_KV_RANK), full(MLA_Q_RANK, qkw),
                  full(MLA_KV_RANK, MLA_HEADS * MLA_NOPE), full(MLA_KV_RANK, vw), full(2, MLA_PAD),
                  pl.BlockSpec((tm, MLA_PAD), pos_blk), pl.BlockSpec((tm, MLA_PAD), pos_blk)],
        out_specs=[pl.BlockSpec((None, tm, qkw), seq_blk), pl.BlockSpec((None, tm, qkw), seq_blk),
                   pl.BlockSpec((None, tm, vw + MLA_HEADS * LANE), seq_blk)],
        out_shape=[jax.ShapeDtypeStruct((B, Lc + S, qkw), BF16), jax.ShapeDtypeStruct((B, Lc + S, qkw), BF16),
                   jax.ShapeDtypeStruct((B, Lc + S, vw + MLA_HEADS * LANE), BF16)],
        compiler_params=_params(("arbitrary",)),
        name="mla_qkv",
    )(p, p, p, q_norm_g.reshape(1, -1), kv_norm_g.reshape(1, -1), wq, wk, wv, qk_g, cos_t, sin_t)


def _attn_kernel(q_ref, k_ref, v_ref, o_ref, *, sub):
    nt = (((1,), (1,)), ((), ()))
    for r in range(0, q_ref.shape[0], sub):
        s = lax.dot_general(q_ref[r:r + sub, :], k_ref[...], nt, preferred_element_type=F32)
        e = jnp.exp2(s - jnp.max(s, axis=-1, keepdims=True)).astype(BF16)
        ov = jnp.dot(e, v_ref[...], preferred_element_type=F32)
        o_ref[r:r + sub, :] = (ov[:, :MLA_V] / ov[:, MLA_V:MLA_V + 1]).astype(BF16)


def _attention(q, k, v, rows, *, tq):
    B, Lc, S = rows.B, rows.Lc, rows.S
    nq = S // tq
    return pl.pallas_call(
        functools.partial(_attn_kernel, sub=min(tq, 256)),
        grid=(B, MLA_HEADS, nq),
        in_specs=[pl.BlockSpec((None, tq, MLA_PAD), lambda b, h, i: (b, i, h)),
                  pl.BlockSpec((None, Lc + S, MLA_PAD), lambda b, h, i: (b, 0, h)),
                  pl.BlockSpec((None, Lc + S, MLA_V + LANE), lambda b, h, i: (b, 0, h))],
        out_specs=pl.BlockSpec((tq, MLA_V), lambda b, h, i: (b * nq + i, h)),
        out_shape=jax.ShapeDtypeStruct((B * S, MLA_HEADS * MLA_V), BF16),
        compiler_params=_params(("arbitrary", "arbitrary", "arbitrary")),
        name="attention",
    )(q, k, v)


def _gelu_tanh(x):
    return 0.5 * x * (1.0 + jnp.tanh(math.sqrt(2.0 / math.pi) * (x + 0.044715 * (x * x * x))))


def _merge_cd_kernel(x_ref, mod_ref, yb_ref, hf_ref, hb_ref, att_ref, w1_ref, w2_ref, o_ref):
    y1 = _gelu_tanh(yb_ref[...]) * (hf_ref[...] + hb_ref[...])
    y = (jnp.dot(y1.astype(BF16), w1_ref[...], preferred_element_type=F32)
         + jnp.dot(att_ref[...], w2_ref[...], preferred_element_type=F32))
    o_ref[...] = x_ref[...] + mod_ref[5:6, :] * y


def _merge_cd(x, mods, p, h_f, h_b, att, w_out, rows, *, tm):
    D = x.shape[1]
    W = LRU_WIDTH
    aw = MLA_HEADS * MLA_V
    off = rows.n_ctx // tm
    lat = lambda i: (i + off, 0)
    return pl.pallas_call(
        _merge_cd_kernel,
        grid=(rows.n_lat // tm,),
        in_specs=[pl.BlockSpec((tm, D), lat),
                  pl.BlockSpec((None, N_MOD, D), lambda i: (rows.mod_row(i, tm, False), 0, 0)),
                  pl.BlockSpec((tm, W), lambda i: (i + off, CD_YB // W)),
                  pl.BlockSpec((tm, W), lat), pl.BlockSpec((tm, W), lat),
                  pl.BlockSpec((tm, aw), lambda i: (i, 0)),
                  pl.BlockSpec((W, D), lambda i: (0, 0)),
                  pl.BlockSpec((aw, D), lambda i: (W // aw, 0))],
        out_specs=pl.BlockSpec((tm, D), lambda i: (i, 0)),
        out_shape=jax.ShapeDtypeStruct((rows.n_lat, D), F32),
        compiler_params=_params(("arbitrary",)),
        name="merge_cd",
    )(x, mods, p, h_f, h_b, att, w_out, w_out)


def _ret_tables(rows):
    half = RET_DK // 2
    freqs = ROPE_BASE ** (-jnp.arange(half, dtype=F32) / half)
    ang = jnp.arange(rows.S, dtype=F32)[:, None] * freqs
    cos = jnp.concatenate([jnp.cos(ang), jnp.cos(ang)], axis=-1)
    sin = jnp.concatenate([-jnp.sin(ang), jnp.sin(ang)], axis=-1)
    cos = jnp.concatenate([jnp.ones((rows.Lc, RET_DK), F32), cos], axis=0)
    sin = jnp.concatenate([jnp.zeros((rows.Lc, RET_DK), F32), sin], axis=0)
    return cos, sin


def _mla_tables(rows):
    S = rows.S
    quarter = MLA_ROPE // 4
    freqs = ROPE_BASE ** (-jnp.arange(quarter, dtype=F32) / quarter)
    t = jnp.arange(S)
    row = (t // GRID_W).astype(F32)
    col = (t % GRID_W).astype(F32)

    def part(pos):
        ang = pos[:, None] * freqs
        return (jnp.concatenate([jnp.cos(ang), jnp.cos(ang)], axis=-1),
                jnp.concatenate([-jnp.sin(ang), jnp.sin(ang)], axis=-1))

    rc, rs = part(row)
    cc, cs = part(col)
    tail = MLA_PAD - MLA_QK
    cos = jnp.concatenate([jnp.ones((S, MLA_NOPE), F32), rc, cc, jnp.ones((S, tail), F32)], axis=-1)
    sin = jnp.concatenate([jnp.zeros((S, MLA_NOPE), F32), rs, cs, jnp.zeros((S, tail), F32)], axis=-1)
    cos = jnp.concatenate([jnp.ones((rows.Lc, MLA_PAD), F32), cos], axis=0)
    sin = jnp.concatenate([jnp.zeros((rows.Lc, MLA_PAD), F32), sin], axis=0)
    return cos, sin


def _pad_cols(w, n):
    return jnp.pad(w, ((0, 0), (0, n - w.shape[1])))


def _head_pad(w, heads, width, padded):
    K = w.shape[0]
    return jnp.pad(w.reshape(K, heads, width), ((0, 0), (0, 0), (0, padded - width))).reshape(K, heads * padded)


def kernel(x, c, ctx, c_ctx, ada_w, ada_b, norm_g, ffn_wg, ffn_wu, ffn_wd, ab_w_in, ab_w_out, ret_decay_logit, ret_gn_g, mlstm_gate_b, mlstm_gn_g, cd_w_in, cd_w_out, lru_conv_w, lru_conv_b, lru_wa, lru_ba, lru_wx, lru_bx, lru_lambda, mla_q_norm_g, mla_kv_norm_g, mla_w_uq, mla_w_uk, mla_w_uv, mla_qk_norm_g):
    B, S, D = x.shape
    Lc = ctx.shape[1]
    depth = ada_w.shape[0]
    F = ffn_wg.shape[-1]
    rows = Rows(B, Lc, S)
    assert B < SUBLANE and Lc % CHUNK == 0 and S % Lc == 0 and S % GRID_W == 0

    tm = min(1024, rows.n_ctx)
    tf = 512
    Fp = _round_up(F, tf)

    cond = jnp.zeros((SUBLANE, D), F32).at[:B].set(c.astype(F32)).at[B].set(c_ctx.astype(F32))
    mods = _adaln(cond, ada_w, ada_b)

    xa = jnp.concatenate([ctx.reshape(B * Lc, D), x.reshape(B * S, D)], axis=0).astype(F32)
    wg = jnp.pad(ffn_wg.astype(BF16), ((0, 0), (0, 0), (0, 0), (0, Fp - F)))
    wu = jnp.pad(ffn_wu.astype(BF16), ((0, 0), (0, 0), (0, 0), (0, Fp - F)))
    wd = jnp.pad(ffn_wd.astype(BF16), ((0, 0), (0, 0), (0, Fp - F), (0, 0)))

    for l in range(depth):
        last = l == depth - 1
        j = l // 2
        m_l = mods[l]
        xa = _ffn(xa, m_l, norm_g[l, 0], wg, wu, wd, rows, layer=l, half=0, mod_base=0, with_ctx=True,
                  tm=tm, tf=tf)

        if l % 2 == 0:
            w_in = _pad_cols(ab_w_in[j].astype(BF16), AB_PAD)
            p = _proj(xa, m_l, norm_g[l, 1], w_in, rows, tm=tm, tn=AB_PAD // 7)
            cos_t, sin_t = _ret_tables(rows)
            ret_f, ret_b = _retention(p, ret_decay_logit[j], cos_t, sin_t, rows)
            ml_f, ml_b = _mlstm(p, mlstm_gate_b[j], rows)
            xa = _merge_ab(xa, m_l, ret_f, ret_b, ml_f, ml_b, p, ret_gn_g[j], mlstm_gn_g[j],
                           ab_w_out[j].astype(BF16), rows, tm=min(256, tm))
            if last:
                xa = xa[rows.n_ctx:]
        else:
            assert last, "context outputs of the recurrent/attention mixer are not produced"
            w_in = _pad_cols(cd_w_in[j].astype(BF16), CD_PAD)
            p = _proj(xa, m_l, norm_g[l, 1], w_in, rows, tm=tm, tn=CD_PAD // 3)
            a_c, b_c = _lru_coef(p, lru_conv_w[j], lru_conv_b[j], lru_wa[j].astype(BF16), lru_wx[j].astype(BF16),
                                 lru_ba[j], lru_bx[j], lru_lambda[j], rows, tm=min(256, Lc))
            h_f, h_b = _lru_scan(a_c, b_c, rows)
            cos_t, sin_t = _mla_tables(rows)
            wq = _head_pad(mla_w_uq[j].astype(BF16), MLA_HEADS, MLA_QK, MLA_PAD)
            qk_g = jnp.pad(mla_qk_norm_g[j].astype(F32), ((0, 0), (0, MLA_PAD - MLA_QK)))
            q, k, v = _mla_qkv(p, mla_q_norm_g[j], mla_kv_norm_g[j], wq, mla_w_uk[j].astype(BF16),
                               mla_w_uv[j].astype(BF16), qk_g, cos_t, sin_t, rows, tm=min(256, Lc))
            att = _attention(q, k, v, rows, tq=min(2048, S))
            xa = _merge_cd(xa, m_l, p, h_f, h_b, att, cd_w_out[j].astype(BF16), rows, tm=min(256, tm))

        xa = _ffn(xa, m_l, norm_g[l, 2], wg, wu, wd, rows, layer=l, half=1, mod_base=6,
                  with_ctx=not last, tm=tm, tf=tf)

    if xa.shape[0] != rows.n_lat:
        xa = xa[rows.n_ctx:]
    return xa.reshape(B, S, D).astype(x.dtype)
```

```python
import functools
import math

import jax
import jax.numpy as jnp
import numpy as np
from jax import lax
from jax.experimental import pallas as pl
from jax.experimental.pallas import tpu as pltpu

F32 = jnp.float32
BF16 = jnp.bfloat16

N_MOD = 9
EPS = 1e-6
CHUNK = 128
GRID_W = 64
ROPE_BASE = 10000.0
NEG_BIG = -1e30

RET_HEADS = 4
RET_DK = 128
RET_DV = 256
MLSTM_HEADS = 4
MLSTM_DK = 128
MLSTM_DV = 256
LRU_WIDTH = 1024
LRU_BLOCKS = 8
LRU_BLOCK = LRU_WIDTH // LRU_BLOCKS
LRU_C = 8.0
MLA_HEADS = 8
MLA_Q_RANK = 512
MLA_KV_RANK = 256
MLA_NOPE = 128
MLA_ROPE = 64
MLA_V = 128
MLA_QK = MLA_NOPE + MLA_ROPE
LOG2E = math.log2(math.e)
MLA_PAD = 256

AB_SPLITS = (RET_HEADS * RET_DK, RET_HEADS * RET_DK, RET_HEADS * RET_DV, RET_HEADS * RET_DV,
             MLSTM_HEADS * MLSTM_DK, MLSTM_HEADS * MLSTM_DK, MLSTM_HEADS * MLSTM_DV, MLSTM_HEADS * MLSTM_DV,
             4 * MLSTM_HEADS)
AB_IN = sum(AB_SPLITS)
CD_SPLITS = (LRU_WIDTH, LRU_WIDTH, MLA_Q_RANK, MLA_KV_RANK, MLA_ROPE)
CD_IN = sum(CD_SPLITS)

LANE = 128
SUBLANE = 8
VMEM_LIMIT = 56 * 1024 * 1024
FFN_VMEM_LIMIT = 60 * 1024 * 1024

AB_QR, AB_KR, AB_VR, AB_GR, AB_QM, AB_KM, AB_VM, AB_GM, AB_GATES = np.cumsum((0,) + AB_SPLITS[:-1]).tolist()
AB_PAD = 6272
CD_YB, CD_XB, CD_CQ, CD_CKV, CD_KR = np.cumsum((0,) + CD_SPLITS[:-1]).tolist()
CD_PAD = 3072


def _params(sem, vmem=VMEM_LIMIT):
    return pltpu.CompilerParams(dimension_semantics=sem, vmem_limit_bytes=vmem)


def _round_up(n, m):
    return (n + m - 1) // m * m


def _sigmoid(x):
    return 0.5 * jnp.tanh(0.5 * x) + 0.5


def _log_sigmoid(x):
    return jnp.minimum(x, 0.0) - jnp.log(1.0 + jnp.exp(-jnp.abs(x)))


def _rms(x, g):
    return x * lax.rsqrt(jnp.mean(x * x, axis=-1, keepdims=True) + EPS) * g


def _adaln_kernel(c_ref, w_ref, b_ref, o_ref):
    c = c_ref[...]
    s = (c * _sigmoid(c)).astype(BF16)
    o_ref[...] = jnp.dot(s, w_ref[...].astype(BF16), preferred_element_type=F32) + b_ref[...]


def _adaln(cond, ada_w, ada_b):
    L, D, N = ada_w.shape
    tn = 1024 if N % 1024 == 0 else N
    out = pl.pallas_call(
        _adaln_kernel,
        grid=(L, N // tn),
        in_specs=[pl.BlockSpec((SUBLANE, D), lambda l, n: (0, 0)),
                  pl.BlockSpec((None, D, tn), lambda l, n: (l, 0, n)),
                  pl.BlockSpec((None, 1, tn), lambda l, n: (l, 0, n))],
        out_specs=pl.BlockSpec((None, SUBLANE, tn), lambda l, n: (l, 0, n)),
        out_shape=jax.ShapeDtypeStruct((L, SUBLANE, N), F32),
        compiler_params=_params(("arbitrary", "arbitrary")),
        name="adaln",
    )(cond, ada_w, ada_b.reshape(L, 1, N))
    return out.reshape(L, SUBLANE, N_MOD, D)


class Rows:
    def __init__(self, B, Lc, S):
        self.B, self.Lc, self.S = B, Lc, S
        self.n_ctx = B * Lc
        self.n_lat = B * S
        self.n_all = self.n_ctx + self.n_lat

    def mod_row(self, tile, tm, with_ctx):
        if not with_ctx:
            return (tile * tm) // self.S
        nct = self.n_ctx // tm
        return jnp.where(tile < nct, self.B, (jnp.maximum(tile - nct, 0) * tm) // self.S)


def _ffn_kernel(x_ref, mod_ref, g_ref, wg_ref, wu_ref, wd_ref, o_ref, h_scr, *, mod_base, sub, last_cols):
    f = pl.program_id(1)
    nf = pl.num_programs(1)
    tm = x_ref.shape[0]
    tf = wg_ref.shape[1]

    def step(cols, first, final):
        if first:
            shift = mod_ref[mod_base:mod_base + 1, :]
            scale = 1.0 + mod_ref[mod_base + 1:mod_base + 2, :]
        if final:
            gate = 0.5 * mod_ref[mod_base + 2:mod_base + 3, :]
        for r in range(0, tm, sub):
            rs = slice(r, r + sub)
            if first:
                h = (_rms(x_ref[rs, :], g_ref[...]) * scale + shift).astype(BF16)
                h_scr[rs, :] = h
            else:
                h = h_scr[rs, :]
            a = jnp.dot(h, wg_ref[:, :cols], preferred_element_type=F32)
            u = jnp.dot(h, wu_ref[:, :cols], preferred_element_type=F32)
            act = (a * _sigmoid(a) * u).astype(BF16)
            part = jnp.dot(act, wd_ref[:cols, :], preferred_element_type=F32)
            if first:
                o_ref[rs, :] = part
            elif final:
                o_ref[rs, :] = x_ref[rs, :] + gate * (o_ref[rs, :] + part)
            else:
                o_ref[rs, :] += part

    pl.when(f == 0)(functools.partial(step, tf, True, False))
    pl.when(jnp.logical_and(f > 0, f < nf - 1))(functools.partial(step, tf, False, False))
    pl.when(f == nf - 1)(functools.partial(step, last_cols, False, True))


def _ffn_kernel_into(x_ref, mod_ref, g_ref, wg_ref, wu_ref, wd_ref, into_ref, o_ref, h_scr, **kw):
    del into_ref
    _ffn_kernel(x_ref, mod_ref, g_ref, wg_ref, wu_ref, wd_ref, o_ref, h_scr, **kw)


def _ffn(x, mods, g, wg, wu, wd, *, layer, half, mod_base, mod_of_tile, tm, tf, out_rows=None, out_tile0=0,
         into=None):
    n, D = x.shape
    F = wg.shape[-1]
    nf = pl.cdiv(F, tf)
    assert nf >= 2 and n % tm == 0
    out_rows = n if out_rows is None else out_rows
    kw = dict(mod_base=mod_base, sub=min(tm, 512), last_cols=F - (nf - 1) * tf)
    in_specs = [pl.BlockSpec((tm, D), lambda i, f: (i, 0)),
                pl.BlockSpec((None, N_MOD, D), lambda i, f: (mod_of_tile(i), 0, 0)),
                pl.BlockSpec((1, D), lambda i, f: (0, 0)),
                pl.BlockSpec((None, None, D, tf), lambda i, f: (layer, half, 0, f)),
                pl.BlockSpec((None, None, D, tf), lambda i, f: (layer, half, 0, f)),
                pl.BlockSpec((None, None, tf, D), lambda i, f: (layer, half, f, 0))]
    args = [x, mods, g.reshape(1, D), wg, wu, wd]
    if into is None:
        kern, aliases = functools.partial(_ffn_kernel, **kw), {}
    else:
        assert into.shape == (out_rows, D)
        kern, aliases = functools.partial(_ffn_kernel_into, **kw), {len(args): 0}
        in_specs.append(pl.BlockSpec(memory_space=pl.ANY))
        args.append(into)
    return pl.pallas_call(
        kern,
        grid=(n // tm, nf),
        in_specs=in_specs,
        out_specs=pl.BlockSpec((tm, D), lambda i, f: (out_tile0 + i, 0)),
        out_shape=jax.ShapeDtypeStruct((out_rows, D), F32),
        scratch_shapes=[pltpu.VMEM((tm, D), BF16)],
        input_output_aliases=aliases,
        compiler_params=_params(("arbitrary", "arbitrary"), FFN_VMEM_LIMIT),
        name="ffn",
    )(*args)


def _proj_kernel(x_ref, mod_ref, g_ref, w_ref, o_ref, h_scr, *, sub):
    tm = x_ref.shape[0]

    @pl.when(pl.program_id(1) == 0)
    def _():
        scale = 1.0 + mod_ref[4:5, :]
        for r in range(0, tm, sub):
            h = (_rms(x_ref[r:r + sub, :], g_ref[...]) * scale + mod_ref[3:4, :]).astype(BF16)
            h_scr[r:r + sub, :] = h
            o_ref[r:r + sub, :] = jnp.dot(h, w_ref[...], preferred_element_type=F32)

    @pl.when(pl.program_id(1) != 0)
    def _():
        for r in range(0, tm, sub):
            o_ref[r:r + sub, :] = jnp.dot(h_scr[r:r + sub, :], w_ref[...], preferred_element_type=F32)


def _proj(x, mods, g, w, rows, *, tm, tn):
    n, D = x.shape
    Np = w.shape[1]
    return pl.pallas_call(
        functools.partial(_proj_kernel, sub=min(tm, 512)),
        grid=(n // tm, Np // tn),
        in_specs=[pl.BlockSpec((tm, D), lambda i, j: (i, 0)),
                  pl.BlockSpec((None, N_MOD, D), lambda i, j: (rows.mod_row(i, tm, True), 0, 0)),
                  pl.BlockSpec((1, D), lambda i, j: (0, 0)),
                  pl.BlockSpec((D, tn), lambda i, j: (0, j))],
        out_specs=pl.BlockSpec((tm, tn), lambda i, j: (i, j)),
        out_shape=jax.ShapeDtypeStruct((n, Np), F32),
        scratch_shapes=[pltpu.VMEM((tm, D), BF16)],
        compiler_params=_params(("arbitrary", "arbitrary")),
        name="proj",
    )(x, mods, g.reshape(1, D), w)


def _chunk_maps(rows):
    B = rows.B
    nc = rows.Lc // CHUNK
    nl = rows.S // CHUNK

    def seq_f(j):
        return j

    def seq_b(j):
        return jnp.where(j < nc, nc - 1 - j, nc + nl - 1 - (j - nc))

    def unit(b, s):
        return jnp.where(s < nc, b * nc + s, B * nc + b * nl + (s - nc))

    return nc + nl, seq_f, seq_b, unit


def _rope128(x, cos, sin):
    return x * cos + pltpu.roll(x, 64, axis=1) * sin


def _tri_masks():
    r = lax.broadcasted_iota(jnp.int32, (CHUNK, CHUNK), 0)
    c = lax.broadcasted_iota(jnp.int32, (CHUNK, CHUNK), 1)
    return r - c


def _ret_kernel(dl_ref, qf_ref, kf_ref, vf_ref, cf_ref, sf_ref, qb_ref, kb_ref, vb_ref, cb_ref, sb_ref,
                of_ref, ob_ref, s_scr):
    j = pl.program_id(1)

    @pl.when(j == 0)
    def _():
        s_scr[...] = jnp.zeros_like(s_scr)

    lg_all = _log_sigmoid(dl_ref[...])
    rel = _tri_masks().astype(F32)
    pos_c = lax.broadcasted_iota(jnp.int32, (CHUNK, 1), 0).astype(F32)
    dirs = ((qf_ref, kf_ref, vf_ref, cf_ref, sf_ref, of_ref, False),
            (qb_ref, kb_ref, vb_ref, cb_ref, sb_ref, ob_ref, True))
    for d, (q_ref, k_ref, v_ref, c_ref, sn_ref, o_ref, rev) in enumerate(dirs):
        cos = c_ref[...]
        sin = sn_ref[...]
        dist = -rel if rev else rel
        step = (CHUNK - 1.0 - pos_c) if rev else pos_c
        for h in range(RET_HEADS):
            r = d * RET_HEADS + h
            lg = lg_all[r:r + 1, 0:1]
            q = _rope128(q_ref[:, h * RET_DK:(h + 1) * RET_DK], cos, sin)
            k = _rope128(k_ref[:, h * RET_DK:(h + 1) * RET_DK], cos, sin) * (RET_DK ** -0.5)
            v = v_ref[:, h * RET_DV:(h + 1) * RET_DV].astype(BF16)
            qb16 = q.astype(BF16)
            decay = jnp.where(dist >= 0, jnp.exp(jnp.maximum(dist, 0.0) * lg), 0.0)
            sc = lax.dot_general(qb16, k.astype(BF16), (((1,), (1,)), ((), ())), preferred_element_type=F32)
            inner = jnp.dot((sc * decay).astype(BF16), v, preferred_element_type=F32)
            s_prev = s_scr[r]
            xi = jnp.exp((step + 1.0) * lg)
            cross = jnp.dot(qb16, s_prev.astype(BF16), preferred_element_type=F32) * xi
            o_ref[:, h * RET_DV:(h + 1) * RET_DV] = inner + cross
            zeta = jnp.exp((CHUNK - 1.0 - step) * lg)
            kz_t = jnp.transpose(k * zeta).astype(BF16)
            u = jnp.dot(kz_t, v, preferred_element_type=F32)
            s_scr[r] = jnp.exp(CHUNK * lg) * s_prev + u


def _retention(p, decay_logit, cos_t, sin_t, rows):
    n = p.shape[0]
    B = rows.B
    n_steps, seq_f, seq_b, unit = _chunk_maps(rows)
    qw = RET_HEADS * RET_DK
    vw = RET_HEADS * RET_DV
    dl = jnp.broadcast_to(decay_logit.astype(F32).reshape(2 * RET_HEADS, 1), (2 * RET_HEADS, LANE))

    def in_specs(seq):
        return [pl.BlockSpec((CHUNK, qw), lambda b, j: (unit(b, seq(j)), AB_QR // qw)),
                pl.BlockSpec((CHUNK, qw), lambda b, j: (unit(b, seq(j)), AB_KR // qw)),
                pl.BlockSpec((CHUNK, vw), lambda b, j: (unit(b, seq(j)), AB_VR // vw)),
                pl.BlockSpec((CHUNK, RET_DK), lambda b, j: (seq(j), 0)),
                pl.BlockSpec((CHUNK, RET_DK), lambda b, j: (seq(j), 0))]

    def out_spec(seq):
        return pl.BlockSpec((CHUNK, vw), lambda b, j: (unit(b, seq(j)), 0))

    return pl.pallas_call(
        _ret_kernel,
        grid=(B, n_steps),
        in_specs=[pl.BlockSpec((2 * RET_HEADS, LANE), lambda b, j: (0, 0))] + in_specs(seq_f) + in_specs(seq_b),
        out_specs=[out_spec(seq_f), out_spec(seq_b)],
        out_shape=[jax.ShapeDtypeStruct((n, vw), F32)] * 2,
        scratch_shapes=[pltpu.VMEM((2 * RET_HEADS, RET_DK, RET_DV), F32)],
        compiler_params=_params(("arbitrary", "arbitrary")),
        name="retention",
    )(dl, p, p, p, cos_t, sin_t, p, p, p, cos_t, sin_t)


def _scan_lanes(x, reverse):
    n = x.shape[-1]
    lane = lax.broadcasted_iota(jnp.int32, x.shape, 1)
    d = 1
    while d < n:
        if reverse:
            x = x + jnp.where(lane < n - d, pltpu.roll(x, n - d, axis=1), 0.0)
        else:
            x = x + jnp.where(lane >= d, pltpu.roll(x, d, axis=1), 0.0)
        d *= 2
    return x


def _mlstm_kernel(gb_ref, qf_ref, kf_ref, vf_ref, gf_ref, qb_ref, kb_ref, vb_ref, gbk_ref,
                  of_ref, ob_ref, c_scr, m_scr):
    j = pl.program_id(1)
    H = MLSTM_HEADS
    dv = MLSTM_DV
    ext = dv + LANE

    @pl.when(j == 0)
    def _():
        c_scr[...] = jnp.zeros_like(c_scr)
        m_scr[...] = jnp.full_like(m_scr, NEG_BIG)

    rel = _tri_masks()
    zpad = jnp.zeros((CHUNK - SUBLANE, CHUNK), F32)
    ones = jnp.ones((CHUNK, LANE), BF16)
    dirs = ((qf_ref, kf_ref, vf_ref, gf_ref, of_ref, False),
            (qb_ref, kb_ref, vb_ref, gbk_ref, ob_ref, True))
    for d, (q_ref, k_ref, v_ref, g_ref, o_ref, rev) in enumerate(dirs):
        gt = jnp.transpose(g_ref[...] + gb_ref[...])
        g8 = gt[2 * d * H:2 * d * H + SUBLANE, :]
        cs = _scan_lanes(_log_sigmoid(g8), rev)
        cs_t = jnp.transpose(jnp.concatenate([cs, zpad], axis=0))
        mask = (rel <= 0) if rev else (rel >= 0)
        for h in range(H):
            r = d * H + h
            i_row = g8[h:h + 1, :]
            b_row = cs[H + h:H + h + 1, :]
            b_col = cs_t[:, H + h:H + h + 1]
            q = (q_ref[:, h * MLSTM_DK:(h + 1) * MLSTM_DK] * (MLSTM_DK ** -0.5)).astype(BF16)
            k = k_ref[:, h * MLSTM_DK:(h + 1) * MLSTM_DK]
            v_ext = jnp.concatenate([v_ref[:, h * dv:(h + 1) * dv].astype(BF16), ones], axis=1)
            m_prev = m_scr[r:r + 1, 0:1]
            c_prev = c_scr[r]

            log_d = jnp.where(mask, b_col + (i_row - b_row), -jnp.inf)
            m_intra = jnp.max(log_d, axis=1, keepdims=True)
            log_inter = b_col + m_prev
            m_t = jnp.maximum(log_inter, m_intra)
            dmat = jnp.exp(log_d - m_t)
            s = lax.dot_general(q, k.astype(BF16), (((1,), (1,)), ((), ())), preferred_element_type=F32) * dmat
            inter = jnp.exp(log_inter - m_t)
            tot = (jnp.dot(s.astype(BF16), v_ext, preferred_element_type=F32)
                   + jnp.dot(q, c_prev.astype(BF16), preferred_element_type=F32) * inter)
            den = tot[:, dv:dv + 1]
            o_ref[:, h * dv:(h + 1) * dv] = tot[:, :dv] / jnp.maximum(jnp.abs(den), jnp.exp(-m_t))

            b_last = b_row[:, 0:1] if rev else b_row[:, CHUNK - 1:CHUNK]
            log_w = b_last - b_row + i_row
            m_loc = jnp.max(log_w, axis=1, keepdims=True)
            kw_t = (jnp.transpose(k) * jnp.exp(log_w - m_loc)).astype(BF16)
            u = jnp.dot(kw_t, v_ext, preferred_element_type=F32)
            m_new = jnp.maximum(b_last + m_prev, m_loc)
            c_scr[r] = jnp.exp(b_last + m_prev - m_new) * c_prev + jnp.exp(m_loc - m_new) * u
            m_scr[r:r + 1, :] = jnp.broadcast_to(m_new, (1, LANE))


def _mlstm(p, gate_b, rows):
    n = p.shape[0]
    B = rows.B
    H = MLSTM_HEADS
    n_steps, seq_f, seq_b, unit = _chunk_maps(rows)
    qw = H * MLSTM_DK
    vw = H * MLSTM_DV
    gb = jnp.zeros((1, LANE), F32).at[0, :4 * H].set(gate_b.astype(F32).reshape(4 * H))

    def in_specs(seq):
        return [pl.BlockSpec((CHUNK, qw), lambda b, j: (unit(b, seq(j)), AB_QM // qw)),
                pl.BlockSpec((CHUNK, qw), lambda b, j: (unit(b, seq(j)), AB_KM // qw)),
                pl.BlockSpec((CHUNK, vw), lambda b, j: (unit(b, seq(j)), AB_VM // vw)),
                pl.BlockSpec((CHUNK, LANE), lambda b, j: (unit(b, seq(j)), AB_GATES // LANE))]

    def out_spec(seq):
        return pl.BlockSpec((CHUNK, vw), lambda b, j: (unit(b, seq(j)), 0))

    return pl.pallas_call(
        _mlstm_kernel,
        grid=(B, n_steps),
        in_specs=[pl.BlockSpec((1, LANE), lambda b, j: (0, 0))] + in_specs(seq_f) + in_specs(seq_b),
        out_specs=[out_spec(seq_f), out_spec(seq_b)],
        out_shape=[jax.ShapeDtypeStruct((n, vw), F32)] * 2,
        scratch_shapes=[pltpu.VMEM((2 * H, MLSTM_DK, MLSTM_DV + LANE), F32),
                        pltpu.VMEM((2 * SUBLANE, LANE), F32)],
        compiler_params=_params(("arbitrary", "arbitrary")),
        name="mlstm",
    )(gb, p, p, p, p, p, p, p, p)


def _head_ln(y, g, heads, width):
    outs = []
    for h in range(heads):
        yh = y[:, h * width:(h + 1) * width]
        mu = jnp.mean(yh, axis=-1, keepdims=True)
        yc = yh - mu
        var = jnp.mean(yc * yc, axis=-1, keepdims=True)
        outs.append(yc * lax.rsqrt(var + EPS))
    return jnp.concatenate(outs, axis=1) * g


def _merge_ab_kernel(x_ref, mod_ref, rf_ref, rb_ref, mf_ref, mb_ref, gr_ref, gm_ref, rg_ref, mg_ref,
                     w1_ref, w2_ref, o_ref):
    gr = gr_ref[...]
    ret_y = (gr * _sigmoid(gr)) * _head_ln(rf_ref[...] + rb_ref[...], rg_ref[...], RET_HEADS, RET_DV)
    ml_y = _sigmoid(gm_ref[...]) * _head_ln(mf_ref[...] + mb_ref[...], mg_ref[...], MLSTM_HEADS, MLSTM_DV)
    y = (jnp.dot(ret_y.astype(BF16), w1_ref[...], preferred_element_type=F32)
         + jnp.dot(ml_y.astype(BF16), w2_ref[...], preferred_element_type=F32))
    o_ref[...] = x_ref[...] + mod_ref[5:6, :] * y


def _merge_ab(x, mods, ret_f, ret_b, ml_f, ml_b, p, ret_g, ml_g, w_out, rows, *, tm):
    n, D = x.shape
    rw = RET_HEADS * RET_DV
    mw = MLSTM_HEADS * MLSTM_DV
    row = lambda i: (i, 0)
    return pl.pallas_call(
        _merge_ab_kernel,
        grid=(n // tm,),
        in_specs=[pl.BlockSpec((tm, D), row),
                  pl.BlockSpec((None, N_MOD, D), lambda i: (rows.mod_row(i, tm, True), 0, 0)),
                  pl.BlockSpec((tm, rw), row), pl.BlockSpec((tm, rw), row),
                  pl.BlockSpec((tm, mw), row), pl.BlockSpec((tm, mw), row),
                  pl.BlockSpec((tm, rw), lambda i: (i, AB_GR // rw)),
                  pl.BlockSpec((tm, mw), lambda i: (i, AB_GM // mw)),
                  pl.BlockSpec((1, rw), lambda i: (0, 0)),
                  pl.BlockSpec((1, mw), lambda i: (0, 0)),
                  pl.BlockSpec((rw, D), lambda i: (0, 0)),
                  pl.BlockSpec((mw, D), lambda i: (rw // mw, 0))],
        out_specs=pl.BlockSpec((tm, D), row),
        out_shape=jax.ShapeDtypeStruct((n, D), F32),
        compiler_params=_params(("arbitrary",)),
        name="merge_ab",
    )(x, mods, ret_f, ret_b, ml_f, ml_b, p, p, ret_g.reshape(1, rw), ml_g.reshape(1, mw), w_out, w_out)


def _lru_coef_kernel(x_ref, xp_ref, xn_ref, cw_ref, cb_ref, wa_ref, wx_ref, ba_ref, bx_ref, lam_ref,
                     a_ref, b_ref, *, tm, ctx_tiles, ctx_seg, lat_seg):
    i = pl.program_id(0)
    seg_pos = jnp.where(i < ctx_tiles, i % ctx_seg, (i - ctx_tiles) % lat_seg)
    seg_len = jnp.where(i < ctx_tiles, ctx_seg, lat_seg)
    keep_prev = (seg_pos != 0).astype(F32)
    keep_next = (seg_pos != seg_len - 1).astype(F32)
    xe = jnp.concatenate([xp_ref[...] * keep_prev, x_ref[...], xn_ref[...] * keep_next], axis=0)
    ne = tm + 2 * SUBLANE
    xc = cb_ref[...] + cw_ref[2:3, :] * x_ref[...]
    for tap, off in ((0, -2), (1, -1), (3, 1)):
        shifted = pltpu.roll(xe, (-off) % ne, axis=0)[SUBLANE:SUBLANE + tm, :]
        xc = xc + cw_ref[tap:tap + 1, :] * shifted
    for d in range(2):
        lam = lam_ref[d:d + 1, :]
        sp = jnp.maximum(-lam, 0.0) + jnp.log(1.0 + jnp.exp(-jnp.abs(lam)))
        rate = (-LRU_C * LOG2E) * sp
        for g in range(LRU_BLOCKS):
            sl = slice(g * LRU_BLOCK, (g + 1) * LRU_BLOCK)
            xg = xc[:, sl]
            xg16 = xg.astype(BF16)
            r = _sigmoid(jnp.dot(xg16, wa_ref[d, g], preferred_element_type=F32) + ba_ref[d:d + 1, sl])
            ig = _sigmoid(jnp.dot(xg16, wx_ref[d, g], preferred_element_type=F32) + bx_ref[d:d + 1, sl])
            a = jnp.exp2(r * rate[:, sl])
            a_ref[d, :, sl] = a
            b_ref[d, :, sl] = jnp.sqrt(1.0 - a * a) * (ig * xg)


def _lru_coef(p, conv_w, conv_b, wa, wx, ba, bx, lam, rows, *, tm):
    n = p.shape[0]
    W = LRU_WIDTH
    tpb = tm // SUBLANE
    n8 = n // SUBLANE
    kern = functools.partial(_lru_coef_kernel, tm=tm, ctx_tiles=rows.n_ctx // tm,
                             ctx_seg=rows.Lc // tm, lat_seg=rows.S // tm)
    full = lambda *s: pl.BlockSpec(s, lambda i: (0,) * len(s))
    return pl.pallas_call(
        kern,
        grid=(n // tm,),
        in_specs=[pl.BlockSpec((tm, W), lambda i: (i, CD_XB // W)),
                  pl.BlockSpec((SUBLANE, W), lambda i: (jnp.maximum(i * tpb - 1, 0), CD_XB // W)),
                  pl.BlockSpec((SUBLANE, W), lambda i: (jnp.minimum((i + 1) * tpb, n8 - 1), CD_XB // W)),
                  full(4, W), full(1, W), full(2, LRU_BLOCKS, LRU_BLOCK, LRU_BLOCK),
                  full(2, LRU_BLOCKS, LRU_BLOCK, LRU_BLOCK), full(2, W), full(2, W), full(2, W)],
        out_specs=[pl.BlockSpec((2, tm, W), lambda i: (0, i, 0))] * 2,
        out_shape=[jax.ShapeDtypeStruct((2, n, W), F32)] * 2,
        compiler_params=_params(("arbitrary",)),
        name="lru_coef",
    )(p, p, p, conv_w, conv_b.reshape(1, W), wa, wx, ba, bx, lam)


def _lru_scan_kernel(af_ref, bf_ref, ab_ref, bb_ref, of_ref, ob_ref, h_scr, *, tb, lw):
    @pl.when(pl.program_id(1) == 0)
    def _():
        h_scr[...] = jnp.zeros_like(h_scr)

    row = lax.broadcasted_iota(jnp.int32, (SUBLANE, lw), 0)
    ng = tb // SUBLANE

    def scan8(a, b, rev):
        d = 1
        while d < SUBLANE:
            if rev:
                keep = row < SUBLANE - d
                sh = SUBLANE - d
            else:
                keep = row >= d
                sh = d
            a_sh = jnp.where(keep, pltpu.roll(a, sh, axis=0), 1.0)
            b_sh = jnp.where(keep, pltpu.roll(b, sh, axis=0), 0.0)
            b = a * b_sh + b
            a = a * a_sh
            d *= 2
        return a, b

    for c in range(LRU_WIDTH // lw):
        cs = slice(c * lw, (c + 1) * lw)

        def body(g, carry):
            hf, hb = carry
            rf = pl.multiple_of(g * SUBLANE, SUBLANE)
            a, b = scan8(af_ref[pl.ds(rf, SUBLANE), cs], bf_ref[pl.ds(rf, SUBLANE), cs], False)
            out = a * hf + b
            of_ref[pl.ds(rf, SUBLANE), cs] = out
            hf = jnp.broadcast_to(out[SUBLANE - 1:SUBLANE, :], (SUBLANE, lw))
            rb = pl.multiple_of((ng - 1 - g) * SUBLANE, SUBLANE)
            a, b = scan8(ab_ref[pl.ds(rb, SUBLANE), cs], bb_ref[pl.ds(rb, SUBLANE), cs], True)
            out = a * hb + b
            ob_ref[pl.ds(rb, SUBLANE), cs] = out
            hb = jnp.broadcast_to(out[0:1, :], (SUBLANE, lw))
            return hf, hb

        hf, hb = lax.fori_loop(0, ng, body, (h_scr[0, :, cs], h_scr[1, :, cs]))
        h_scr[0, :, cs] = hf
        h_scr[1, :, cs] = hb


def _lru_scan(a, b, rows):
    n = a.shape[1]
    W = LRU_WIDTH
    B = rows.B
    tb = rows.Lc
    nlb = rows.S // tb

    def blk_f(b_, j):
        return jnp.where(j == 0, b_, B + b_ * nlb + (j - 1))

    def blk_b(b_, j):
        return jnp.where(j == 0, b_, B + b_ * nlb + (nlb - j))

    def spec(d, blk):
        return pl.BlockSpec((None, tb, W), lambda b_, j: (d, blk(b_, j), 0))

    kern = functools.partial(_lru_scan_kernel, tb=tb, lw=512)
    return pl.pallas_call(
        kern,
        grid=(B, 1 + nlb),
        in_specs=[spec(0, blk_f), spec(0, blk_f), spec(1, blk_b), spec(1, blk_b)],
        out_specs=[pl.BlockSpec((tb, W), lambda b_, j: (blk_f(b_, j), 0)),
                   pl.BlockSpec((tb, W), lambda b_, j: (blk_b(b_, j), 0))],
        out_shape=[jax.ShapeDtypeStruct((n, W), F32)] * 2,
        scratch_shapes=[pltpu.VMEM((2, SUBLANE, W), F32)],
        compiler_params=_params(("arbitrary", "arbitrary")),
        name="lru_scan",
    )(a, b, a, b)


def _mla_qkv_kernel(cq_ref, ckv_ref, kr_ref, qg_ref, kvg_ref, wq_ref, wk_ref, wv_ref, qkg_ref, cos_ref, sin_ref,
                    q_ref, k_ref, v_ref):
    cqn = _rms(cq_ref[...], qg_ref[...]).astype(BF16)
    ckvn = _rms(ckv_ref[...], kvg_ref[...]).astype(BF16)
    q_all = jnp.dot(cqn, wq_ref[...], preferred_element_type=F32)
    kn_all = jnp.dot(ckvn, wk_ref[...], preferred_element_type=F32)
    v_all = jnp.dot(ckvn, wv_ref[...], preferred_element_type=F32).astype(BF16)
    ones = jnp.ones((v_all.shape[0], LANE), BF16)
    for h in range(MLA_HEADS):
        v_ref[:, h * (MLA_V + LANE):(h + 1) * (MLA_V + LANE)] = jnp.concatenate(
            [v_all[:, h * MLA_V:(h + 1) * MLA_V], ones], axis=1)
    kr = kr_ref[...]
    cos = cos_ref[...]
    sin = sin_ref[...]
    lane = lax.broadcasted_iota(jnp.int32, cos.shape, 1)
    first_half = (lane % (MLA_ROPE // 2)) < (MLA_ROPE // 4)
    hi_w = MLA_PAD - MLA_NOPE

    def norm_rope(lo, hi, g, scale):
        ssq = jnp.sum(lo * lo + hi * hi, axis=-1, keepdims=True)
        r = lax.rsqrt(ssq * (1.0 / MLA_QK) + EPS) * scale
        hi = hi * r * g[:, MLA_NOPE:]
        rot = jnp.where(first_half, pltpu.roll(hi, hi_w - MLA_ROPE // 4, axis=1),
                        pltpu.roll(hi, MLA_ROPE // 4, axis=1))
        return (lo * r * g[:, :MLA_NOPE]).astype(BF16), (hi * cos + rot * sin).astype(BF16)

    for h in range(MLA_HEADS):
        lo_sl = slice(h * MLA_PAD, h * MLA_PAD + MLA_NOPE)
        hi_sl = slice(h * MLA_PAD + MLA_NOPE, (h + 1) * MLA_PAD)
        q_ref[:, lo_sl], q_ref[:, hi_sl] = norm_rope(q_all[:, lo_sl], q_all[:, hi_sl], qkg_ref[0:1, :],
                                                     MLA_QK ** -0.5 * LOG2E)
        k_ref[:, lo_sl], k_ref[:, hi_sl] = norm_rope(kn_all[:, h * MLA_NOPE:(h + 1) * MLA_NOPE], kr,
                                                     qkg_ref[1:2, :], 1.0)


def _mla_qkv(p, q_norm_g, kv_norm_g, wq, wk, wv, qk_g, cos_t, sin_t, rows, *, tm):
    B, Lc, S = rows.B, rows.Lc, rows.S
    n = p.shape[0]
    nct = rows.n_ctx // tm
    cpb = Lc // tm
    lpb = S // tm

    def seq_blk(i):
        il = jnp.maximum(i - nct, 0)
        return (jnp.where(i < nct, i // cpb, il // lpb), jnp.where(i < nct, lpb + i % cpb, il % lpb), 0)

    def pos_blk(i):
        return (jnp.where(i < nct, i % cpb, cpb + jnp.maximum(i - nct, 0) % lpb), 0)

    full = lambda *s: pl.BlockSpec(s, lambda i: (0,) * len(s))
    qkw = MLA_HEADS * MLA_PAD
    vw = MLA_HEADS * MLA_V
    return pl.pallas_call(
        _mla_qkv_kernel,
        grid=(n // tm,),
        in_specs=[pl.BlockSpec((tm, MLA_Q_RANK), lambda i: (i, CD_CQ // MLA_Q_RANK)),
                  pl.BlockSpec((tm, MLA_KV_RANK), lambda i: (i, CD_CKV // MLA_KV_RANK)),
                  pl.BlockSpec((tm, LANE), lambda i: (i, CD_KR // LANE)),
                  full(1, MLA_Q_RANK), full(1, MLA_KV_RANK), full(MLA_Q_RANK, qkw),
                  full(MLA_KV_RANK, MLA_HEADS * MLA_NOPE), full(MLA_KV_RANK, vw), full(2, MLA_PAD),
                  pl.BlockSpec((tm, MLA_PAD - MLA_NOPE), pos_blk),
                  pl.BlockSpec((tm, MLA_PAD - MLA_NOPE), pos_blk)],
        out_specs=[pl.BlockSpec((None, tm, qkw), seq_blk), pl.BlockSpec((None, tm, qkw), seq_blk),
                   pl.BlockSpec((None, tm, vw + MLA_HEADS * LANE), seq_blk)],
        out_shape=[jax.ShapeDtypeStruct((B, Lc + S, qkw), BF16), jax.ShapeDtypeStruct((B, Lc + S, qkw), BF16),
                   jax.ShapeDtypeStruct((B, Lc + S, vw + MLA_HEADS * LANE), BF16)],
        compiler_params=_params(("arbitrary",)),
        name="mla_qkv",
    )(p, p, p, q_norm_g.reshape(1, -1), kv_norm_g.reshape(1, -1), wq, wk, wv, qk_g, cos_t, sin_t)


def _attn_kernel(q_ref, k_ref, v_ref, o_ref, *, sub):
    nt = (((1,), (1,)), ((), ()))
    for r in range(0, q_ref.shape[0], sub):
        s = lax.dot_general(q_ref[r:r + sub, :], k_ref[...], nt, preferred_element_type=F32)
        e = jnp.exp2(s - jnp.max(s, axis=-1, keepdims=True)).astype(BF16)
        ov = jnp.dot(e, v_ref[...], preferred_element_type=F32)
        o_ref[r:r + sub, :] = (ov[:, :MLA_V] / ov[:, MLA_V:MLA_V + 1]).astype(BF16)


def _attention(q, k, v, rows, *, tq):
    B, Lc, S = rows.B, rows.Lc, rows.S
    nq = S // tq
    return pl.pallas_call(
        functools.partial(_attn_kernel, sub=min(tq, 256)),
        grid=(B, MLA_HEADS, nq),
        in_specs=[pl.BlockSpec((None, tq, MLA_PAD), lambda b, h, i: (b, i, h)),
                  pl.BlockSpec((None, Lc + S, MLA_PAD), lambda b, h, i: (b, 0, h)),
                  pl.BlockSpec((None, Lc + S, MLA_V + LANE), lambda b, h, i: (b, 0, h))],
        out_specs=pl.BlockSpec((tq, MLA_V), lambda b, h, i: (b * nq + i, h)),
        out_shape=jax.ShapeDtypeStruct((B * S, MLA_HEADS * MLA_V), BF16),
        compiler_params=_params(("arbitrary", "arbitrary", "arbitrary")),
        name="attention",
    )(q, k, v)


def _gelu_tanh(x):
    return 0.5 * x * (1.0 + jnp.tanh(math.sqrt(2.0 / math.pi) * (x + 0.044715 * (x * x * x))))


def _merge_cd_kernel(x_ref, mod_ref, yb_ref, hf_ref, hb_ref, att_ref, w1_ref, w2_ref, o_ref):
    y1 = _gelu_tanh(yb_ref[...]) * (hf_ref[...] + hb_ref[...])
    y = (jnp.dot(y1.astype(BF16), w1_ref[...], preferred_element_type=F32)
         + jnp.dot(att_ref[...], w2_ref[...], preferred_element_type=F32))
    o_ref[...] = x_ref[...] + mod_ref[5:6, :] * y


def _merge_cd(x, mods, p, h_f, h_b, att, w_out, rows, *, tm):
    D = x.shape[1]
    W = LRU_WIDTH
    aw = MLA_HEADS * MLA_V
    off = rows.n_ctx // tm
    lat = lambda i: (i + off, 0)
    return pl.pallas_call(
        _merge_cd_kernel,
        grid=(rows.n_lat // tm,),
        in_specs=[pl.BlockSpec((tm, D), lat),
                  pl.BlockSpec((None, N_MOD, D), lambda i: (rows.mod_row(i, tm, False), 0, 0)),
                  pl.BlockSpec((tm, W), lambda i: (i + off, CD_YB // W)),
                  pl.BlockSpec((tm, W), lat), pl.BlockSpec((tm, W), lat),
                  pl.BlockSpec((tm, aw), lambda i: (i, 0)),
                  pl.BlockSpec((W, D), lambda i: (0, 0)),
                  pl.BlockSpec((aw, D), lambda i: (W // aw, 0))],
        out_specs=pl.BlockSpec((tm, D), lambda i: (i, 0)),
        out_shape=jax.ShapeDtypeStruct((rows.n_lat, D), F32),
        compiler_params=_params(("arbitrary",)),
        name="merge_cd",
    )(x, mods, p, h_f, h_b, att, w_out, w_out)


def _ret_tables(rows):
    half = RET_DK // 2
    freqs = ROPE_BASE ** (-jnp.arange(half, dtype=F32) / half)
    ang = jnp.arange(rows.S, dtype=F32)[:, None] * freqs
    cos = jnp.concatenate([jnp.cos(ang), jnp.cos(ang)], axis=-1)
    sin = jnp.concatenate([-jnp.sin(ang), jnp.sin(ang)], axis=-1)
    cos = jnp.concatenate([jnp.ones((rows.Lc, RET_DK), F32), cos], axis=0)
    sin = jnp.concatenate([jnp.zeros((rows.Lc, RET_DK), F32), sin], axis=0)
    return cos, sin


def _mla_tables(rows):
    S = rows.S
    quarter = MLA_ROPE // 4
    freqs = ROPE_BASE ** (-jnp.arange(quarter, dtype=F32) / quarter)
    t = jnp.arange(S)
    row = (t // GRID_W).astype(F32)
    col = (t % GRID_W).astype(F32)

    def part(pos):
        ang = pos[:, None] * freqs
        return (jnp.concatenate([jnp.cos(ang), jnp.cos(ang)], axis=-1),
                jnp.concatenate([-jnp.sin(ang), jnp.sin(ang)], axis=-1))

    rc, rs = part(row)
    cc, cs = part(col)
    tail = MLA_PAD - MLA_QK
    cos = jnp.concatenate([rc, cc, jnp.ones((S, tail), F32)], axis=-1)
    sin = jnp.concatenate([rs, cs, jnp.zeros((S, tail), F32)], axis=-1)
    cos = jnp.concatenate([jnp.ones((rows.Lc, MLA_ROPE + tail), F32), cos], axis=0)
    sin = jnp.concatenate([jnp.zeros((rows.Lc, MLA_ROPE + tail), F32), sin], axis=0)
    return cos, sin


def _pad_cols(w, n):
    return jnp.pad(w, ((0, 0), (0, n - w.shape[1])))


def _head_pad(w, heads, width, padded):
    K = w.shape[0]
    return jnp.pad(w.reshape(K, heads, width), ((0, 0), (0, 0), (0, padded - width))).reshape(K, heads * padded)


def kernel(x, c, ctx, c_ctx, ada_w, ada_b, norm_g, ffn_wg, ffn_wu, ffn_wd, ab_w_in, ab_w_out, ret_decay_logit, ret_gn_g, mlstm_gate_b, mlstm_gn_g, cd_w_in, cd_w_out, lru_conv_w, lru_conv_b, lru_wa, lru_ba, lru_wx, lru_bx, lru_lambda, mla_q_norm_g, mla_kv_norm_g, mla_w_uq, mla_w_uk, mla_w_uv, mla_qk_norm_g):
    B, S, D = x.shape
    Lc = ctx.shape[1]
    depth = ada_w.shape[0]
    F = ffn_wg.shape[-1]
    rows = Rows(B, Lc, S)
    assert B < SUBLANE and Lc % CHUNK == 0 and S % Lc == 0 and S % GRID_W == 0

    tm = min(1024, rows.n_ctx)
    tf = 512
    all_mod = lambda i: rows.mod_row(i, tm, True)
    lat_mod = lambda i: rows.mod_row(i, tm, False)

    cond = jnp.zeros((SUBLANE, D), F32).at[:B].set(c.astype(F32)).at[B].set(c_ctx.astype(F32))
    mods = _adaln(cond, ada_w, ada_b)

    wg = ffn_wg.astype(BF16)
    wu = ffn_wu.astype(BF16)
    wd = ffn_wd.astype(BF16)

    for l in range(depth):
        last = l == depth - 1
        j = l // 2
        m_l = mods[l]
        ffn1 = functools.partial(_ffn, mods=m_l, g=norm_g[l, 0], wg=wg, wu=wu, wd=wd, layer=l, half=0,
                                 mod_base=0, tm=tm, tf=tf)
        if l == 0:
            xa = ffn1(x.reshape(B * S, D).astype(F32), mod_of_tile=lat_mod, out_rows=rows.n_all,
                      out_tile0=rows.n_ctx // tm)
            xa = ffn1(ctx.reshape(B * Lc, D).astype(F32), mod_of_tile=lambda i: B, out_rows=rows.n_all, into=xa)
        else:
            xa = ffn1(xa, mod_of_tile=all_mod)

        if l % 2 == 0:
            w_in = _pad_cols(ab_w_in[j].astype(BF16), AB_PAD)
            p = _proj(xa, m_l, norm_g[l, 1], w_in, rows, tm=tm, tn=AB_PAD // 7)
            cos_t, sin_t = _ret_tables(rows)
            ret_f, ret_b = _retention(p, ret_decay_logit[j], cos_t, sin_t, rows)
            ml_f, ml_b = _mlstm(p, mlstm_gate_b[j], rows)
            xa = _merge_ab(xa, m_l, ret_f, ret_b, ml_f, ml_b, p, ret_gn_g[j], mlstm_gn_g[j],
                           ab_w_out[j].astype(BF16), rows, tm=min(256, tm))
            if last:
                xa = xa[rows.n_ctx:]
        else:
            assert last, "context outputs of the recurrent/attention mixer are not produced"
            w_in = _pad_cols(cd_w_in[j].astype(BF16), CD_PAD)
            p = _proj(xa, m_l, norm_g[l, 1], w_in, rows, tm=tm, tn=CD_PAD // 3)
            a_c, b_c = _lru_coef(p, lru_conv_w[j], lru_conv_b[j], lru_wa[j].astype(BF16), lru_wx[j].astype(BF16),
                                 lru_ba[j], lru_bx[j], lru_lambda[j], rows, tm=min(256, Lc))
            h_f, h_b = _lru_scan(a_c, b_c, rows)
            cos_t, sin_t = _mla_tables(rows)
            wq = _head_pad(mla_w_uq[j].astype(BF16), MLA_HEADS, MLA_QK, MLA_PAD)
            qk_g = jnp.pad(mla_qk_norm_g[j].astype(F32), ((0, 0), (0, MLA_PAD - MLA_QK)))
            q, k, v = _mla_qkv(p, mla_q_norm_g[j], mla_kv_norm_g[j], wq, mla_w_uk[j].astype(BF16),
                               mla_w_uv[j].astype(BF16), qk_g, cos_t, sin_t, rows, tm=min(256, Lc))
            att = _attention(q, k, v, rows, tq=min(2048, S))
            xa = _merge_cd(xa, m_l, p, h_f, h_b, att, cd_w_out[j].astype(BF16), rows, tm=min(256, tm))

        xa = _ffn(xa, m_l, norm_g[l, 2], wg, wu, wd, layer=l, half=1, mod_base=6,
                  mod_of_tile=lat_mod if last else all_mod, tm=tm, tf=tf)

    if xa.shape[0] != rows.n_lat:
        xa = xa[rows.n_ctx:]
    return xa.reshape(B, S, D).astype(x.dtype)
```

```python
import functools
import math

import jax
import jax.numpy as jnp
import numpy as np
from jax import lax
from jax.experimental import pallas as pl
from jax.experimental.pallas import tpu as pltpu

F32 = jnp.float32
BF16 = jnp.bfloat16

N_MOD = 9
EPS = 1e-6
CHUNK = 128
GRID_W = 64
ROPE_BASE = 10000.0
NEG_BIG = -1e30

RET_HEADS = 4
RET_DK = 128
RET_DV = 256
MLSTM_HEADS = 4
MLSTM_DK = 128
MLSTM_DV = 256
LRU_WIDTH = 1024
LRU_BLOCKS = 8
LRU_BLOCK = LRU_WIDTH // LRU_BLOCKS
LRU_C = 8.0
MLA_HEADS = 8
MLA_Q_RANK = 512
MLA_KV_RANK = 256
MLA_NOPE = 128
MLA_ROPE = 64
MLA_V = 128
MLA_QK = MLA_NOPE + MLA_ROPE
LOG2E = math.log2(math.e)
MLA_PAD = 256

AB_SPLITS = (RET_HEADS * RET_DK, RET_HEADS * RET_DK, RET_HEADS * RET_DV, RET_HEADS * RET_DV,
             MLSTM_HEADS * MLSTM_DK, MLSTM_HEADS * MLSTM_DK, MLSTM_HEADS * MLSTM_DV, MLSTM_HEADS * MLSTM_DV,
             4 * MLSTM_HEADS)
AB_IN = sum(AB_SPLITS)
CD_SPLITS = (LRU_WIDTH, LRU_WIDTH, MLA_Q_RANK, MLA_KV_RANK, MLA_ROPE)
CD_IN = sum(CD_SPLITS)

LANE = 128
SUBLANE = 8
VMEM_LIMIT = 56 * 1024 * 1024
FFN_VMEM_LIMIT = 60 * 1024 * 1024

AB_QR, AB_KR, AB_VR, AB_GR, AB_QM, AB_KM, AB_VM, AB_GM, AB_GATES = np.cumsum((0,) + AB_SPLITS[:-1]).tolist()
AB_PAD = 6272
CD_YB, CD_XB, CD_CQ, CD_CKV, CD_KR = np.cumsum((0,) + CD_SPLITS[:-1]).tolist()
CD_PAD = 3072


def _params(sem, vmem=VMEM_LIMIT):
    return pltpu.CompilerParams(dimension_semantics=sem, vmem_limit_bytes=vmem)


def _round_up(n, m):
    return (n + m - 1) // m * m


def _sigmoid(x):
    return 0.5 * jnp.tanh(0.5 * x) + 0.5


def _log_sigmoid(x):
    return jnp.minimum(x, 0.0) - jnp.log(1.0 + jnp.exp(-jnp.abs(x)))


def _rms(x, g):
    return x * lax.rsqrt(jnp.mean(x * x, axis=-1, keepdims=True) + EPS) * g


def _adaln_kernel(c_ref, w_ref, b_ref, o_ref):
    c = c_ref[...]
    s = (c * _sigmoid(c)).astype(BF16)
    o_ref[...] = jnp.dot(s, w_ref[...].astype(BF16), preferred_element_type=F32) + b_ref[...]


def _adaln(cond, ada_w, ada_b):
    L, D, N = ada_w.shape
    tn = 1024 if N % 1024 == 0 else N
    out = pl.pallas_call(
        _adaln_kernel,
        grid=(L, N // tn),
        in_specs=[pl.BlockSpec((SUBLANE, D), lambda l, n: (0, 0)),
                  pl.BlockSpec((None, D, tn), lambda l, n: (l, 0, n)),
                  pl.BlockSpec((None, 1, tn), lambda l, n: (l, 0, n))],
        out_specs=pl.BlockSpec((None, SUBLANE, tn), lambda l, n: (l, 0, n)),
        out_shape=jax.ShapeDtypeStruct((L, SUBLANE, N), F32),
        compiler_params=_params(("arbitrary", "arbitrary")),
        name="adaln",
    )(cond, ada_w, ada_b.reshape(L, 1, N))
    return out.reshape(L, SUBLANE, N_MOD, D)


class Rows:
    def __init__(self, B, Lc, S):
        self.B, self.Lc, self.S = B, Lc, S
        self.n_ctx = B * Lc
        self.n_lat = B * S
        self.n_all = self.n_ctx + self.n_lat

    def mod_row(self, tile, tm, with_ctx):
        if not with_ctx:
            return (tile * tm) // self.S
        nct = self.n_ctx // tm
        return jnp.where(tile < nct, self.B, (jnp.maximum(tile - nct, 0) * tm) // self.S)


def _ffn_kernel(x_ref, mod_ref, g_ref, wg_ref, wu_ref, wd_ref, o_ref, h_scr, *, mod_base, sub, last_cols):
    f = pl.program_id(1)
    nf = pl.num_programs(1)
    tm = x_ref.shape[0]
    tf = wg_ref.shape[1]

    def step(cols, first, final):
        if first:
            shift = mod_ref[mod_base:mod_base + 1, :]
            scale = 1.0 + mod_ref[mod_base + 1:mod_base + 2, :]
        if final:
            gate = 0.5 * mod_ref[mod_base + 2:mod_base + 3, :]
        for r in range(0, tm, sub):
            rs = slice(r, r + sub)
            if first:
                h = (_rms(x_ref[rs, :], g_ref[...]) * scale + shift).astype(BF16)
                h_scr[rs, :] = h
            else:
                h = h_scr[rs, :]
            a = jnp.dot(h, wg_ref[:, :cols], preferred_element_type=F32)
            u = jnp.dot(h, wu_ref[:, :cols], preferred_element_type=F32)
            act = (a * _sigmoid(a) * u).astype(BF16)
            part = jnp.dot(act, wd_ref[:cols, :], preferred_element_type=F32)
            if first:
                o_ref[rs, :] = part
            elif final:
                o_ref[rs, :] = x_ref[rs, :] + gate * (o_ref[rs, :] + part)
            else:
                o_ref[rs, :] += part

    pl.when(f == 0)(functools.partial(step, tf, True, False))
    pl.when(jnp.logical_and(f > 0, f < nf - 1))(functools.partial(step, tf, False, False))
    pl.when(f == nf - 1)(functools.partial(step, last_cols, False, True))


def _ffn_kernel_into(x_ref, mod_ref, g_ref, wg_ref, wu_ref, wd_ref, into_ref, o_ref, h_scr, **kw):
    del into_ref
    _ffn_kernel(x_ref, mod_ref, g_ref, wg_ref, wu_ref, wd_ref, o_ref, h_scr, **kw)


def _ffn(x, mods, g, wg, wu, wd, *, layer, half, mod_base, mod_of_tile, tm, tf, out_rows=None, out_tile0=0,
         into=None):
    n, D = x.shape
    F = wg.shape[-1]
    nf = pl.cdiv(F, tf)
    assert nf >= 2 and n % tm == 0
    out_rows = n if out_rows is None else out_rows
    kw = dict(mod_base=mod_base, sub=min(tm, 512), last_cols=F - (nf - 1) * tf)
    in_specs = [pl.BlockSpec((tm, D), lambda i, f: (i, 0)),
                pl.BlockSpec((None, N_MOD, D), lambda i, f: (mod_of_tile(i), 0, 0)),
                pl.BlockSpec((1, D), lambda i, f: (0, 0)),
                pl.BlockSpec((None, None, D, tf), lambda i, f: (layer, half, 0, f)),
                pl.BlockSpec((None, None, D, tf), lambda i, f: (layer, half, 0, f)),
                pl.BlockSpec((None, None, tf, D), lambda i, f: (layer, half, f, 0))]
    args = [x, mods, g.reshape(1, D), wg, wu, wd]
    if into is None:
        kern, aliases = functools.partial(_ffn_kernel, **kw), {}
    else:
        assert into.shape == (out_rows, D)
        kern, aliases = functools.partial(_ffn_kernel_into, **kw), {len(args): 0}
        in_specs.append(pl.BlockSpec(memory_space=pl.ANY))
        args.append(into)
    return pl.pallas_call(
        kern,
        grid=(n // tm, nf),
        in_specs=in_specs,
        out_specs=pl.BlockSpec((tm, D), lambda i, f: (out_tile0 + i, 0)),
        out_shape=jax.ShapeDtypeStruct((out_rows, D), F32),
        scratch_shapes=[pltpu.VMEM((tm, D), BF16)],
        input_output_aliases=aliases,
        compiler_params=_params(("arbitrary", "arbitrary"), FFN_VMEM_LIMIT),
        name="ffn",
    )(*args)


def _proj_kernel(x_ref, mod_ref, g_ref, w_ref, o_ref, h_scr, *, sub):
    tm = x_ref.shape[0]

    @pl.when(pl.program_id(1) == 0)
    def _():
        scale = 1.0 + mod_ref[4:5, :]
        for r in range(0, tm, sub):
            h = (_rms(x_ref[r:r + sub, :], g_ref[...]) * scale + mod_ref[3:4, :]).astype(BF16)
            h_scr[r:r + sub, :] = h
            o_ref[r:r + sub, :] = jnp.dot(h, w_ref[...], preferred_element_type=F32)

    @pl.when(pl.program_id(1) != 0)
    def _():
        for r in range(0, tm, sub):
            o_ref[r:r + sub, :] = jnp.dot(h_scr[r:r + sub, :], w_ref[...], preferred_element_type=F32)


def _proj(x, mods, g, w, rows, *, tm, tn):
    n, D = x.shape
    Np = w.shape[1]
    return pl.pallas_call(
        functools.partial(_proj_kernel, sub=min(tm, 512)),
        grid=(n // tm, Np // tn),
        in_specs=[pl.BlockSpec((tm, D), lambda i, j: (i, 0)),
                  pl.BlockSpec((None, N_MOD, D), lambda i, j: (rows.mod_row(i, tm, True), 0, 0)),
                  pl.BlockSpec((1, D), lambda i, j: (0, 0)),
                  pl.BlockSpec((D, tn), lambda i, j: (0, j))],
        out_specs=pl.BlockSpec((tm, tn), lambda i, j: (i, j)),
        out_shape=jax.ShapeDtypeStruct((n, Np), F32),
        scratch_shapes=[pltpu.VMEM((tm, D), BF16)],
        compiler_params=_params(("arbitrary", "arbitrary")),
        name="proj",
    )(x, mods, g.reshape(1, D), w)


def _chunk_maps(rows):
    B = rows.B
    nc = rows.Lc // CHUNK
    nl = rows.S // CHUNK

    def seq_f(j):
        return j

    def seq_b(j):
        return jnp.where(j < nc, nc - 1 - j, nc + nl - 1 - (j - nc))

    def unit(b, s):
        return jnp.where(s < nc, b * nc + s, B * nc + b * nl + (s - nc))

    return nc + nl, seq_f, seq_b, unit


def _rope128(x, cos, sin):
    return x * cos + pltpu.roll(x, 64, axis=1) * sin


def _tri_masks():
    r = lax.broadcasted_iota(jnp.int32, (CHUNK, CHUNK), 0)
    c = lax.broadcasted_iota(jnp.int32, (CHUNK, CHUNK), 1)
    return r - c


def _ret_kernel(dl_ref, qf_ref, kf_ref, vf_ref, cf_ref, sf_ref, qb_ref, kb_ref, vb_ref, cb_ref, sb_ref,
                of_ref, ob_ref, s_scr):
    j = pl.program_id(1)

    @pl.when(j == 0)
    def _():
        s_scr[...] = jnp.zeros_like(s_scr)

    lg_all = _log_sigmoid(dl_ref[...])
    rel = _tri_masks().astype(F32)
    pos_c = lax.broadcasted_iota(jnp.int32, (CHUNK, 1), 0).astype(F32)
    dirs = ((qf_ref, kf_ref, vf_ref, cf_ref, sf_ref, of_ref, False),
            (qb_ref, kb_ref, vb_ref, cb_ref, sb_ref, ob_ref, True))
    for d, (q_ref, k_ref, v_ref, c_ref, sn_ref, o_ref, rev) in enumerate(dirs):
        cos = c_ref[...]
        sin = sn_ref[...]
        dist = -rel if rev else rel
        step = (CHUNK - 1.0 - pos_c) if rev else pos_c
        for h in range(RET_HEADS):
            r = d * RET_HEADS + h
            lg = lg_all[r:r + 1, 0:1]
            q = _rope128(q_ref[:, h * RET_DK:(h + 1) * RET_DK], cos, sin)
            k = _rope128(k_ref[:, h * RET_DK:(h + 1) * RET_DK], cos, sin) * (RET_DK ** -0.5)
            v = v_ref[:, h * RET_DV:(h + 1) * RET_DV].astype(BF16)
            qb16 = q.astype(BF16)
            decay = jnp.where(dist >= 0, jnp.exp(jnp.maximum(dist, 0.0) * lg), 0.0)
            sc = lax.dot_general(qb16, k.astype(BF16), (((1,), (1,)), ((), ())), preferred_element_type=F32)
            inner = jnp.dot((sc * decay).astype(BF16), v, preferred_element_type=F32)
            s_prev = s_scr[r]
            xi = jnp.exp((step + 1.0) * lg)
            cross = jnp.dot(qb16, s_prev.astype(BF16), preferred_element_type=F32) * xi
            o_ref[:, h * RET_DV:(h + 1) * RET_DV] = inner + cross
            zeta = jnp.exp((CHUNK - 1.0 - step) * lg)
            kz_t = jnp.transpose(k * zeta).astype(BF16)
            u = jnp.dot(kz_t, v, preferred_element_type=F32)
            s_scr[r] = jnp.exp(CHUNK * lg) * s_prev + u


def _retention(p, decay_logit, cos_t, sin_t, rows):
    n = p.shape[0]
    B = rows.B
    n_steps, seq_f, seq_b, unit = _chunk_maps(rows)
    qw = RET_HEADS * RET_DK
    vw = RET_HEADS * RET_DV
    dl = jnp.broadcast_to(decay_logit.astype(F32).reshape(2 * RET_HEADS, 1), (2 * RET_HEADS, LANE))

    def in_specs(seq):
        return [pl.BlockSpec((CHUNK, qw), lambda b, j: (unit(b, seq(j)), AB_QR // qw)),
                pl.BlockSpec((CHUNK, qw), lambda b, j: (unit(b, seq(j)), AB_KR // qw)),
                pl.BlockSpec((CHUNK, vw), lambda b, j: (unit(b, seq(j)), AB_VR // vw)),
                pl.BlockSpec((CHUNK, RET_DK), lambda b, j: (seq(j), 0)),
                pl.BlockSpec((CHUNK, RET_DK), lambda b, j: (seq(j), 0))]

    def out_spec(seq):
        return pl.BlockSpec((CHUNK, vw), lambda b, j: (unit(b, seq(j)), 0))

    return pl.pallas_call(
        _ret_kernel,
        grid=(B, n_steps),
        in_specs=[pl.BlockSpec((2 * RET_HEADS, LANE), lambda b, j: (0, 0))] + in_specs(seq_f) + in_specs(seq_b),
        out_specs=[out_spec(seq_f), out_spec(seq_b)],
        out_shape=[jax.ShapeDtypeStruct((n, vw), F32)] * 2,
        scratch_shapes=[pltpu.VMEM((2 * RET_HEADS, RET_DK, RET_DV), F32)],
        compiler_params=_params(("arbitrary", "arbitrary")),
        name="retention",
    )(dl, p, p, p, cos_t, sin_t, p, p, p, cos_t, sin_t)


def _scan_lanes(x, reverse):
    n = x.shape[-1]
    lane = lax.broadcasted_iota(jnp.int32, x.shape, 1)
    d = 1
    while d < n:
        if reverse:
            x = x + jnp.where(lane < n - d, pltpu.roll(x, n - d, axis=1), 0.0)
        else:
            x = x + jnp.where(lane >= d, pltpu.roll(x, d, axis=1), 0.0)
        d *= 2
    return x


def _mlstm_kernel(gb_ref, qf_ref, kf_ref, vf_ref, gf_ref, qb_ref, kb_ref, vb_ref, gbk_ref,
                  of_ref, ob_ref, c_scr, m_scr):
    j = pl.program_id(1)
    H = MLSTM_HEADS
    dv = MLSTM_DV
    ext = dv + LANE

    @pl.when(j == 0)
    def _():
        c_scr[...] = jnp.zeros_like(c_scr)
        m_scr[...] = jnp.full_like(m_scr, NEG_BIG)

    rel = _tri_masks()
    lower = (rel >= 0).astype(F32)
    upper = (rel <= 0).astype(F32)
    exact = dict(precision=lax.Precision.HIGHEST, preferred_element_type=F32)
    ones = jnp.ones((CHUNK, LANE), BF16)
    dirs = ((qf_ref, kf_ref, vf_ref, gf_ref, of_ref, False),
            (qb_ref, kb_ref, vb_ref, gbk_ref, ob_ref, True))
    for d, (q_ref, k_ref, v_ref, g_ref, o_ref, rev) in enumerate(dirs):
        gates = g_ref[...] + gb_ref[...]
        cs_col = jnp.dot(upper if rev else lower, _log_sigmoid(gates), **exact)
        g8 = jnp.transpose(gates)[2 * d * H:2 * d * H + SUBLANE, :]
        cs = jnp.dot(_log_sigmoid(g8), lower if rev else upper, **exact)
        mask = (rel <= 0) if rev else (rel >= 0)
        for h in range(H):
            r = d * H + h
            i_row = g8[h:h + 1, :]
            b_row = cs[H + h:H + h + 1, :]
            b_col = cs_col[:, 2 * d * H + H + h:2 * d * H + H + h + 1]
            q = (q_ref[:, h * MLSTM_DK:(h + 1) * MLSTM_DK] * (MLSTM_DK ** -0.5)).astype(BF16)
            k = k_ref[:, h * MLSTM_DK:(h + 1) * MLSTM_DK]
            v_ext = jnp.concatenate([v_ref[:, h * dv:(h + 1) * dv].astype(BF16), ones], axis=1)
            m_prev = m_scr[r:r + 1, 0:1]
            c_prev = c_scr[r]

            log_d = jnp.where(mask, b_col + (i_row - b_row), -jnp.inf)
            m_intra = jnp.max(log_d, axis=1, keepdims=True)
            log_inter = b_col + m_prev
            m_t = jnp.maximum(log_inter, m_intra)
            dmat = jnp.exp(log_d - m_t)
            s = lax.dot_general(q, k.astype(BF16), (((1,), (1,)), ((), ())), preferred_element_type=F32) * dmat
            inter = jnp.exp(log_inter - m_t)
            tot = (jnp.dot(s.astype(BF16), v_ext, preferred_element_type=F32)
                   + jnp.dot(q, c_prev.astype(BF16), preferred_element_type=F32) * inter)
            den = tot[:, dv:dv + 1]
            o_ref[:, h * dv:(h + 1) * dv] = tot[:, :dv] / jnp.maximum(jnp.abs(den), jnp.exp(-m_t))

            b_last = b_row[:, 0:1] if rev else b_row[:, CHUNK - 1:CHUNK]
            log_w = b_last - b_row + i_row
            m_loc = jnp.max(log_w, axis=1, keepdims=True)
            kw_t = (jnp.transpose(k) * jnp.exp(log_w - m_loc)).astype(BF16)
            u = jnp.dot(kw_t, v_ext, preferred_element_type=F32)
            m_new = jnp.maximum(b_last + m_prev, m_loc)
            c_scr[r] = jnp.exp(b_last + m_prev - m_new) * c_prev + jnp.exp(m_loc - m_new) * u
            m_scr[r:r + 1, :] = jnp.broadcast_to(m_new, (1, LANE))


def _mlstm(p, gate_b, rows):
    n = p.shape[0]
    B = rows.B
    H = MLSTM_HEADS
    n_steps, seq_f, seq_b, unit = _chunk_maps(rows)
    qw = H * MLSTM_DK
    vw = H * MLSTM_DV
    gb = jnp.zeros((1, LANE), F32).at[0, :4 * H].set(gate_b.astype(F32).reshape(4 * H))

    def in_specs(seq):
        return [pl.BlockSpec((CHUNK, qw), lambda b, j: (unit(b, seq(j)), AB_QM // qw)),
                pl.BlockSpec((CHUNK, qw), lambda b, j: (unit(b, seq(j)), AB_KM // qw)),
                pl.BlockSpec((CHUNK, vw), lambda b, j: (unit(b, seq(j)), AB_VM // vw)),
                pl.BlockSpec((CHUNK, LANE), lambda b, j: (unit(b, seq(j)), AB_GATES // LANE))]

    def out_spec(seq):
        return pl.BlockSpec((CHUNK, vw), lambda b, j: (unit(b, seq(j)), 0))

    return pl.pallas_call(
        _mlstm_kernel,
        grid=(B, n_steps),
        in_specs=[pl.BlockSpec((1, LANE), lambda b, j: (0, 0))] + in_specs(seq_f) + in_specs(seq_b),
        out_specs=[out_spec(seq_f), out_spec(seq_b)],
        out_shape=[jax.ShapeDtypeStruct((n, vw), F32)] * 2,
        scratch_shapes=[pltpu.VMEM((2 * H, MLSTM_DK, MLSTM_DV + LANE), F32),
                        pltpu.VMEM((2 * SUBLANE, LANE), F32)],
        compiler_params=_params(("arbitrary", "arbitrary")),
        name="mlstm",
    )(gb, p, p, p, p, p, p, p, p)


def _head_ln(y, g, heads, width):
    outs = []
    for h in range(heads):
        yh = y[:, h * width:(h + 1) * width]
        mu = jnp.mean(yh, axis=-1, keepdims=True)
        yc = yh - mu
        var = jnp.mean(yc * yc, axis=-1, keepdims=True)
        outs.append(yc * lax.rsqrt(var + EPS))
    return jnp.concatenate(outs, axis=1) * g


def _merge_ab_kernel(x_ref, mod_ref, rf_ref, rb_ref, mf_ref, mb_ref, gr_ref, gm_ref, rg_ref, mg_ref,
                     w1_ref, w2_ref, o_ref):
    gr = gr_ref[...]
    ret_y = (gr * _sigmoid(gr)) * _head_ln(rf_ref[...] + rb_ref[...], rg_ref[...], RET_HEADS, RET_DV)
    ml_y = _sigmoid(gm_ref[...]) * _head_ln(mf_ref[...] + mb_ref[...], mg_ref[...], MLSTM_HEADS, MLSTM_DV)
    y = (jnp.dot(ret_y.astype(BF16), w1_ref[...], preferred_element_type=F32)
         + jnp.dot(ml_y.astype(BF16), w2_ref[...], preferred_element_type=F32))
    o_ref[...] = x_ref[...] + mod_ref[5:6, :] * y


def _merge_ab(x, mods, ret_f, ret_b, ml_f, ml_b, p, ret_g, ml_g, w_out, rows, *, tm):
    n, D = x.shape
    rw = RET_HEADS * RET_DV
    mw = MLSTM_HEADS * MLSTM_DV
    row = lambda i: (i, 0)
    return pl.pallas_call(
        _merge_ab_kernel,
        grid=(n // tm,),
        in_specs=[pl.BlockSpec((tm, D), row),
                  pl.BlockSpec((None, N_MOD, D), lambda i: (rows.mod_row(i, tm, True), 0, 0)),
                  pl.BlockSpec((tm, rw), row), pl.BlockSpec((tm, rw), row),
                  pl.BlockSpec((tm, mw), row), pl.BlockSpec((tm, mw), row),
                  pl.BlockSpec((tm, rw), lambda i: (i, AB_GR // rw)),
                  pl.BlockSpec((tm, mw), lambda i: (i, AB_GM // mw)),
                  pl.BlockSpec((1, rw), lambda i: (0, 0)),
                  pl.BlockSpec((1, mw), lambda i: (0, 0)),
                  pl.BlockSpec((rw, D), lambda i: (0, 0)),
                  pl.BlockSpec((mw, D), lambda i: (rw // mw, 0))],
        out_specs=pl.BlockSpec((tm, D), row),
        out_shape=jax.ShapeDtypeStruct((n, D), F32),
        compiler_params=_params(("arbitrary",)),
        name="merge_ab",
    )(x, mods, ret_f, ret_b, ml_f, ml_b, p, p, ret_g.reshape(1, rw), ml_g.reshape(1, mw), w_out, w_out)


def _lru_coef_kernel(x_ref, xp_ref, xn_ref, cw_ref, cb_ref, wa_ref, wx_ref, ba_ref, bx_ref, lam_ref,
                     a_ref, b_ref, *, tm, ctx_tiles, ctx_seg, lat_seg):
    i = pl.program_id(0)
    seg_pos = jnp.where(i < ctx_tiles, i % ctx_seg, (i - ctx_tiles) % lat_seg)
    seg_len = jnp.where(i < ctx_tiles, ctx_seg, lat_seg)
    keep_prev = (seg_pos != 0).astype(F32)
    keep_next = (seg_pos != seg_len - 1).astype(F32)
    xe = jnp.concatenate([xp_ref[...] * keep_prev, x_ref[...], xn_ref[...] * keep_next], axis=0)
    ne = tm + 2 * SUBLANE
    xc = cb_ref[...] + cw_ref[2:3, :] * x_ref[...]
    for tap, off in ((0, -2), (1, -1), (3, 1)):
        shifted = pltpu.roll(xe, (-off) % ne, axis=0)[SUBLANE:SUBLANE + tm, :]
        xc = xc + cw_ref[tap:tap + 1, :] * shifted
    for d in range(2):
        lam = lam_ref[d:d + 1, :]
        sp = jnp.maximum(-lam, 0.0) + jnp.log(1.0 + jnp.exp(-jnp.abs(lam)))
        rate = (-LRU_C * LOG2E) * sp
        for g in range(LRU_BLOCKS):
            sl = slice(g * LRU_BLOCK, (g + 1) * LRU_BLOCK)
            xg = xc[:, sl]
            xg16 = xg.astype(BF16)
            r = _sigmoid(jnp.dot(xg16, wa_ref[d, g], preferred_element_type=F32) + ba_ref[d:d + 1, sl])
            ig = _sigmoid(jnp.dot(xg16, wx_ref[d, g], preferred_element_type=F32) + bx_ref[d:d + 1, sl])
            a = jnp.exp2(r * rate[:, sl])
            a_ref[d, :, sl] = a
            b_ref[d, :, sl] = jnp.sqrt(1.0 - a * a) * (ig * xg)


def _lru_coef(p, conv_w, conv_b, wa, wx, ba, bx, lam, rows, *, tm):
    n = p.shape[0]
    W = LRU_WIDTH
    tpb = tm // SUBLANE
    n8 = n // SUBLANE
    kern = functools.partial(_lru_coef_kernel, tm=tm, ctx_tiles=rows.n_ctx // tm,
                             ctx_seg=rows.Lc // tm, lat_seg=rows.S // tm)
    full = lambda *s: pl.BlockSpec(s, lambda i: (0,) * len(s))
    return pl.pallas_call(
        kern,
        grid=(n // tm,),
        in_specs=[pl.BlockSpec((tm, W), lambda i: (i, CD_XB // W)),
                  pl.BlockSpec((SUBLANE, W), lambda i: (jnp.maximum(i * tpb - 1, 0), CD_XB // W)),
                  pl.BlockSpec((SUBLANE, W), lambda i: (jnp.minimum((i + 1) * tpb, n8 - 1), CD_XB // W)),
                  full(4, W), full(1, W), full(2, LRU_BLOCKS, LRU_BLOCK, LRU_BLOCK),
                  full(2, LRU_BLOCKS, LRU_BLOCK, LRU_BLOCK), full(2, W), full(2, W), full(2, W)],
        out_specs=[pl.BlockSpec((2, tm, W), lambda i: (0, i, 0))] * 2,
        out_shape=[jax.ShapeDtypeStruct((2, n, W), F32)] * 2,
        compiler_params=_params(("arbitrary",)),
        name="lru_coef",
    )(p, p, p, conv_w, conv_b.reshape(1, W), wa, wx, ba, bx, lam)


def _lru_scan_kernel(af_ref, bf_ref, ab_ref, bb_ref, of_ref, ob_ref, h_scr, *, tb, lw):
    @pl.when(pl.program_id(1) == 0)
    def _():
        h_scr[...] = jnp.zeros_like(h_scr)

    row = lax.broadcasted_iota(jnp.int32, (SUBLANE, lw), 0)
    ng = tb // SUBLANE

    def scan8(a, b, rev):
        d = 1
        while d < SUBLANE:
            if rev:
                keep = row < SUBLANE - d
                sh = SUBLANE - d
            else:
                keep = row >= d
                sh = d
            b = a * jnp.where(keep, pltpu.roll(b, sh, axis=0), 0.0) + b
            a = a * jnp.where(keep, pltpu.roll(a, sh, axis=0), 1.0)
            d *= 2
        return a, b

    for c in range(LRU_WIDTH // lw):
        cs = slice(c * lw, (c + 1) * lw)

        def body(g, carry):
            hf, hb = carry
            rf = pl.multiple_of(g * SUBLANE, SUBLANE)
            a, b = scan8(af_ref[pl.ds(rf, SUBLANE), cs], bf_ref[pl.ds(rf, SUBLANE), cs], False)
            out = a * hf + b
            of_ref[pl.ds(rf, SUBLANE), cs] = out
            hf = jnp.broadcast_to(out[SUBLANE - 1:SUBLANE, :], (SUBLANE, lw))
            rb = pl.multiple_of((ng - 1 - g) * SUBLANE, SUBLANE)
            a, b = scan8(ab_ref[pl.ds(rb, SUBLANE), cs], bb_ref[pl.ds(rb, SUBLANE), cs], True)
            out = a * hb + b
            ob_ref[pl.ds(rb, SUBLANE), cs] = out
            hb = jnp.broadcast_to(out[0:1, :], (SUBLANE, lw))
            return hf, hb

        hf, hb = lax.fori_loop(0, ng, body, (h_scr[0, :, cs], h_scr[1, :, cs]))
        h_scr[0, :, cs] = hf
        h_scr[1, :, cs] = hb


def _lru_scan(a, b, rows):
    n = a.shape[1]
    W = LRU_WIDTH
    B = rows.B
    tb = rows.Lc
    nlb = rows.S // tb

    def blk_f(b_, j):
        return jnp.where(j == 0, b_, B + b_ * nlb + (j - 1))

    def blk_b(b_, j):
        return jnp.where(j == 0, b_, B + b_ * nlb + (nlb - j))

    def spec(d, blk):
        return pl.BlockSpec((None, tb, W), lambda b_, j: (d, blk(b_, j), 0))

    kern = functools.partial(_lru_scan_kernel, tb=tb, lw=512)
    return pl.pallas_call(
        kern,
        grid=(B, 1 + nlb),
        in_specs=[spec(0, blk_f), spec(0, blk_f), spec(1, blk_b), spec(1, blk_b)],
        out_specs=[pl.BlockSpec((tb, W), lambda b_, j: (blk_f(b_, j), 0)),
                   pl.BlockSpec((tb, W), lambda b_, j: (blk_b(b_, j), 0))],
        out_shape=[jax.ShapeDtypeStruct((n, W), F32)] * 2,
        scratch_shapes=[pltpu.VMEM((2, SUBLANE, W), F32)],
        compiler_params=_params(("arbitrary", "arbitrary")),
        name="lru_scan",
    )(a, b, a, b)


def _mla_qkv_kernel(cq_ref, ckv_ref, kr_ref, qg_ref, kvg_ref, wq_ref, wk_ref, wv_ref, qkg_ref, cos_ref, sin_ref,
                    q_ref, k_ref, v_ref):
    cqn = _rms(cq_ref[...], qg_ref[...]).astype(BF16)
    ckvn = _rms(ckv_ref[...], kvg_ref[...]).astype(BF16)
    q_all = jnp.dot(cqn, wq_ref[...], preferred_element_type=F32)
    kn_all = jnp.dot(ckvn, wk_ref[...], preferred_element_type=F32)
    v_all = jnp.dot(ckvn, wv_ref[...], preferred_element_type=F32).astype(BF16)
    ones = jnp.ones((v_all.shape[0], LANE), BF16)
    for h in range(MLA_HEADS):
        v_ref[:, h * (MLA_V + LANE):(h + 1) * (MLA_V + LANE)] = jnp.concatenate(
            [v_all[:, h * MLA_V:(h + 1) * MLA_V], ones], axis=1)
    kr = kr_ref[...]
    cos = cos_ref[...]
    sin = sin_ref[...]
    lane = lax.broadcasted_iota(jnp.int32, cos.shape, 1)
    first_half = (lane % (MLA_ROPE // 2)) < (MLA_ROPE // 4)
    hi_w = MLA_PAD - MLA_NOPE

    def norm_rope(lo, hi, g, scale):
        ssq = jnp.sum(lo * lo + hi * hi, axis=-1, keepdims=True)
        r = lax.rsqrt(ssq * (1.0 / MLA_QK) + EPS) * scale
        hi = hi * r * g[:, MLA_NOPE:]
        rot = jnp.where(first_half, pltpu.roll(hi, hi_w - MLA_ROPE // 4, axis=1),
                        pltpu.roll(hi, MLA_ROPE // 4, axis=1))
        return (lo * r * g[:, :MLA_NOPE]).astype(BF16), (hi * cos + rot * sin).astype(BF16)

    for h in range(MLA_HEADS):
        lo_sl = slice(h * MLA_PAD, h * MLA_PAD + MLA_NOPE)
        hi_sl = slice(h * MLA_PAD + MLA_NOPE, (h + 1) * MLA_PAD)
        q_ref[:, lo_sl], q_ref[:, hi_sl] = norm_rope(q_all[:, lo_sl], q_all[:, hi_sl], qkg_ref[0:1, :],
                                                     MLA_QK ** -0.5 * LOG2E)
        k_ref[:, lo_sl], k_ref[:, hi_sl] = norm_rope(kn_all[:, h * MLA_NOPE:(h + 1) * MLA_NOPE], kr,
                                                     qkg_ref[1:2, :], 1.0)


def _mla_qkv(p, q_norm_g, kv_norm_g, wq, wk, wv, qk_g, cos_t, sin_t, rows, *, tm):
    B, Lc, S = rows.B, rows.Lc, rows.S
    n = p.shape[0]
    nct = rows.n_ctx // tm
    cpb = Lc // tm
    lpb = S // tm

    def seq_blk(i):
        il = jnp.maximum(i - nct, 0)
        return (jnp.where(i < nct, i // cpb, il // lpb), jnp.where(i < nct, lpb + i % cpb, il % lpb), 0)

    def pos_blk(i):
        return (jnp.where(i < nct, i % cpb, cpb + jnp.maximum(i - nct, 0) % lpb), 0)

    full = lambda *s: pl.BlockSpec(s, lambda i: (0,) * len(s))
    qkw = MLA_HEADS * MLA_PAD
    vw = MLA_HEADS * MLA_V
    return pl.pallas_call(
        _mla_qkv_kernel,
        grid=(n // tm,),
        in_specs=[pl.BlockSpec((tm, MLA_Q_RANK), lambda i: (i, CD_CQ // MLA_Q_RANK)),
                  pl.BlockSpec((tm, MLA_KV_RANK), lambda i: (i, CD_CKV // MLA_KV_RANK)),
                  pl.BlockSpec((tm, LANE), lambda i: (i, CD_KR // LANE)),
                  full(1, MLA_Q_RANK), full(1, MLA_KV_RANK), full(MLA_Q_RANK, qkw),
                  full(MLA_KV_RANK, MLA_HEADS * MLA_NOPE), full(MLA_KV_RANK, vw), full(2, MLA_PAD),
                  pl.BlockSpec((tm, MLA_PAD - MLA_NOPE), pos_blk),
                  pl.BlockSpec((tm, MLA_PAD - MLA_NOPE), pos_blk)],
        out_specs=[pl.BlockSpec((None, tm, qkw), seq_blk), pl.BlockSpec((None, tm, qkw), seq_blk),
                   pl.BlockSpec((None, tm, vw + MLA_HEADS * LANE), seq_blk)],
        out_shape=[jax.ShapeDtypeStruct((B, Lc + S, qkw), BF16), jax.ShapeDtypeStruct((B, Lc + S, qkw), BF16),
                   jax.ShapeDtypeStruct((B, Lc + S, vw + MLA_HEADS * LANE), BF16)],
        compiler_params=_params(("arbitrary",)),
        name="mla_qkv",
    )(p, p, p, q_norm_g.reshape(1, -1), kv_norm_g.reshape(1, -1), wq, wk, wv, qk_g, cos_t, sin_t)


def _attn_kernel(q_ref, k_ref, v_ref, o_ref, *, sub):
    nt = (((1,), (1,)), ((), ()))
    for r in range(0, q_ref.shape[0], sub):
        s = lax.dot_general(q_ref[r:r + sub, :], k_ref[...], nt, preferred_element_type=F32)
        e = jnp.exp2(s - jnp.max(s, axis=-1, keepdims=True)).astype(BF16)
        ov = jnp.dot(e, v_ref[...], preferred_element_type=F32)
        o_ref[r:r + sub, :] = (ov[:, :MLA_V] / ov[:, MLA_V:MLA_V + 1]).astype(BF16)


def _attention(q, k, v, rows, *, tq):
    B, Lc, S = rows.B, rows.Lc, rows.S
    nq = S // tq
    return pl.pallas_call(
        functools.partial(_attn_kernel, sub=min(tq, 256)),
        grid=(B, MLA_HEADS, nq),
        in_specs=[pl.BlockSpec((None, tq, MLA_PAD), lambda b, h, i: (b, i, h)),
                  pl.BlockSpec((None, Lc + S, MLA_PAD), lambda b, h, i: (b, 0, h)),
                  pl.BlockSpec((None, Lc + S, MLA_V + LANE), lambda b, h, i: (b, 0, h))],
        out_specs=pl.BlockSpec((tq, MLA_V), lambda b, h, i: (b * nq + i, h)),
        out_shape=jax.ShapeDtypeStruct((B * S, MLA_HEADS * MLA_V), BF16),
        compiler_params=_params(("arbitrary", "arbitrary", "arbitrary")),
        name="attention",
    )(q, k, v)


def _gelu_tanh(x):
    return 0.5 * x * (1.0 + jnp.tanh(math.sqrt(2.0 / math.pi) * (x + 0.044715 * (x * x * x))))


def _merge_cd_kernel(x_ref, mod_ref, yb_ref, hf_ref, hb_ref, att_ref, w1_ref, w2_ref, o_ref):
    y1 = _gelu_tanh(yb_ref[...]) * (hf_ref[...] + hb_ref[...])
    y = (jnp.dot(y1.astype(BF16), w1_ref[...], preferred_element_type=F32)
         + jnp.dot(att_ref[...], w2_ref[...], preferred_element_type=F32))
    o_ref[...] = x_ref[...] + mod_ref[5:6, :] * y


def _merge_cd(x, mods, p, h_f, h_b, att, w_out, rows, *, tm):
    D = x.shape[1]
    W = LRU_WIDTH
    aw = MLA_HEADS * MLA_V
    off = rows.n_ctx // tm
    lat = lambda i: (i + off, 0)
    return pl.pallas_call(
        _merge_cd_kernel,
        grid=(rows.n_lat // tm,),
        in_specs=[pl.BlockSpec((tm, D), lat),
                  pl.BlockSpec((None, N_MOD, D), lambda i: (rows.mod_row(i, tm, False), 0, 0)),
                  pl.BlockSpec((tm, W), lambda i: (i + off, CD_YB // W)),
                  pl.BlockSpec((tm, W), lat), pl.BlockSpec((tm, W), lat),
                  pl.BlockSpec((tm, aw), lambda i: (i, 0)),
                  pl.BlockSpec((W, D), lambda i: (0, 0)),
                  pl.BlockSpec((aw, D), lambda i: (W // aw, 0))],
        out_specs=pl.BlockSpec((tm, D), lambda i: (i, 0)),
        out_shape=jax.ShapeDtypeStruct((rows.n_lat, D), F32),
        compiler_params=_params(("arbitrary",)),
        name="merge_cd",
    )(x, mods, p, h_f, h_b, att, w_out, w_out)


def _ret_tables(rows):
    half = RET_DK // 2
    freqs = ROPE_BASE ** (-jnp.arange(half, dtype=F32) / half)
    ang = jnp.arange(rows.S, dtype=F32)[:, None] * freqs
    cos = jnp.concatenate([jnp.cos(ang), jnp.cos(ang)], axis=-1)
    sin = jnp.concatenate([-jnp.sin(ang), jnp.sin(ang)], axis=-1)
    cos = jnp.concatenate([jnp.ones((rows.Lc, RET_DK), F32), cos], axis=0)
    sin = jnp.concatenate([jnp.zeros((rows.Lc, RET_DK), F32), sin], axis=0)
    return cos, sin


def _mla_tables(rows):
    S = rows.S
    quarter = MLA_ROPE // 4
    freqs = ROPE_BASE ** (-jnp.arange(quarter, dtype=F32) / quarter)
    t = jnp.arange(S)
    row = (t // GRID_W).astype(F32)
    col = (t % GRID_W).astype(F32)

    def part(pos):
        ang = pos[:, None] * freqs
        return (jnp.concatenate([jnp.cos(ang), jnp.cos(ang)], axis=-1),
                jnp.concatenate([-jnp.sin(ang), jnp.sin(ang)], axis=-1))

    rc, rs = part(row)
    cc, cs = part(col)
    tail = MLA_PAD - MLA_QK
    cos = jnp.concatenate([rc, cc, jnp.ones((S, tail), F32)], axis=-1)
    sin = jnp.concatenate([rs, cs, jnp.zeros((S, tail), F32)], axis=-1)
    cos = jnp.concatenate([jnp.ones((rows.Lc, MLA_ROPE + tail), F32), cos], axis=0)
    sin = jnp.concatenate([jnp.zeros((rows.Lc, MLA_ROPE + tail), F32), sin], axis=0)
    return cos, sin


def _pad_cols(w, n):
    return jnp.pad(w, ((0, 0), (0, n - w.shape[1])))


def _head_pad(w, heads, width, padded):
    K = w.shape[0]
    return jnp.pad(w.reshape(K, heads, width), ((0, 0), (0, 0), (0, padded - width))).reshape(K, heads * padded)


def kernel(x, c, ctx, c_ctx, ada_w, ada_b, norm_g, ffn_wg, ffn_wu, ffn_wd, ab_w_in, ab_w_out, ret_decay_logit, ret_gn_g, mlstm_gate_b, mlstm_gn_g, cd_w_in, cd_w_out, lru_conv_w, lru_conv_b, lru_wa, lru_ba, lru_wx, lru_bx, lru_lambda, mla_q_norm_g, mla_kv_norm_g, mla_w_uq, mla_w_uk, mla_w_uv, mla_qk_norm_g):
    B, S, D = x.shape
    Lc = ctx.shape[1]
    depth = ada_w.shape[0]
    F = ffn_wg.shape[-1]
    rows = Rows(B, Lc, S)
    assert B < SUBLANE and Lc % CHUNK == 0 and S % Lc == 0 and S % GRID_W == 0

    tm = min(1024, rows.n_ctx)
    tf = 512
    all_mod = lambda i: rows.mod_row(i, tm, True)
    lat_mod = lambda i: rows.mod_row(i, tm, False)

    cond = jnp.zeros((SUBLANE, D), F32).at[:B].set(c.astype(F32)).at[B].set(c_ctx.astype(F32))
    mods = _adaln(cond, ada_w, ada_b)

    wg = ffn_wg.astype(BF16)
    wu = ffn_wu.astype(BF16)
    wd = ffn_wd.astype(BF16)

    for l in range(depth):
        last = l == depth - 1
        j = l // 2
        m_l = mods[l]
        ffn1 = functools.partial(_ffn, mods=m_l, g=norm_g[l, 0], wg=wg, wu=wu, wd=wd, layer=l, half=0,
                                 mod_base=0, tm=tm, tf=tf)
        if l == 0:
            xa = ffn1(x.reshape(B * S, D).astype(F32), mod_of_tile=lat_mod, out_rows=rows.n_all,
                      out_tile0=rows.n_ctx // tm)
            xa = ffn1(ctx.reshape(B * Lc, D).astype(F32), mod_of_tile=lambda i: B, out_rows=rows.n_all, into=xa)
        else:
            xa = ffn1(xa, mod_of_tile=all_mod)

        if l % 2 == 0:
            w_in = _pad_cols(ab_w_in[j].astype(BF16), AB_PAD)
            p = _proj(xa, m_l, norm_g[l, 1], w_in, rows, tm=tm, tn=AB_PAD // 7)
            cos_t, sin_t = _ret_tables(rows)
            ret_f, ret_b = _retention(p, ret_decay_logit[j], cos_t, sin_t, rows)
            ml_f, ml_b = _mlstm(p, mlstm_gate_b[j], rows)
            xa = _merge_ab(xa, m_l, ret_f, ret_b, ml_f, ml_b, p, ret_gn_g[j], mlstm_gn_g[j],
                           ab_w_out[j].astype(BF16), rows, tm=min(256, tm))
            if last:
                xa = xa[rows.n_ctx:]
        else:
            assert last, "context outputs of the recurrent/attention mixer are not produced"
            w_in = _pad_cols(cd_w_in[j].astype(BF16), CD_PAD)
            p = _proj(xa, m_l, norm_g[l, 1], w_in, rows, tm=tm, tn=CD_PAD // 3)
            a_c, b_c = _lru_coef(p, lru_conv_w[j], lru_conv_b[j], lru_wa[j].astype(BF16), lru_wx[j].astype(BF16),
                                 lru_ba[j], lru_bx[j], lru_lambda[j], rows, tm=min(256, Lc))
            h_f, h_b = _lru_scan(a_c, b_c, rows)
            cos_t, sin_t = _mla_tables(rows)
            wq = _head_pad(mla_w_uq[j].astype(BF16), MLA_HEADS, MLA_QK, MLA_PAD)
            qk_g = jnp.pad(mla_qk_norm_g[j].astype(F32), ((0, 0), (0, MLA_PAD - MLA_QK)))
            q, k, v = _mla_qkv(p, mla_q_norm_g[j], mla_kv_norm_g[j], wq, mla_w_uk[j].astype(BF16),
                               mla_w_uv[j].astype(BF16), qk_g, cos_t, sin_t, rows, tm=min(256, Lc))
            att = _attention(q, k, v, rows, tq=min(2048, S))
            xa = _merge_cd(xa, m_l, p, h_f, h_b, att, cd_w_out[j].astype(BF16), rows, tm=min(256, tm))

        xa = _ffn(xa, m_l, norm_g[l, 2], wg, wu, wd, layer=l, half=1, mod_base=6,
                  mod_of_tile=lat_mod if last else all_mod, tm=tm, tf=tf)

    if xa.shape[0] != rows.n_lat:
        xa = xa[rows.n_ctx:]
    return xa.reshape(B, S, D).astype(x.dtype)
```

```python
import functools
import math

import jax
import jax.numpy as jnp
import numpy as np
from jax import lax
from jax.experimental import pallas as pl
from jax.experimental.pallas import tpu as pltpu

F32 = jnp.float32
BF16 = jnp.bfloat16

N_MOD = 9
EPS = 1e-6
CHUNK = 128
GRID_W = 64
ROPE_BASE = 10000.0
NEG_BIG = -1e30

RET_HEADS = 4
RET_DK = 128
RET_DV = 256
MLSTM_HEADS = 4
MLSTM_DK = 128
MLSTM_DV = 256
LRU_WIDTH = 1024
LRU_BLOCKS = 8
LRU_BLOCK = LRU_WIDTH // LRU_BLOCKS
LRU_C = 8.0
MLA_HEADS = 8
MLA_Q_RANK = 512
MLA_KV_RANK = 256
MLA_NOPE = 128
MLA_ROPE = 64
MLA_V = 128
MLA_QK = MLA_NOPE + MLA_ROPE
LOG2E = math.log2(math.e)
MLA_PAD = 256

AB_SPLITS = (RET_HEADS * RET_DK, RET_HEADS * RET_DK, RET_HEADS * RET_DV, RET_HEADS * RET_DV,
             MLSTM_HEADS * MLSTM_DK, MLSTM_HEADS * MLSTM_DK, MLSTM_HEADS * MLSTM_DV, MLSTM_HEADS * MLSTM_DV,
             4 * MLSTM_HEADS)
AB_IN = sum(AB_SPLITS)
CD_SPLITS = (LRU_WIDTH, LRU_WIDTH, MLA_Q_RANK, MLA_KV_RANK, MLA_ROPE)
CD_IN = sum(CD_SPLITS)

LANE = 128
SUBLANE = 8
VMEM_LIMIT = 56 * 1024 * 1024
FFN_VMEM_LIMIT = 60 * 1024 * 1024

AB_QR, AB_KR, AB_VR, AB_GR, AB_QM, AB_KM, AB_VM, AB_GM, AB_GATES = np.cumsum((0,) + AB_SPLITS[:-1]).tolist()
AB_PAD = 6272
CD_YB, CD_XB, CD_CQ, CD_CKV, CD_KR = np.cumsum((0,) + CD_SPLITS[:-1]).tolist()
CD_PAD = 3072


def _params(sem, vmem=VMEM_LIMIT):
    return pltpu.CompilerParams(dimension_semantics=sem, vmem_limit_bytes=vmem)


def _round_up(n, m):
    return (n + m - 1) // m * m


def _sigmoid(x):
    return 0.5 * jnp.tanh(0.5 * x) + 0.5


def _log_sigmoid(x):
    return jnp.minimum(x, 0.0) - jnp.log(1.0 + jnp.exp(-jnp.abs(x)))


def _rms(x, g):
    return x * lax.rsqrt(jnp.mean(x * x, axis=-1, keepdims=True) + EPS) * g


def _adaln_kernel(c_ref, w_ref, b_ref, o_ref):
    c = c_ref[...]
    s = (c * _sigmoid(c)).astype(BF16)
    o_ref[...] = jnp.dot(s, w_ref[...].astype(BF16), preferred_element_type=F32) + b_ref[...]


def _adaln(cond, ada_w, ada_b):
    L, D, N = ada_w.shape
    tn = 1024 if N % 1024 == 0 else N
    out = pl.pallas_call(
        _adaln_kernel,
        grid=(L, N // tn),
        in_specs=[pl.BlockSpec((SUBLANE, D), lambda l, n: (0, 0)),
                  pl.BlockSpec((None, D, tn), lambda l, n: (l, 0, n)),
                  pl.BlockSpec((None, 1, tn), lambda l, n: (l, 0, n))],
        out_specs=pl.BlockSpec((None, SUBLANE, tn), lambda l, n: (l, 0, n)),
        out_shape=jax.ShapeDtypeStruct((L, SUBLANE, N), F32),
        compiler_params=_params(("arbitrary", "arbitrary")),
        name="adaln",
    )(cond, ada_w, ada_b.reshape(L, 1, N))
    return out.reshape(L, SUBLANE, N_MOD, D)


class Rows:
    def __init__(self, B, Lc, S):
        self.B, self.Lc, self.S = B, Lc, S
        self.n_ctx = B * Lc
        self.n_lat = B * S
        self.n_all = self.n_ctx + self.n_lat

    def mod_row(self, tile, tm, with_ctx):
        if not with_ctx:
            return (tile * tm) // self.S
        nct = self.n_ctx // tm
        return jnp.where(tile < nct, self.B, (jnp.maximum(tile - nct, 0) * tm) // self.S)


def _ffn_kernel(x_ref, mod_ref, g_ref, wg_ref, wu_ref, wd_ref, o_ref, h_scr, *, mod_base, sub, last_cols):
    f = pl.program_id(1)
    nf = pl.num_programs(1)
    tm = x_ref.shape[0]
    tf = wg_ref.shape[1]

    def step(cols, first, final):
        if first:
            shift = mod_ref[mod_base:mod_base + 1, :]
            scale = 1.0 + mod_ref[mod_base + 1:mod_base + 2, :]
        if final:
            gate = 0.5 * mod_ref[mod_base + 2:mod_base + 3, :]
        for r in range(0, tm, sub):
            rs = slice(r, r + sub)
            if first:
                h = (_rms(x_ref[rs, :], g_ref[...]) * scale + shift).astype(BF16)
                h_scr[rs, :] = h
            else:
                h = h_scr[rs, :]
            a = jnp.dot(h, wg_ref[:, :cols], preferred_element_type=F32)
            u = jnp.dot(h, wu_ref[:, :cols], preferred_element_type=F32)
            act = (a * _sigmoid(a) * u).astype(BF16)
            part = jnp.dot(act, wd_ref[:cols, :], preferred_element_type=F32)
            if first:
                o_ref[rs, :] = part
            elif final:
                o_ref[rs, :] = x_ref[rs, :] + gate * (o_ref[rs, :] + part)
            else:
                o_ref[rs, :] += part

    pl.when(f == 0)(functools.partial(step, tf, True, False))
    pl.when(jnp.logical_and(f > 0, f < nf - 1))(functools.partial(step, tf, False, False))
    pl.when(f == nf - 1)(functools.partial(step, last_cols, False, True))


def _ffn_kernel_into(x_ref, mod_ref, g_ref, wg_ref, wu_ref, wd_ref, into_ref, o_ref, h_scr, **kw):
    del into_ref
    _ffn_kernel(x_ref, mod_ref, g_ref, wg_ref, wu_ref, wd_ref, o_ref, h_scr, **kw)


def _ffn(x, mods, g, wg, wu, wd, *, layer, half, mod_base, mod_of_tile, tm, tf, out_rows=None, out_tile0=0,
         into=None):
    n, D = x.shape
    F = wg.shape[-1]
    nf = pl.cdiv(F, tf)
    assert nf >= 2 and n % tm == 0
    out_rows = n if out_rows is None else out_rows
    kw = dict(mod_base=mod_base, sub=min(tm, 512), last_cols=F - (nf - 1) * tf)
    in_specs = [pl.BlockSpec((tm, D), lambda i, f: (i, 0)),
                pl.BlockSpec((None, N_MOD, D), lambda i, f: (mod_of_tile(i), 0, 0)),
                pl.BlockSpec((1, D), lambda i, f: (0, 0)),
                pl.BlockSpec((None, None, D, tf), lambda i, f: (layer, half, 0, f)),
                pl.BlockSpec((None, None, D, tf), lambda i, f: (layer, half, 0, f)),
                pl.BlockSpec((None, None, tf, D), lambda i, f: (layer, half, f, 0))]
    args = [x, mods, g.reshape(1, D), wg, wu, wd]
    if into is None:
        kern, aliases = functools.partial(_ffn_kernel, **kw), {}
    else:
        assert into.shape == (out_rows, D)
        kern, aliases = functools.partial(_ffn_kernel_into, **kw), {len(args): 0}
        in_specs.append(pl.BlockSpec(memory_space=pl.ANY))
        args.append(into)
    return pl.pallas_call(
        kern,
        grid=(n // tm, nf),
        in_specs=in_specs,
        out_specs=pl.BlockSpec((tm, D), lambda i, f: (out_tile0 + i, 0)),
        out_shape=jax.ShapeDtypeStruct((out_rows, D), F32),
        scratch_shapes=[pltpu.VMEM((tm, D), BF16)],
        input_output_aliases=aliases,
        compiler_params=_params(("arbitrary", "arbitrary"), FFN_VMEM_LIMIT),
        name="ffn",
    )(*args)


def _proj_kernel(x_ref, mod_ref, g_ref, w_ref, o_ref, h_scr, *, sub):
    tm = x_ref.shape[0]

    @pl.when(pl.program_id(1) == 0)
    def _():
        scale = 1.0 + mod_ref[4:5, :]
        for r in range(0, tm, sub):
            h = (_rms(x_ref[r:r + sub, :], g_ref[...]) * scale + mod_ref[3:4, :]).astype(BF16)
            h_scr[r:r + sub, :] = h
            o_ref[r:r + sub, :] = jnp.dot(h, w_ref[...], preferred_element_type=F32)

    @pl.when(pl.program_id(1) != 0)
    def _():
        for r in range(0, tm, sub):
            o_ref[r:r + sub, :] = jnp.dot(h_scr[r:r + sub, :], w_ref[...], preferred_element_type=F32)


def _proj(x, mods, g, w, rows, *, tm, tn):
    n, D = x.shape
    Np = w.shape[1]
    return pl.pallas_call(
        functools.partial(_proj_kernel, sub=min(tm, 512)),
        grid=(n // tm, Np // tn),
        in_specs=[pl.BlockSpec((tm, D), lambda i, j: (i, 0)),
                  pl.BlockSpec((None, N_MOD, D), lambda i, j: (rows.mod_row(i, tm, True), 0, 0)),
                  pl.BlockSpec((1, D), lambda i, j: (0, 0)),
                  pl.BlockSpec((D, tn), lambda i, j: (0, j))],
        out_specs=pl.BlockSpec((tm, tn), lambda i, j: (i, j)),
        out_shape=jax.ShapeDtypeStruct((n, Np), F32),
        scratch_shapes=[pltpu.VMEM((tm, D), BF16)],
        compiler_params=_params(("arbitrary", "arbitrary")),
        name="proj",
    )(x, mods, g.reshape(1, D), w)


def _chunk_maps(rows):
    B = rows.B
    nc = rows.Lc // CHUNK
    nl = rows.S // CHUNK

    def seq_f(j):
        return j

    def seq_b(j):
        return jnp.where(j < nc, nc - 1 - j, nc + nl - 1 - (j - nc))

    def unit(b, s):
        return jnp.where(s < nc, b * nc + s, B * nc + b * nl + (s - nc))

    return nc + nl, seq_f, seq_b, unit


def _rope128(x, cos, sin):
    return x * cos + pltpu.roll(x, 64, axis=1) * sin


def _tri_masks():
    r = lax.broadcasted_iota(jnp.int32, (CHUNK, CHUNK), 0)
    c = lax.broadcasted_iota(jnp.int32, (CHUNK, CHUNK), 1)
    return r - c


def _ret_kernel(dl_ref, qf_ref, kf_ref, vf_ref, cf_ref, sf_ref, qb_ref, kb_ref, vb_ref, cb_ref, sb_ref,
                of_ref, ob_ref, s_scr):
    j = pl.program_id(1)

    @pl.when(j == 0)
    def _():
        s_scr[...] = jnp.zeros_like(s_scr)

    lg_all = _log_sigmoid(dl_ref[...])
    rel = _tri_masks().astype(F32)
    pos_c = lax.broadcasted_iota(jnp.int32, (CHUNK, 1), 0).astype(F32)
    dirs = ((qf_ref, kf_ref, vf_ref, cf_ref, sf_ref, of_ref, False),
            (qb_ref, kb_ref, vb_ref, cb_ref, sb_ref, ob_ref, True))
    probs = []
    for d, (q_ref, k_ref, v_ref, c_ref, sn_ref, o_ref, rev) in enumerate(dirs):
        dist = -rel if rev else rel
        step = (CHUNK - 1.0 - pos_c) if rev else pos_c
        for h in range(RET_HEADS):
            probs.append(dict(r=d * RET_HEADS + h, h=h, q_ref=q_ref, k_ref=k_ref, v_ref=v_ref, c_ref=c_ref,
                              sn_ref=sn_ref, o_ref=o_ref, dist=dist, step=step))

    nt = (((1,), (1,)), ((), ()))
    for p in probs:
        h, r = p["h"], p["r"]
        cos = p["c_ref"][...]
        sin = p["sn_ref"][...]
        p["lg"] = lg_all[r:r + 1, 0:1]
        q = _rope128(p["q_ref"][:, h * RET_DK:(h + 1) * RET_DK], cos, sin).astype(BF16)
        p["k"] = _rope128(p["k_ref"][:, h * RET_DK:(h + 1) * RET_DK], cos, sin) * (RET_DK ** -0.5)
        p["v"] = p["v_ref"][:, h * RET_DV:(h + 1) * RET_DV].astype(BF16)
        p["s_prev"] = s_scr[r]
        p["sc"] = lax.dot_general(q, p["k"].astype(BF16), nt, preferred_element_type=F32)
        p["cross"] = jnp.dot(q, p["s_prev"].astype(BF16), preferred_element_type=F32)
    for p in probs:
        lg = p["lg"]
        zeta = jnp.exp((CHUNK - 1.0 - p["step"]) * lg)
        kz_t = jnp.transpose(p["k"] * zeta).astype(BF16)
        u = jnp.dot(kz_t, p["v"], preferred_element_type=F32)
        s_scr[p["r"]] = jnp.exp(CHUNK * lg) * p["s_prev"] + u
    for p in probs:
        h, lg, dist = p["h"], p["lg"], p["dist"]
        decay = jnp.where(dist >= 0, jnp.exp(jnp.maximum(dist, 0.0) * lg), 0.0)
        inner = jnp.dot((p["sc"] * decay).astype(BF16), p["v"], preferred_element_type=F32)
        xi = jnp.exp((p["step"] + 1.0) * lg)
        p["o_ref"][:, h * RET_DV:(h + 1) * RET_DV] = inner + p["cross"] * xi


def _retention(p, decay_logit, cos_t, sin_t, rows):
    n = p.shape[0]
    B = rows.B
    n_steps, seq_f, seq_b, unit = _chunk_maps(rows)
    qw = RET_HEADS * RET_DK
    vw = RET_HEADS * RET_DV
    dl = jnp.broadcast_to(decay_logit.astype(F32).reshape(2 * RET_HEADS, 1), (2 * RET_HEADS, LANE))

    def in_specs(seq):
        return [pl.BlockSpec((CHUNK, qw), lambda b, j: (unit(b, seq(j)), AB_QR // qw)),
                pl.BlockSpec((CHUNK, qw), lambda b, j: (unit(b, seq(j)), AB_KR // qw)),
                pl.BlockSpec((CHUNK, vw), lambda b, j: (unit(b, seq(j)), AB_VR // vw)),
                pl.BlockSpec((CHUNK, RET_DK), lambda b, j: (seq(j), 0)),
                pl.BlockSpec((CHUNK, RET_DK), lambda b, j: (seq(j), 0))]

    def out_spec(seq):
        return pl.BlockSpec((CHUNK, vw), lambda b, j: (unit(b, seq(j)), 0))

    return pl.pallas_call(
        _ret_kernel,
        grid=(B, n_steps),
        in_specs=[pl.BlockSpec((2 * RET_HEADS, LANE), lambda b, j: (0, 0))] + in_specs(seq_f) + in_specs(seq_b),
        out_specs=[out_spec(seq_f), out_spec(seq_b)],
        out_shape=[jax.ShapeDtypeStruct((n, vw), F32)] * 2,
        scratch_shapes=[pltpu.VMEM((2 * RET_HEADS, RET_DK, RET_DV), F32)],
        compiler_params=_params(("arbitrary", "arbitrary")),
        name="retention",
    )(dl, p, p, p, cos_t, sin_t, p, p, p, cos_t, sin_t)


def _scan_lanes(x, reverse):
    n = x.shape[-1]
    lane = lax.broadcasted_iota(jnp.int32, x.shape, 1)
    d = 1
    while d < n:
        if reverse:
            x = x + jnp.where(lane < n - d, pltpu.roll(x, n - d, axis=1), 0.0)
        else:
            x = x + jnp.where(lane >= d, pltpu.roll(x, d, axis=1), 0.0)
        d *= 2
    return x


def _mlstm_kernel(gb_ref, qf_ref, kf_ref, vf_ref, gf_ref, qb_ref, kb_ref, vb_ref, gbk_ref,
                  of_ref, ob_ref, c_scr, m_scr):
    j = pl.program_id(1)
    H = MLSTM_HEADS
    dv = MLSTM_DV
    ext = dv + LANE

    @pl.when(j == 0)
    def _():
        c_scr[...] = jnp.zeros_like(c_scr)
        m_scr[...] = jnp.full_like(m_scr, NEG_BIG)

    rel = _tri_masks()
    lower = (rel >= 0).astype(F32)
    upper = (rel <= 0).astype(F32)
    exact = dict(precision=lax.Precision.HIGHEST, preferred_element_type=F32)
    ones = jnp.ones((CHUNK, LANE), BF16)
    dirs = ((qf_ref, kf_ref, vf_ref, gf_ref, of_ref, False),
            (qb_ref, kb_ref, vb_ref, gbk_ref, ob_ref, True))
    probs = []
    for d, (q_ref, k_ref, v_ref, g_ref, o_ref, rev) in enumerate(dirs):
        gates = g_ref[...] + gb_ref[...]
        cs_col = jnp.dot(upper if rev else lower, _log_sigmoid(gates), **exact)
        g8 = jnp.transpose(gates)[2 * d * H:2 * d * H + SUBLANE, :]
        cs = jnp.dot(_log_sigmoid(g8), lower if rev else upper, **exact)
        mask = (rel <= 0) if rev else (rel >= 0)
        for h in range(H):
            probs.append(dict(
                r=d * H + h, h=h, rev=rev, mask=mask, q_ref=q_ref, k_ref=k_ref, v_ref=v_ref, o_ref=o_ref,
                i_row=g8[h:h + 1, :], b_row=cs[H + h:H + h + 1, :],
                b_col=cs_col[:, 2 * d * H + H + h:2 * d * H + H + h + 1]))

    nt = (((1,), (1,)), ((), ()))
    for p in probs:
        h = p["h"]
        p["q"] = (p["q_ref"][:, h * MLSTM_DK:(h + 1) * MLSTM_DK] * (MLSTM_DK ** -0.5)).astype(BF16)
        k = p["k_ref"][:, h * MLSTM_DK:(h + 1) * MLSTM_DK]
        p["k_t"] = jnp.transpose(k)
        p["v_ext"] = jnp.concatenate([p["v_ref"][:, h * dv:(h + 1) * dv].astype(BF16), ones], axis=1)
        p["m_prev"] = m_scr[p["r"]:p["r"] + 1, 0:1]
        p["c_prev"] = c_scr[p["r"]]
        p["qk"] = lax.dot_general(p["q"], k.astype(BF16), nt, preferred_element_type=F32)
        p["qc"] = jnp.dot(p["q"], p["c_prev"].astype(BF16), preferred_element_type=F32)
    for p in probs:
        r = p["r"]
        b_row = p["b_row"]
        b_last = b_row[:, 0:1] if p["rev"] else b_row[:, CHUNK - 1:CHUNK]
        log_w = b_last - b_row + p["i_row"]
        m_loc = jnp.max(log_w, axis=1, keepdims=True)
        kw_t = (p["k_t"] * jnp.exp(log_w - m_loc)).astype(BF16)
        u = jnp.dot(kw_t, p["v_ext"], preferred_element_type=F32)
        m_new = jnp.maximum(b_last + p["m_prev"], m_loc)
        c_scr[r] = jnp.exp(b_last + p["m_prev"] - m_new) * p["c_prev"] + jnp.exp(m_loc - m_new) * u
        m_scr[r:r + 1, :] = jnp.broadcast_to(m_new, (1, LANE))
    for p in probs:
        log_d = jnp.where(p["mask"], p["b_col"] + (p["i_row"] - p["b_row"]), -jnp.inf)
        log_inter = p["b_col"] + p["m_prev"]
        p["m_t"] = jnp.maximum(log_inter, jnp.max(log_d, axis=1, keepdims=True))
        p["s"] = (p["qk"] * jnp.exp(log_d - p["m_t"])).astype(BF16)
        p["inter"] = jnp.exp(log_inter - p["m_t"])
    for p in probs:
        h = p["h"]
        tot = jnp.dot(p["s"], p["v_ext"], preferred_element_type=F32) + p["qc"] * p["inter"]
        den = tot[:, dv:dv + 1]
        p["o_ref"][:, h * dv:(h + 1) * dv] = tot[:, :dv] / jnp.maximum(jnp.abs(den), jnp.exp(-p["m_t"]))


def _mlstm(p, gate_b, rows):
    n = p.shape[0]
    B = rows.B
    H = MLSTM_HEADS
    n_steps, seq_f, seq_b, unit = _chunk_maps(rows)
    qw = H * MLSTM_DK
    vw = H * MLSTM_DV
    gb = jnp.zeros((1, LANE), F32).at[0, :4 * H].set(gate_b.astype(F32).reshape(4 * H))

    def in_specs(seq):
        return [pl.BlockSpec((CHUNK, qw), lambda b, j: (unit(b, seq(j)), AB_QM // qw)),
                pl.BlockSpec((CHUNK, qw), lambda b, j: (unit(b, seq(j)), AB_KM // qw)),
                pl.BlockSpec((CHUNK, vw), lambda b, j: (unit(b, seq(j)), AB_VM // vw)),
                pl.BlockSpec((CHUNK, LANE), lambda b, j: (unit(b, seq(j)), AB_GATES // LANE))]

    def out_spec(seq):
        return pl.BlockSpec((CHUNK, vw), lambda b, j: (unit(b, seq(j)), 0))

    return pl.pallas_call(
        _mlstm_kernel,
        grid=(B, n_steps),
        in_specs=[pl.BlockSpec((1, LANE), lambda b, j: (0, 0))] + in_specs(seq_f) + in_specs(seq_b),
        out_specs=[out_spec(seq_f), out_spec(seq_b)],
        out_shape=[jax.ShapeDtypeStruct((n, vw), F32)] * 2,
        scratch_shapes=[pltpu.VMEM((2 * H, MLSTM_DK, MLSTM_DV + LANE), F32),
                        pltpu.VMEM((2 * SUBLANE, LANE), F32)],
        compiler_params=_params(("arbitrary", "arbitrary")),
        name="mlstm",
    )(gb, p, p, p, p, p, p, p, p)


def _head_ln(y, g, heads, width):
    outs = []
    for h in range(heads):
        yh = y[:, h * width:(h + 1) * width]
        mu = jnp.mean(yh, axis=-1, keepdims=True)
        yc = yh - mu
        var = jnp.mean(yc * yc, axis=-1, keepdims=True)
        outs.append(yc * lax.rsqrt(var + EPS))
    return jnp.concatenate(outs, axis=1) * g


def _merge_ab_kernel(x_ref, mod_ref, rf_ref, rb_ref, mf_ref, mb_ref, gr_ref, gm_ref, rg_ref, mg_ref,
                     w1_ref, w2_ref, o_ref):
    gr = gr_ref[...]
    ret_y = (gr * _sigmoid(gr)) * _head_ln(rf_ref[...] + rb_ref[...], rg_ref[...], RET_HEADS, RET_DV)
    ml_y = _sigmoid(gm_ref[...]) * _head_ln(mf_ref[...] + mb_ref[...], mg_ref[...], MLSTM_HEADS, MLSTM_DV)
    y = (jnp.dot(ret_y.astype(BF16), w1_ref[...], preferred_element_type=F32)
         + jnp.dot(ml_y.astype(BF16), w2_ref[...], preferred_element_type=F32))
    o_ref[...] = x_ref[...] + mod_ref[5:6, :] * y


def _merge_ab(x, mods, ret_f, ret_b, ml_f, ml_b, p, ret_g, ml_g, w_out, rows, *, tm):
    n, D = x.shape
    rw = RET_HEADS * RET_DV
    mw = MLSTM_HEADS * MLSTM_DV
    row = lambda i: (i, 0)
    return pl.pallas_call(
        _merge_ab_kernel,
        grid=(n // tm,),
        in_specs=[pl.BlockSpec((tm, D), row),
                  pl.BlockSpec((None, N_MOD, D), lambda i: (rows.mod_row(i, tm, True), 0, 0)),
                  pl.BlockSpec((tm, rw), row), pl.BlockSpec((tm, rw), row),
                  pl.BlockSpec((tm, mw), row), pl.BlockSpec((tm, mw), row),
                  pl.BlockSpec((tm, rw), lambda i: (i, AB_GR // rw)),
                  pl.BlockSpec((tm, mw), lambda i: (i, AB_GM // mw)),
                  pl.BlockSpec((1, rw), lambda i: (0, 0)),
                  pl.BlockSpec((1, mw), lambda i: (0, 0)),
                  pl.BlockSpec((rw, D), lambda i: (0, 0)),
                  pl.BlockSpec((mw, D), lambda i: (rw // mw, 0))],
        out_specs=pl.BlockSpec((tm, D), row),
        out_shape=jax.ShapeDtypeStruct((n, D), F32),
        compiler_params=_params(("arbitrary",)),
        name="merge_ab",
    )(x, mods, ret_f, ret_b, ml_f, ml_b, p, p, ret_g.reshape(1, rw), ml_g.reshape(1, mw), w_out, w_out)


def _lru_coef_kernel(x_ref, xp_ref, xn_ref, cw_ref, cb_ref, wa_ref, wx_ref, ba_ref, bx_ref, lam_ref,
                     a_ref, b_ref, *, tm, ctx_tiles, ctx_seg, lat_seg):
    i = pl.program_id(0)
    seg_pos = jnp.where(i < ctx_tiles, i % ctx_seg, (i - ctx_tiles) % lat_seg)
    seg_len = jnp.where(i < ctx_tiles, ctx_seg, lat_seg)
    keep_prev = (seg_pos != 0).astype(F32)
    keep_next = (seg_pos != seg_len - 1).astype(F32)
    xe = jnp.concatenate([xp_ref[...] * keep_prev, x_ref[...], xn_ref[...] * keep_next], axis=0)
    ne = tm + 2 * SUBLANE
    xc = cb_ref[...] + cw_ref[2:3, :] * x_ref[...]
    for tap, off in ((0, -2), (1, -1), (3, 1)):
        shifted = pltpu.roll(xe, (-off) % ne, axis=0)[SUBLANE:SUBLANE + tm, :]
        xc = xc + cw_ref[tap:tap + 1, :] * shifted
    for d in range(2):
        lam = lam_ref[d:d + 1, :]
        sp = jnp.maximum(-lam, 0.0) + jnp.log(1.0 + jnp.exp(-jnp.abs(lam)))
        rate = (-LRU_C * LOG2E) * sp
        for g in range(LRU_BLOCKS):
            sl = slice(g * LRU_BLOCK, (g + 1) * LRU_BLOCK)
            xg = xc[:, sl]
            xg16 = xg.astype(BF16)
            r = _sigmoid(jnp.dot(xg16, wa_ref[d, g], preferred_element_type=F32) + ba_ref[d:d + 1, sl])
            ig = _sigmoid(jnp.dot(xg16, wx_ref[d, g], preferred_element_type=F32) + bx_ref[d:d + 1, sl])
            a = jnp.exp2(r * rate[:, sl])
            a_ref[d, :, sl] = a
            b_ref[d, :, sl] = jnp.sqrt(1.0 - a * a) * (ig * xg)


def _lru_coef(p, conv_w, conv_b, wa, wx, ba, bx, lam, rows, *, tm):
    n = p.shape[0]
    W = LRU_WIDTH
    tpb = tm // SUBLANE
    n8 = n // SUBLANE
    kern = functools.partial(_lru_coef_kernel, tm=tm, ctx_tiles=rows.n_ctx // tm,
                             ctx_seg=rows.Lc // tm, lat_seg=rows.S // tm)
    full = lambda *s: pl.BlockSpec(s, lambda i: (0,) * len(s))
    return pl.pallas_call(
        kern,
        grid=(n // tm,),
        in_specs=[pl.BlockSpec((tm, W), lambda i: (i, CD_XB // W)),
                  pl.BlockSpec((SUBLANE, W), lambda i: (jnp.maximum(i * tpb - 1, 0), CD_XB // W)),
                  pl.BlockSpec((SUBLANE, W), lambda i: (jnp.minimum((i + 1) * tpb, n8 - 1), CD_XB // W)),
                  full(4, W), full(1, W), full(2, LRU_BLOCKS, LRU_BLOCK, LRU_BLOCK),
                  full(2, LRU_BLOCKS, LRU_BLOCK, LRU_BLOCK), full(2, W), full(2, W), full(2, W)],
        out_specs=[pl.BlockSpec((2, tm, W), lambda i: (0, i, 0))] * 2,
        out_shape=[jax.ShapeDtypeStruct((2, n, W), F32)] * 2,
        compiler_params=_params(("arbitrary",)),
        name="lru_coef",
    )(p, p, p, conv_w, conv_b.reshape(1, W), wa, wx, ba, bx, lam)


def _lru_scan_kernel(af_ref, bf_ref, ab_ref, bb_ref, of_ref, ob_ref, h_scr, *, tb, lw):
    @pl.when(pl.program_id(1) == 0)
    def _():
        h_scr[...] = jnp.zeros_like(h_scr)

    row = lax.broadcasted_iota(jnp.int32, (SUBLANE, lw), 0)
    ng = tb // SUBLANE

    def scan8(a, b, rev):
        d = 1
        while d < SUBLANE:
            if rev:
                keep = row < SUBLANE - d
                sh = SUBLANE - d
            else:
                keep = row >= d
                sh = d
            b = a * jnp.where(keep, pltpu.roll(b, sh, axis=0), 0.0) + b
            a = a * jnp.where(keep, pltpu.roll(a, sh, axis=0), 1.0)
            d *= 2
        return a, b

    for c in range(LRU_WIDTH // lw):
        cs = slice(c * lw, (c + 1) * lw)

        def body(g, carry):
            hf, hb = carry
            rf = pl.multiple_of(g * SUBLANE, SUBLANE)
            a, b = scan8(af_ref[pl.ds(rf, SUBLANE), cs], bf_ref[pl.ds(rf, SUBLANE), cs], False)
            out = a * hf + b
            of_ref[pl.ds(rf, SUBLANE), cs] = out
            hf = jnp.broadcast_to(out[SUBLANE - 1:SUBLANE, :], (SUBLANE, lw))
            rb = pl.multiple_of((ng - 1 - g) * SUBLANE, SUBLANE)
            a, b = scan8(ab_ref[pl.ds(rb, SUBLANE), cs], bb_ref[pl.ds(rb, SUBLANE), cs], True)
            out = a * hb + b
            ob_ref[pl.ds(rb, SUBLANE), cs] = out
            hb = jnp.broadcast_to(out[0:1, :], (SUBLANE, lw))
            return hf, hb

        hf, hb = lax.fori_loop(0, ng, body, (h_scr[0, :, cs], h_scr[1, :, cs]))
        h_scr[0, :, cs] = hf
        h_scr[1, :, cs] = hb


def _lru_scan(a, b, rows):
    n = a.shape[1]
    W = LRU_WIDTH
    B = rows.B
    tb = rows.Lc
    nlb = rows.S // tb

    def blk_f(b_, j):
        return jnp.where(j == 0, b_, B + b_ * nlb + (j - 1))

    def blk_b(b_, j):
        return jnp.where(j == 0, b_, B + b_ * nlb + (nlb - j))

    def spec(d, blk):
        return pl.BlockSpec((None, tb, W), lambda b_, j: (d, blk(b_, j), 0))

    kern = functools.partial(_lru_scan_kernel, tb=tb, lw=512)
    return pl.pallas_call(
        kern,
        grid=(B, 1 + nlb),
        in_specs=[spec(0, blk_f), spec(0, blk_f), spec(1, blk_b), spec(1, blk_b)],
        out_specs=[pl.BlockSpec((tb, W), lambda b_, j: (blk_f(b_, j), 0)),
                   pl.BlockSpec((tb, W), lambda b_, j: (blk_b(b_, j), 0))],
        out_shape=[jax.ShapeDtypeStruct((n, W), F32)] * 2,
        scratch_shapes=[pltpu.VMEM((2, SUBLANE, W), F32)],
        compiler_params=_params(("arbitrary", "arbitrary")),
        name="lru_scan",
    )(a, b, a, b)


def _mla_qkv_kernel(cq_ref, ckv_ref, kr_ref, qg_ref, kvg_ref, wq_ref, wk_ref, wv_ref, qkg_ref, cos_ref, sin_ref,
                    q_ref, k_ref, v_ref):
    cqn = _rms(cq_ref[...], qg_ref[...]).astype(BF16)
    ckvn = _rms(ckv_ref[...], kvg_ref[...]).astype(BF16)
    q_all = jnp.dot(cqn, wq_ref[...], preferred_element_type=F32)
    kn_all = jnp.dot(ckvn, wk_ref[...], preferred_element_type=F32)
    v_all = jnp.dot(ckvn, wv_ref[...], preferred_element_type=F32).astype(BF16)
    ones = jnp.ones((v_all.shape[0], LANE), BF16)
    for h in range(MLA_HEADS):
        v_ref[:, h * (MLA_V + LANE):(h + 1) * (MLA_V + LANE)] = jnp.concatenate(
            [v_all[:, h * MLA_V:(h + 1) * MLA_V], ones], axis=1)
    kr = kr_ref[...]
    cos = cos_ref[...]
    sin = sin_ref[...]
    lane = lax.broadcasted_iota(jnp.int32, cos.shape, 1)
    first_half = (lane % (MLA_ROPE // 2)) < (MLA_ROPE // 4)
    hi_w = MLA_PAD - MLA_NOPE

    def norm_rope(lo, hi, g, scale):
        ssq = jnp.sum(lo * lo + hi * hi, axis=-1, keepdims=True)
        r = lax.rsqrt(ssq * (1.0 / MLA_QK) + EPS) * scale
        hi = hi * r * g[:, MLA_NOPE:]
        rot = jnp.where(first_half, pltpu.roll(hi, hi_w - MLA_ROPE // 4, axis=1),
                        pltpu.roll(hi, MLA_ROPE // 4, axis=1))
        return (lo * r * g[:, :MLA_NOPE]).astype(BF16), (hi * cos + rot * sin).astype(BF16)

    for h in range(MLA_HEADS):
        lo_sl = slice(h * MLA_PAD, h * MLA_PAD + MLA_NOPE)
        hi_sl = slice(h * MLA_PAD + MLA_NOPE, (h + 1) * MLA_PAD)
        q_ref[:, lo_sl], q_ref[:, hi_sl] = norm_rope(q_all[:, lo_sl], q_all[:, hi_sl], qkg_ref[0:1, :],
                                                     MLA_QK ** -0.5 * LOG2E)
        k_ref[:, lo_sl], k_ref[:, hi_sl] = norm_rope(kn_all[:, h * MLA_NOPE:(h + 1) * MLA_NOPE], kr,
                                                     qkg_ref[1:2, :], 1.0)


def _mla_qkv(p, q_norm_g, kv_norm_g, wq, wk, wv, qk_g, cos_t, sin_t, rows, *, tm):
    B, Lc, S = rows.B, rows.Lc, rows.S
    n = p.shape[0]
    nct = rows.n_ctx // tm
    cpb = Lc // tm
    lpb = S // tm

    def seq_blk(i):
        il = jnp.maximum(i - nct, 0)
        return (jnp.where(i < nct, i // cpb, il // lpb), jnp.where(i < nct, lpb + i % cpb, il % lpb), 0)

    def pos_blk(i):
        return (jnp.where(i < nct, i % cpb, cpb + jnp.maximum(i - nct, 0) % lpb), 0)

    full = lambda *s: pl.BlockSpec(s, lambda i: (0,) * len(s))
    qkw = MLA_HEADS * MLA_PAD
    vw = MLA_HEADS * MLA_V
    return pl.pallas_call(
        _mla_qkv_kernel,
        grid=(n // tm,),
        in_specs=[pl.BlockSpec((tm, MLA_Q_RANK), lambda i: (i, CD_CQ // MLA_Q_RANK)),
                  pl.BlockSpec((tm, MLA_KV_RANK), lambda i: (i, CD_CKV // MLA_KV_RANK)),
                  pl.BlockSpec((tm, LANE), lambda i: (i, CD_KR // LANE)),
                  full(1, MLA_Q_RANK), full(1, MLA_KV_RANK), full(MLA_Q_RANK, qkw),
                  full(MLA_KV_RANK, MLA_HEADS * MLA_NOPE), full(MLA_KV_RANK, vw), full(2, MLA_PAD),
                  pl.BlockSpec((tm, MLA_PAD - MLA_NOPE), pos_blk),
                  pl.BlockSpec((tm, MLA_PAD - MLA_NOPE), pos_blk)],
        out_specs=[pl.BlockSpec((None, tm, qkw), seq_blk), pl.BlockSpec((None, tm, qkw), seq_blk),
                   pl.BlockSpec((None, tm, vw + MLA_HEADS * LANE), seq_blk)],
        out_shape=[jax.ShapeDtypeStruct((B, Lc + S, qkw), BF16), jax.ShapeDtypeStruct((B, Lc + S, qkw), BF16),
                   jax.ShapeDtypeStruct((B, Lc + S, vw + MLA_HEADS * LANE), BF16)],
        compiler_params=_params(("arbitrary",)),
        name="mla_qkv",
    )(p, p, p, q_norm_g.reshape(1, -1), kv_norm_g.reshape(1, -1), wq, wk, wv, qk_g, cos_t, sin_t)


def _attn_kernel(q_ref, k_ref, v_ref, o_ref, *, sub):
    nt = (((1,), (1,)), ((), ()))
    groups = list(range(0, q_ref.shape[0], sub))

    def scores(r):
        return lax.dot_general(q_ref[r:r + sub, :], k_ref[...], nt, preferred_element_type=F32)

    s_next = scores(groups[0])
    for g, r in enumerate(groups):
        s = s_next
        if g + 1 < len(groups):
            s_next = scores(groups[g + 1])
        e = jnp.exp2(s - jnp.max(s, axis=-1, keepdims=True)).astype(BF16)
        ov = jnp.dot(e, v_ref[...], preferred_element_type=F32)
        o_ref[r:r + sub, :] = (ov[:, :MLA_V] / ov[:, MLA_V:MLA_V + 1]).astype(BF16)


def _attention(q, k, v, rows, *, tq):
    B, Lc, S = rows.B, rows.Lc, rows.S
    nq = S // tq
    return pl.pallas_call(
        functools.partial(_attn_kernel, sub=min(tq, 256)),
        grid=(B, MLA_HEADS, nq),
        in_specs=[pl.BlockSpec((None, tq, MLA_PAD), lambda b, h, i: (b, i, h)),
                  pl.BlockSpec((None, Lc + S, MLA_PAD), lambda b, h, i: (b, 0, h)),
                  pl.BlockSpec((None, Lc + S, MLA_V + LANE), lambda b, h, i: (b, 0, h))],
        out_specs=pl.BlockSpec((tq, MLA_V), lambda b, h, i: (b * nq + i, h)),
        out_shape=jax.ShapeDtypeStruct((B * S, MLA_HEADS * MLA_V), BF16),
        compiler_params=_params(("arbitrary", "arbitrary", "arbitrary")),
        name="attention",
    )(q, k, v)


def _gelu_tanh(x):
    return 0.5 * x * (1.0 + jnp.tanh(math.sqrt(2.0 / math.pi) * (x + 0.044715 * (x * x * x))))


def _merge_cd_kernel(x_ref, mod_ref, yb_ref, hf_ref, hb_ref, att_ref, w1_ref, w2_ref, o_ref):
    y1 = _gelu_tanh(yb_ref[...]) * (hf_ref[...] + hb_ref[...])
    y = (jnp.dot(y1.astype(BF16), w1_ref[...], preferred_element_type=F32)
         + jnp.dot(att_ref[...], w2_ref[...], preferred_element_type=F32))
    o_ref[...] = x_ref[...] + mod_ref[5:6, :] * y


def _merge_cd(x, mods, p, h_f, h_b, att, w_out, rows, *, tm):
    D = x.shape[1]
    W = LRU_WIDTH
    aw = MLA_HEADS * MLA_V
    off = rows.n_ctx // tm
    lat = lambda i: (i + off, 0)
    return pl.pallas_call(
        _merge_cd_kernel,
        grid=(rows.n_lat // tm,),
        in_specs=[pl.BlockSpec((tm, D), lat),
                  pl.BlockSpec((None, N_MOD, D), lambda i: (rows.mod_row(i, tm, False), 0, 0)),
                  pl.BlockSpec((tm, W), lambda i: (i + off, CD_YB // W)),
                  pl.BlockSpec((tm, W), lat), pl.BlockSpec((tm, W), lat),
                  pl.BlockSpec((tm, aw), lambda i: (i, 0)),
                  pl.BlockSpec((W, D), lambda i: (0, 0)),
                  pl.BlockSpec((aw, D), lambda i: (W // aw, 0))],
        out_specs=pl.BlockSpec((tm, D), lambda i: (i, 0)),
        out_shape=jax.ShapeDtypeStruct((rows.n_lat, D), F32),
        compiler_params=_params(("arbitrary",)),
        name="merge_cd",
    )(x, mods, p, h_f, h_b, att, w_out, w_out)


def _ret_tables(rows):
    half = RET_DK // 2
    freqs = ROPE_BASE ** (-jnp.arange(half, dtype=F32) / half)
    ang = jnp.arange(rows.S, dtype=F32)[:, None] * freqs
    cos = jnp.concatenate([jnp.cos(ang), jnp.cos(ang)], axis=-1)
    sin = jnp.concatenate([-jnp.sin(ang), jnp.sin(ang)], axis=-1)
    cos = jnp.concatenate([jnp.ones((rows.Lc, RET_DK), F32), cos], axis=0)
    sin = jnp.concatenate([jnp.zeros((rows.Lc, RET_DK), F32), sin], axis=0)
    return cos, sin


def _mla_tables(rows):
    S = rows.S
    quarter = MLA_ROPE // 4
    freqs = ROPE_BASE ** (-jnp.arange(quarter, dtype=F32) / quarter)
    t = jnp.arange(S)
    row = (t // GRID_W).astype(F32)
    col = (t % GRID_W).astype(F32)

    def part(pos):
        ang = pos[:, None] * freqs
        return (jnp.concatenate([jnp.cos(ang), jnp.cos(ang)], axis=-1),
                jnp.concatenate([-jnp.sin(ang), jnp.sin(ang)], axis=-1))

    rc, rs = part(row)
    cc, cs = part(col)
    tail = MLA_PAD - MLA_QK
    cos = jnp.concatenate([rc, cc, jnp.ones((S, tail), F32)], axis=-1)
    sin = jnp.concatenate([rs, cs, jnp.zeros((S, tail), F32)], axis=-1)
    cos = jnp.concatenate([jnp.ones((rows.Lc, MLA_ROPE + tail), F32), cos], axis=0)
    sin = jnp.concatenate([jnp.zeros((rows.Lc, MLA_ROPE + tail), F32), sin], axis=0)
    return cos, sin


def _pad_cols(w, n):
    return jnp.pad(w, ((0, 0), (0, n - w.shape[1])))


def _head_pad(w, heads, width, padded):
    K = w.shape[0]
    return jnp.pad(w.reshape(K, heads, width), ((0, 0), (0, 0), (0, padded - width))).reshape(K, heads * padded)


def kernel(x, c, ctx, c_ctx, ada_w, ada_b, norm_g, ffn_wg, ffn_wu, ffn_wd, ab_w_in, ab_w_out, ret_decay_logit, ret_gn_g, mlstm_gate_b, mlstm_gn_g, cd_w_in, cd_w_out, lru_conv_w, lru_conv_b, lru_wa, lru_ba, lru_wx, lru_bx, lru_lambda, mla_q_norm_g, mla_kv_norm_g, mla_w_uq, mla_w_uk, mla_w_uv, mla_qk_norm_g):
    B, S, D = x.shape
    Lc = ctx.shape[1]
    depth = ada_w.shape[0]
    F = ffn_wg.shape[-1]
    rows = Rows(B, Lc, S)
    assert B < SUBLANE and Lc % CHUNK == 0 and S % Lc == 0 and S % GRID_W == 0

    tm = min(1024, rows.n_ctx)
    tf = 512
    all_mod = lambda i: rows.mod_row(i, tm, True)
    lat_mod = lambda i: rows.mod_row(i, tm, False)

    cond = jnp.zeros((SUBLANE, D), F32).at[:B].set(c.astype(F32)).at[B].set(c_ctx.astype(F32))
    mods = _adaln(cond, ada_w, ada_b)

    wg = ffn_wg.astype(BF16)
    wu = ffn_wu.astype(BF16)
    wd = ffn_wd.astype(BF16)

    for l in range(depth):
        last = l == depth - 1
        j = l // 2
        m_l = mods[l]
        ffn1 = functools.partial(_ffn, mods=m_l, g=norm_g[l, 0], wg=wg, wu=wu, wd=wd, layer=l, half=0,
                                 mod_base=0, tm=tm, tf=tf)
        if l == 0:
            xa = ffn1(x.reshape(B * S, D).astype(F32), mod_of_tile=lat_mod, out_rows=rows.n_all,
                      out_tile0=rows.n_ctx // tm)
            xa = ffn1(ctx.reshape(B * Lc, D).astype(F32), mod_of_tile=lambda i: B, out_rows=rows.n_all, into=xa)
        else:
            xa = ffn1(xa, mod_of_tile=all_mod)

        if l % 2 == 0:
            w_in = _pad_cols(ab_w_in[j].astype(BF16), AB_PAD)
            p = _proj(xa, m_l, norm_g[l, 1], w_in, rows, tm=tm, tn=AB_PAD // 7)
            cos_t, sin_t = _ret_tables(rows)
            ret_f, ret_b = _retention(p, ret_decay_logit[j], cos_t, sin_t, rows)
            ml_f, ml_b = _mlstm(p, mlstm_gate_b[j], rows)
            xa = _merge_ab(xa, m_l, ret_f, ret_b, ml_f, ml_b, p, ret_gn_g[j], mlstm_gn_g[j],
                           ab_w_out[j].astype(BF16), rows, tm=min(256, tm))
            if last:
                xa = xa[rows.n_ctx:]
        else:
            assert last, "context outputs of the recurrent/attention mixer are not produced"
            w_in = _pad_cols(cd_w_in[j].astype(BF16), CD_PAD)
            p = _proj(xa, m_l, norm_g[l, 1], w_in, rows, tm=tm, tn=CD_PAD // 3)
            a_c, b_c = _lru_coef(p, lru_conv_w[j], lru_conv_b[j], lru_wa[j].astype(BF16), lru_wx[j].astype(BF16),
                                 lru_ba[j], lru_bx[j], lru_lambda[j], rows, tm=min(256, Lc))
            h_f, h_b = _lru_scan(a_c, b_c, rows)
            cos_t, sin_t = _mla_tables(rows)
            wq = _head_pad(mla_w_uq[j].astype(BF16), MLA_HEADS, MLA_QK, MLA_PAD)
            qk_g = jnp.pad(mla_qk_norm_g[j].astype(F32), ((0, 0), (0, MLA_PAD - MLA_QK)))
            q, k, v = _mla_qkv(p, mla_q_norm_g[j], mla_kv_norm_g[j], wq, mla_w_uk[j].astype(BF16),
                               mla_w_uv[j].astype(BF16), qk_g, cos_t, sin_t, rows, tm=min(256, Lc))
            att = _attention(q, k, v, rows, tq=min(2048, S))
            xa = _merge_cd(xa, m_l, p, h_f, h_b, att, cd_w_out[j].astype(BF16), rows, tm=min(256, tm))

        xa = _ffn(xa, m_l, norm_g[l, 2], wg, wu, wd, layer=l, half=1, mod_base=6,
                  mod_of_tile=lat_mod if last else all_mod, tm=tm, tf=tf)

    if xa.shape[0] != rows.n_lat:
        xa = xa[rows.n_ctx:]
    return xa.reshape(B, S, D).astype(x.dtype)
```

```python
import functools
import math

import jax
import jax.numpy as jnp
import numpy as np
from jax import lax
from jax.experimental import pallas as pl
from jax.experimental.pallas import tpu as pltpu

F32 = jnp.float32
BF16 = jnp.bfloat16

N_MOD = 9
EPS = 1e-6
CHUNK = 128
GRID_W = 64
ROPE_BASE = 10000.0
NEG_BIG = -1e30

RET_HEADS = 4
RET_DK = 128
RET_DV = 256
MLSTM_HEADS = 4
MLSTM_DK = 128
MLSTM_DV = 256
LRU_WIDTH = 1024
LRU_BLOCKS = 8
LRU_BLOCK = LRU_WIDTH // LRU_BLOCKS
LRU_C = 8.0
MLA_HEADS = 8
MLA_Q_RANK = 512
MLA_KV_RANK = 256
MLA_NOPE = 128
MLA_ROPE = 64
MLA_V = 128
MLA_QK = MLA_NOPE + MLA_ROPE
LOG2E = math.log2(math.e)
MLA_PAD = 256

AB_SPLITS = (RET_HEADS * RET_DK, RET_HEADS * RET_DK, RET_HEADS * RET_DV, RET_HEADS * RET_DV,
             MLSTM_HEADS * MLSTM_DK, MLSTM_HEADS * MLSTM_DK, MLSTM_HEADS * MLSTM_DV, MLSTM_HEADS * MLSTM_DV,
             4 * MLSTM_HEADS)
AB_IN = sum(AB_SPLITS)
CD_SPLITS = (LRU_WIDTH, LRU_WIDTH, MLA_Q_RANK, MLA_KV_RANK, MLA_ROPE)
CD_IN = sum(CD_SPLITS)

LANE = 128
SUBLANE = 8
BF16_SUBLANES = 16
VMEM_LIMIT = 56 * 1024 * 1024
FFN_VMEM_LIMIT = 60 * 1024 * 1024

AB_QR, AB_KR, AB_VR, AB_GR, AB_QM, AB_KM, AB_VM, AB_GM, AB_GATES = np.cumsum((0,) + AB_SPLITS[:-1]).tolist()
CD_YB, CD_XB, CD_CQ, CD_CKV, CD_KR = np.cumsum((0,) + CD_SPLITS[:-1]).tolist()
CD_PAD = 3072


def _params(sem, vmem=VMEM_LIMIT):
    return pltpu.CompilerParams(dimension_semantics=sem, vmem_limit_bytes=vmem)


def _round_up(n, m):
    return (n + m - 1) // m * m


def _sigmoid(x):
    return 0.5 * jnp.tanh(0.5 * x) + 0.5


def _log_sigmoid(x):
    return jnp.minimum(x, 0.0) - jnp.log(1.0 + jnp.exp(-jnp.abs(x)))


def _rms(x, g):
    return x * lax.rsqrt(jnp.mean(x * x, axis=-1, keepdims=True) + EPS) * g


def _adaln_kernel(c_ref, w_ref, b_ref, o_ref):
    c = c_ref[...]
    s = (c * _sigmoid(c)).astype(BF16)
    o_ref[...] = jnp.dot(s, w_ref[...].astype(BF16), preferred_element_type=F32) + b_ref[...]


def _adaln(cond, ada_w, ada_b):
    L, D, N = ada_w.shape
    tn = 1024 if N % 1024 == 0 else N
    out = pl.pallas_call(
        _adaln_kernel,
        grid=(L, N // tn),
        in_specs=[pl.BlockSpec((SUBLANE, D), lambda l, n: (0, 0)),
                  pl.BlockSpec((None, D, tn), lambda l, n: (l, 0, n)),
                  pl.BlockSpec((None, 1, tn), lambda l, n: (l, 0, n))],
        out_specs=pl.BlockSpec((None, SUBLANE, tn), lambda l, n: (l, 0, n)),
        out_shape=jax.ShapeDtypeStruct((L, SUBLANE, N), F32),
        compiler_params=_params(("arbitrary", "arbitrary")),
        name="adaln",
    )(cond, ada_w, ada_b.reshape(L, 1, N))
    return out.reshape(L, SUBLANE, N_MOD, D)


class Rows:
    def __init__(self, B, Lc, S):
        self.B, self.Lc, self.S = B, Lc, S
        self.n_ctx = B * Lc
        self.n_lat = B * S
        self.n_all = self.n_ctx + self.n_lat

    def mod_row(self, tile, tm, with_ctx):
        if not with_ctx:
            return (tile * tm) // self.S
        nct = self.n_ctx // tm
        return jnp.where(tile < nct, self.B, (jnp.maximum(tile - nct, 0) * tm) // self.S)


def _ffn_kernel(*refs, mod_base, sub, last_cols, has_into, n_cast):
    x_ref, mod_ref, g_ref, wg_ref, wu_ref, wd_ref = refs[:6]
    cast_src = refs[6:6 + n_cast]
    o_ref = refs[6 + n_cast + has_into]
    cast_dst = refs[7 + n_cast + has_into:7 + 2 * n_cast + has_into]
    h_scr = refs[-1]
    f = pl.program_id(1)
    nf = pl.num_programs(1)
    tm = x_ref.shape[0]
    tf = wg_ref.shape[1]

    def step(cols, first, final):
        for src, dst in zip(cast_src, cast_dst):
            dst[...] = src[...].astype(BF16)
        if first:
            shift = mod_ref[mod_base:mod_base + 1, :]
            scale = 1.0 + mod_ref[mod_base + 1:mod_base + 2, :]
        if final:
            gate = 0.5 * mod_ref[mod_base + 2:mod_base + 3, :]
        for r in range(0, tm, sub):
            rs = slice(r, r + sub)
            if first:
                h = (_rms(x_ref[rs, :], g_ref[...]) * scale + shift).astype(BF16)
                h_scr[rs, :] = h
            else:
                h = h_scr[rs, :]
            a = jnp.dot(h, wg_ref[:, :cols], preferred_element_type=F32)
            u = jnp.dot(h, wu_ref[:, :cols], preferred_element_type=F32)
            act = (a * _sigmoid(a) * u).astype(BF16)
            part = jnp.dot(act, wd_ref[:cols, :], preferred_element_type=F32)
            if first:
                o_ref[rs, :] = part
            elif final:
                o_ref[rs, :] = x_ref[rs, :] + gate * (o_ref[rs, :] + part)
            else:
                o_ref[rs, :] += part

    pl.when(f == 0)(functools.partial(step, tf, True, False))
    pl.when(jnp.logical_and(f > 0, f < nf - 1))(functools.partial(step, tf, False, False))
    pl.when(f == nf - 1)(functools.partial(step, last_cols, False, True))


def _ffn(x, mods, g, wg, wu, wd, *, mod_base, mod_of_tile, tm, tf, out_rows=None, out_tile0=0, into=None,
         cast_next=None):
    n, D = x.shape
    F = wg.shape[-1]
    nf = pl.cdiv(F, tf)
    ni = n // tm
    assert nf >= 2 and n % tm == 0
    out_rows = n if out_rows is None else out_rows
    in_specs = [pl.BlockSpec((tm, D), lambda i, f: (i, 0)),
                pl.BlockSpec((None, N_MOD, D), lambda i, f: (mod_of_tile(i), 0, 0)),
                pl.BlockSpec((1, D), lambda i, f: (0, 0)),
                pl.BlockSpec((D, tf), lambda i, f: (0, f)),
                pl.BlockSpec((D, tf), lambda i, f: (0, f)),
                pl.BlockSpec((tf, D), lambda i, f: (f, 0))]
    args = [x, mods, g.reshape(1, D), wg, wu, wd]
    out_specs = [pl.BlockSpec((tm, D), lambda i, f: (out_tile0 + i, 0))]
    out_shape = [jax.ShapeDtypeStruct((out_rows, D), F32)]
    n_cast = 0
    if cast_next is not None:
        stacks, l_n, h_n = cast_next
        n_cast = len(stacks)
        for w in stacks:
            rows_w, cols_w = w.shape[2:]
            slab_rows = next(m for m in range(BF16_SUBLANES * pl.cdiv(rows_w, BF16_SUBLANES * ni * nf),
                                              rows_w + 1, BF16_SUBLANES) if rows_w % m == 0)
            nblk = rows_w // slab_rows
            slab = lambda i, f, nblk=nblk: jnp.minimum(i * nf + f, nblk - 1)
            in_specs.append(pl.BlockSpec((None, None, slab_rows, cols_w),
                                         lambda i, f, slab=slab: (l_n, h_n, slab(i, f), 0)))
            out_specs.append(pl.BlockSpec((slab_rows, cols_w), lambda i, f, slab=slab: (slab(i, f), 0)))
            out_shape.append(jax.ShapeDtypeStruct((rows_w, cols_w), BF16))
            args.append(w)
    aliases = {}
    if into is not None:
        assert into.shape == (out_rows, D)
        aliases = {len(args): 0}
        in_specs.append(pl.BlockSpec(memory_space=pl.ANY))
        args.append(into)
    kern = functools.partial(_ffn_kernel, mod_base=mod_base, sub=min(tm, 512), last_cols=F - (nf - 1) * tf,
                             has_into=int(into is not None), n_cast=n_cast)
    res = pl.pallas_call(
        kern,
        grid=(ni, nf),
        in_specs=in_specs,
        out_specs=out_specs,
        out_shape=out_shape,
        scratch_shapes=[pltpu.VMEM((tm, D), BF16)],
        input_output_aliases=aliases,
        compiler_params=_params(("arbitrary", "arbitrary"), FFN_VMEM_LIMIT),
        name="ffn",
    )(*args)
    return (res[0], tuple(res[1:])) if n_cast else res[0]


def _proj_kernel(x_ref, mod_ref, g_ref, w_ref, *rest, sub):
    wt_ref, o_ref, ot_ref, h_scr = rest if len(rest) == 4 else (None, rest[0], None, rest[1])
    tm = x_ref.shape[0]

    @pl.when(pl.program_id(1) == 0)
    def _():
        scale = 1.0 + mod_ref[4:5, :]
        for r in range(0, tm, sub):
            h = (_rms(x_ref[r:r + sub, :], g_ref[...]) * scale + mod_ref[3:4, :]).astype(BF16)
            h_scr[r:r + sub, :] = h
            o_ref[r:r + sub, :] = jnp.dot(h, w_ref[...], preferred_element_type=F32)
            if wt_ref is not None:
                ot_ref[r:r + sub, :] = jnp.dot(h, wt_ref[...], preferred_element_type=F32)

    @pl.when(pl.program_id(1) != 0)
    def _():
        for r in range(0, tm, sub):
            o_ref[r:r + sub, :] = jnp.dot(h_scr[r:r + sub, :], w_ref[...], preferred_element_type=F32)


def _proj(x, mods, g, w, rows, *, tm, tn, w_tail=None):
    n, D = x.shape
    N = w.shape[1]
    in_specs = [pl.BlockSpec((tm, D), lambda i, j: (i, 0)),
                pl.BlockSpec((None, N_MOD, D), lambda i, j: (rows.mod_row(i, tm, True), 0, 0)),
                pl.BlockSpec((1, D), lambda i, j: (0, 0)),
                pl.BlockSpec((D, tn), lambda i, j: (0, j))]
    out_specs = [pl.BlockSpec((tm, tn), lambda i, j: (i, j))]
    out_shape = [jax.ShapeDtypeStruct((n, N), F32)]
    args = [x, mods, g.reshape(1, D), w]
    if w_tail is not None:
        in_specs.append(pl.BlockSpec((D, LANE), lambda i, j: (0, 0)))
        out_specs.append(pl.BlockSpec((tm, LANE), lambda i, j: (i, 0)))
        out_shape.append(jax.ShapeDtypeStruct((n, LANE), F32))
        args.append(w_tail)
    return pl.pallas_call(
        functools.partial(_proj_kernel, sub=min(tm, 512)),
        grid=(n // tm, N // tn),
        in_specs=in_specs,
        out_specs=out_specs,
        out_shape=out_shape,
        scratch_shapes=[pltpu.VMEM((tm, D), BF16)],
        compiler_params=_params(("arbitrary", "arbitrary")),
        name="proj",
    )(*args)


def _chunk_maps(rows):
    B = rows.B
    nc = rows.Lc // CHUNK
    nl = rows.S // CHUNK

    def seq_f(j):
        return j

    def seq_b(j):
        return jnp.where(j < nc, nc - 1 - j, nc + nl - 1 - (j - nc))

    def unit(b, s):
        return jnp.where(s < nc, b * nc + s, B * nc + b * nl + (s - nc))

    return nc + nl, seq_f, seq_b, unit


def _rope128(x, cos, sin):
    return x * cos + pltpu.roll(x, 64, axis=1) * sin


def _tri_masks():
    r = lax.broadcasted_iota(jnp.int32, (CHUNK, CHUNK), 0)
    c = lax.broadcasted_iota(jnp.int32, (CHUNK, CHUNK), 1)
    return r - c


def _ret_kernel(dl_ref, qf_ref, kf_ref, vf_ref, cf_ref, sf_ref, qb_ref, kb_ref, vb_ref, cb_ref, sb_ref,
                of_ref, ob_ref, s_scr):
    j = pl.program_id(1)

    @pl.when(j == 0)
    def _():
        s_scr[...] = jnp.zeros_like(s_scr)

    lg_all = _log_sigmoid(dl_ref[...])
    rel = _tri_masks().astype(F32)
    pos_c = lax.broadcasted_iota(jnp.int32, (CHUNK, 1), 0).astype(F32)
    dirs = ((qf_ref, kf_ref, vf_ref, cf_ref, sf_ref, of_ref, False),
            (qb_ref, kb_ref, vb_ref, cb_ref, sb_ref, ob_ref, True))
    probs = []
    for d, (q_ref, k_ref, v_ref, c_ref, sn_ref, o_ref, rev) in enumerate(dirs):
        dist = -rel if rev else rel
        step = (CHUNK - 1.0 - pos_c) if rev else pos_c
        for h in range(RET_HEADS):
            probs.append(dict(r=d * RET_HEADS + h, h=h, q_ref=q_ref, k_ref=k_ref, v_ref=v_ref, c_ref=c_ref,
                              sn_ref=sn_ref, o_ref=o_ref, dist=dist, step=step))

    nt = (((1,), (1,)), ((), ()))
    for p in probs:
        h, r = p["h"], p["r"]
        cos = p["c_ref"][...]
        sin = p["sn_ref"][...]
        p["lg"] = lg_all[r:r + 1, 0:1]
        q = _rope128(p["q_ref"][:, h * RET_DK:(h + 1) * RET_DK], cos, sin).astype(BF16)
        p["k"] = _rope128(p["k_ref"][:, h * RET_DK:(h + 1) * RET_DK], cos, sin) * (RET_DK ** -0.5)
        p["v"] = p["v_ref"][:, h * RET_DV:(h + 1) * RET_DV].astype(BF16)
        p["s_prev"] = s_scr[r]
        p["sc"] = lax.dot_general(q, p["k"].astype(BF16), nt, preferred_element_type=F32)
        p["cross"] = jnp.dot(q, p["s_prev"].astype(BF16), preferred_element_type=F32)
    for p in probs:
        lg = p["lg"]
        zeta = jnp.exp((CHUNK - 1.0 - p["step"]) * lg)
        kz_t = jnp.transpose(p["k"] * zeta).astype(BF16)
        u = jnp.dot(kz_t, p["v"], preferred_element_type=F32)
        s_scr[p["r"]] = jnp.exp(CHUNK * lg) * p["s_prev"] + u
    for p in probs:
        h, lg, dist = p["h"], p["lg"], p["dist"]
        decay = jnp.where(dist >= 0, jnp.exp(jnp.maximum(dist, 0.0) * lg), 0.0)
        inner = jnp.dot((p["sc"] * decay).astype(BF16), p["v"], preferred_element_type=F32)
        xi = jnp.exp((p["step"] + 1.0) * lg)
        p["o_ref"][:, h * RET_DV:(h + 1) * RET_DV] = inner + p["cross"] * xi


def _retention(p, decay_logit, cos_t, sin_t, rows):
    n = p.shape[0]
    B = rows.B
    n_steps, seq_f, seq_b, unit = _chunk_maps(rows)
    qw = RET_HEADS * RET_DK
    vw = RET_HEADS * RET_DV
    dl = jnp.broadcast_to(decay_logit.astype(F32).reshape(2 * RET_HEADS, 1), (2 * RET_HEADS, LANE))

    def in_specs(seq):
        return [pl.BlockSpec((CHUNK, qw), lambda b, j: (unit(b, seq(j)), AB_QR // qw)),
                pl.BlockSpec((CHUNK, qw), lambda b, j: (unit(b, seq(j)), AB_KR // qw)),
                pl.BlockSpec((CHUNK, vw), lambda b, j: (unit(b, seq(j)), AB_VR // vw)),
                pl.BlockSpec((CHUNK, RET_DK), lambda b, j: (seq(j), 0)),
                pl.BlockSpec((CHUNK, RET_DK), lambda b, j: (seq(j), 0))]

    def out_spec(seq):
        return pl.BlockSpec((CHUNK, vw), lambda b, j: (unit(b, seq(j)), 0))

    return pl.pallas_call(
        _ret_kernel,
        grid=(B, n_steps),
        in_specs=[pl.BlockSpec((2 * RET_HEADS, LANE), lambda b, j: (0, 0))] + in_specs(seq_f) + in_specs(seq_b),
        out_specs=[out_spec(seq_f), out_spec(seq_b)],
        out_shape=[jax.ShapeDtypeStruct((n, vw), F32)] * 2,
        scratch_shapes=[pltpu.VMEM((2 * RET_HEADS, RET_DK, RET_DV), F32)],
        compiler_params=_params(("arbitrary", "arbitrary")),
        name="retention",
    )(dl, p, p, p, cos_t, sin_t, p, p, p, cos_t, sin_t)


def _scan_lanes(x, reverse):
    n = x.shape[-1]
    lane = lax.broadcasted_iota(jnp.int32, x.shape, 1)
    d = 1
    while d < n:
        if reverse:
            x = x + jnp.where(lane < n - d, pltpu.roll(x, n - d, axis=1), 0.0)
        else:
            x = x + jnp.where(lane >= d, pltpu.roll(x, d, axis=1), 0.0)
        d *= 2
    return x


def _mlstm_kernel(gb_ref, qf_ref, kf_ref, vf_ref, gf_ref, qb_ref, kb_ref, vb_ref, gbk_ref,
                  of_ref, ob_ref, c_scr, m_scr):
    j = pl.program_id(1)
    H = MLSTM_HEADS
    dv = MLSTM_DV
    ext = dv + LANE

    @pl.when(j == 0)
    def _():
        c_scr[...] = jnp.zeros_like(c_scr)
        m_scr[...] = jnp.full_like(m_scr, NEG_BIG)

    rel = _tri_masks()
    lower = (rel >= 0).astype(F32)
    upper = (rel <= 0).astype(F32)
    exact = dict(precision=lax.Precision.HIGHEST, preferred_element_type=F32)
    ones = jnp.ones((CHUNK, LANE), BF16)
    dirs = ((qf_ref, kf_ref, vf_ref, gf_ref, of_ref, False),
            (qb_ref, kb_ref, vb_ref, gbk_ref, ob_ref, True))
    probs = []
    for d, (q_ref, k_ref, v_ref, g_ref, o_ref, rev) in enumerate(dirs):
        gates = g_ref[...] + gb_ref[...]
        cs_col = jnp.dot(upper if rev else lower, _log_sigmoid(gates), **exact)
        g8 = jnp.transpose(gates)[2 * d * H:2 * d * H + SUBLANE, :]
        cs = jnp.dot(_log_sigmoid(g8), lower if rev else upper, **exact)
        mask = (rel <= 0) if rev else (rel >= 0)
        for h in range(H):
            probs.append(dict(
                r=d * H + h, h=h, rev=rev, mask=mask, q_ref=q_ref, k_ref=k_ref, v_ref=v_ref, o_ref=o_ref,
                i_row=g8[h:h + 1, :], b_row=cs[H + h:H + h + 1, :],
                b_col=cs_col[:, 2 * d * H + H + h:2 * d * H + H + h + 1]))

    nt = (((1,), (1,)), ((), ()))
    for p in probs:
        h = p["h"]
        p["q"] = (p["q_ref"][:, h * MLSTM_DK:(h + 1) * MLSTM_DK] * (MLSTM_DK ** -0.5)).astype(BF16)
        k = p["k_ref"][:, h * MLSTM_DK:(h + 1) * MLSTM_DK]
        p["k_t"] = jnp.transpose(k)
        p["v_ext"] = jnp.concatenate([p["v_ref"][:, h * dv:(h + 1) * dv].astype(BF16), ones], axis=1)
        p["m_prev"] = m_scr[p["r"]:p["r"] + 1, 0:1]
        p["c_prev"] = c_scr[p["r"]]
        p["qk"] = lax.dot_general(p["q"], k.astype(BF16), nt, preferred_element_type=F32)
        p["qc"] = jnp.dot(p["q"], p["c_prev"].astype(BF16), preferred_element_type=F32)
    for p in probs:
        r = p["r"]
        b_row = p["b_row"]
        b_last = b_row[:, 0:1] if p["rev"] else b_row[:, CHUNK - 1:CHUNK]
        log_w = b_last - b_row + p["i_row"]
        m_loc = jnp.max(log_w, axis=1, keepdims=True)
        kw_t = (p["k_t"] * jnp.exp(log_w - m_loc)).astype(BF16)
        u = jnp.dot(kw_t, p["v_ext"], preferred_element_type=F32)
        m_new = jnp.maximum(b_last + p["m_prev"], m_loc)
        c_scr[r] = jnp.exp(b_last + p["m_prev"] - m_new) * p["c_prev"] + jnp.exp(m_loc - m_new) * u
        m_scr[r:r + 1, :] = jnp.broadcast_to(m_new, (1, LANE))
    for p in probs:
        log_d = jnp.where(p["mask"], p["b_col"] + (p["i_row"] - p["b_row"]), -jnp.inf)
        log_inter = p["b_col"] + p["m_prev"]
        p["m_t"] = jnp.maximum(log_inter, jnp.max(log_d, axis=1, keepdims=True))
        p["s"] = (p["qk"] * jnp.exp(log_d - p["m_t"])).astype(BF16)
        p["inter"] = jnp.exp(log_inter - p["m_t"])
    for p in probs:
        h = p["h"]
        tot = jnp.dot(p["s"], p["v_ext"], preferred_element_type=F32) + p["qc"] * p["inter"]
        den = tot[:, dv:dv + 1]
        p["o_ref"][:, h * dv:(h + 1) * dv] = tot[:, :dv] / jnp.maximum(jnp.abs(den), jnp.exp(-p["m_t"]))


def _mlstm(p, p_gates, gate_b, rows):
    n = p.shape[0]
    B = rows.B
    H = MLSTM_HEADS
    n_steps, seq_f, seq_b, unit = _chunk_maps(rows)
    qw = H * MLSTM_DK
    vw = H * MLSTM_DV
    gb = jnp.zeros((1, LANE), F32).at[0, :4 * H].set(gate_b.astype(F32).reshape(4 * H))

    def in_specs(seq):
        return [pl.BlockSpec((CHUNK, qw), lambda b, j: (unit(b, seq(j)), AB_QM // qw)),
                pl.BlockSpec((CHUNK, qw), lambda b, j: (unit(b, seq(j)), AB_KM // qw)),
                pl.BlockSpec((CHUNK, vw), lambda b, j: (unit(b, seq(j)), AB_VM // vw)),
                pl.BlockSpec((CHUNK, LANE), lambda b, j: (unit(b, seq(j)), 0))]

    def out_spec(seq):
        return pl.BlockSpec((CHUNK, vw), lambda b, j: (unit(b, seq(j)), 0))

    return pl.pallas_call(
        _mlstm_kernel,
        grid=(B, n_steps),
        in_specs=[pl.BlockSpec((1, LANE), lambda b, j: (0, 0))] + in_specs(seq_f) + in_specs(seq_b),
        out_specs=[out_spec(seq_f), out_spec(seq_b)],
        out_shape=[jax.ShapeDtypeStruct((n, vw), F32)] * 2,
        scratch_shapes=[pltpu.VMEM((2 * H, MLSTM_DK, MLSTM_DV + LANE), F32),
                        pltpu.VMEM((2 * SUBLANE, LANE), F32)],
        compiler_params=_params(("arbitrary", "arbitrary")),
        name="mlstm",
    )(gb, p, p, p, p_gates, p, p, p, p_gates)


def _head_ln(y, g, heads, width):
    outs = []
    for h in range(heads):
        yh = y[:, h * width:(h + 1) * width]
        mu = jnp.mean(yh, axis=-1, keepdims=True)
        yc = yh - mu
        var = jnp.mean(yc * yc, axis=-1, keepdims=True)
        outs.append(yc * lax.rsqrt(var + EPS))
    return jnp.concatenate(outs, axis=1) * g


def _merge_ab_kernel(x_ref, mod_ref, rf_ref, rb_ref, mf_ref, mb_ref, gr_ref, gm_ref, rg_ref, mg_ref,
                     w1_ref, w2_ref, o_ref):
    gr = gr_ref[...]
    ret_y = (gr * _sigmoid(gr)) * _head_ln(rf_ref[...] + rb_ref[...], rg_ref[...], RET_HEADS, RET_DV)
    ml_y = _sigmoid(gm_ref[...]) * _head_ln(mf_ref[...] + mb_ref[...], mg_ref[...], MLSTM_HEADS, MLSTM_DV)
    y = (jnp.dot(ret_y.astype(BF16), w1_ref[...], preferred_element_type=F32)
         + jnp.dot(ml_y.astype(BF16), w2_ref[...], preferred_element_type=F32))
    o_ref[...] = x_ref[...] + mod_ref[5:6, :] * y


def _merge_ab(x, mods, ret_f, ret_b, ml_f, ml_b, p, ret_g, ml_g, w_out, rows, *, tm):
    n, D = x.shape
    rw = RET_HEADS * RET_DV
    mw = MLSTM_HEADS * MLSTM_DV
    row = lambda i: (i, 0)
    return pl.pallas_call(
        _merge_ab_kernel,
        grid=(n // tm,),
        in_specs=[pl.BlockSpec((tm, D), row),
                  pl.BlockSpec((None, N_MOD, D), lambda i: (rows.mod_row(i, tm, True), 0, 0)),
                  pl.BlockSpec((tm, rw), row), pl.BlockSpec((tm, rw), row),
                  pl.BlockSpec((tm, mw), row), pl.BlockSpec((tm, mw), row),
                  pl.BlockSpec((tm, rw), lambda i: (i, AB_GR // rw)),
                  pl.BlockSpec((tm, mw), lambda i: (i, AB_GM // mw)),
                  pl.BlockSpec((1, rw), lambda i: (0, 0)),
                  pl.BlockSpec((1, mw), lambda i: (0, 0)),
                  pl.BlockSpec((rw, D), lambda i: (0, 0)),
                  pl.BlockSpec((mw, D), lambda i: (rw // mw, 0))],
        out_specs=pl.BlockSpec((tm, D), row),
        out_shape=jax.ShapeDtypeStruct((n, D), F32),
        compiler_params=_params(("arbitrary",)),
        name="merge_ab",
    )(x, mods, ret_f, ret_b, ml_f, ml_b, p, p, ret_g.reshape(1, rw), ml_g.reshape(1, mw), w_out, w_out)


def _lru_coef_kernel(x_ref, xp_ref, xn_ref, cw_ref, cb_ref, wa_ref, wx_ref, ba_ref, bx_ref, lam_ref,
                     a_ref, b_ref, *, tm, ctx_tiles, ctx_seg, lat_seg):
    i = pl.program_id(0)
    seg_pos = jnp.where(i < ctx_tiles, i % ctx_seg, (i - ctx_tiles) % lat_seg)
    seg_len = jnp.where(i < ctx_tiles, ctx_seg, lat_seg)
    keep_prev = (seg_pos != 0).astype(F32)
    keep_next = (seg_pos != seg_len - 1).astype(F32)
    xe = jnp.concatenate([xp_ref[...] * keep_prev, x_ref[...], xn_ref[...] * keep_next], axis=0)
    ne = tm + 2 * SUBLANE
    xc = cb_ref[...] + cw_ref[2:3, :] * x_ref[...]
    for tap, off in ((0, -2), (1, -1), (3, 1)):
        shifted = pltpu.roll(xe, (-off) % ne, axis=0)[SUBLANE:SUBLANE + tm, :]
        xc = xc + cw_ref[tap:tap + 1, :] * shifted
    for d in range(2):
        lam = lam_ref[d:d + 1, :]
        sp = jnp.maximum(-lam, 0.0) + jnp.log(1.0 + jnp.exp(-jnp.abs(lam)))
        rate = (-LRU_C * LOG2E) * sp
        for g in range(LRU_BLOCKS):
            sl = slice(g * LRU_BLOCK, (g + 1) * LRU_BLOCK)
            xg = xc[:, sl]
            xg16 = xg.astype(BF16)
            r = _sigmoid(jnp.dot(xg16, wa_ref[d, g], preferred_element_type=F32) + ba_ref[d:d + 1, sl])
            ig = _sigmoid(jnp.dot(xg16, wx_ref[d, g], preferred_element_type=F32) + bx_ref[d:d + 1, sl])
            a = jnp.exp2(r * rate[:, sl])
            a_ref[d, :, sl] = a
            b_ref[d, :, sl] = jnp.sqrt(1.0 - a * a) * (ig * xg)


def _lru_coef(p, conv_w, conv_b, wa, wx, ba, bx, lam, rows, *, tm):
    n = p.shape[0]
    W = LRU_WIDTH
    tpb = tm // SUBLANE
    n8 = n // SUBLANE
    kern = functools.partial(_lru_coef_kernel, tm=tm, ctx_tiles=rows.n_ctx // tm,
                             ctx_seg=rows.Lc // tm, lat_seg=rows.S // tm)
    full = lambda *s: pl.BlockSpec(s, lambda i: (0,) * len(s))
    return pl.pallas_call(
        kern,
        grid=(n // tm,),
        in_specs=[pl.BlockSpec((tm, W), lambda i: (i, CD_XB // W)),
                  pl.BlockSpec((SUBLANE, W), lambda i: (jnp.maximum(i * tpb - 1, 0), CD_XB // W)),
                  pl.BlockSpec((SUBLANE, W), lambda i: (jnp.minimum((i + 1) * tpb, n8 - 1), CD_XB // W)),
                  full(4, W), full(1, W), full(2, LRU_BLOCKS, LRU_BLOCK, LRU_BLOCK),
                  full(2, LRU_BLOCKS, LRU_BLOCK, LRU_BLOCK), full(2, W), full(2, W), full(2, W)],
        out_specs=[pl.BlockSpec((2, tm, W), lambda i: (0, i, 0))] * 2,
        out_shape=[jax.ShapeDtypeStruct((2, n, W), F32)] * 2,
        compiler_params=_params(("arbitrary",)),
        name="lru_coef",
    )(p, p, p, conv_w, conv_b.reshape(1, W), wa, wx, ba, bx, lam)


def _lru_scan_kernel(af_ref, bf_ref, ab_ref, bb_ref, of_ref, ob_ref, h_scr, *, tb, lw):
    @pl.when(pl.program_id(1) == 0)
    def _():
        h_scr[...] = jnp.zeros_like(h_scr)

    row = lax.broadcasted_iota(jnp.int32, (SUBLANE, lw), 0)
    ng = tb // SUBLANE

    def scan8(a, b, rev):
        d = 1
        while d < SUBLANE:
            if rev:
                keep = row < SUBLANE - d
                sh = SUBLANE - d
            else:
                keep = row >= d
                sh = d
            b = a * jnp.where(keep, pltpu.roll(b, sh, axis=0), 0.0) + b
            a = a * jnp.where(keep, pltpu.roll(a, sh, axis=0), 1.0)
            d *= 2
        return a, b

    for c in range(LRU_WIDTH // lw):
        cs = slice(c * lw, (c + 1) * lw)

        def body(g, carry):
            hf, hb = carry
            rf = pl.multiple_of(g * SUBLANE, SUBLANE)
            a, b = scan8(af_ref[pl.ds(rf, SUBLANE), cs], bf_ref[pl.ds(rf, SUBLANE), cs], False)
            out = a * hf + b
            of_ref[pl.ds(rf, SUBLANE), cs] = out
            hf = jnp.broadcast_to(out[SUBLANE - 1:SUBLANE, :], (SUBLANE, lw))
            rb = pl.multiple_of((ng - 1 - g) * SUBLANE, SUBLANE)
            a, b = scan8(ab_ref[pl.ds(rb, SUBLANE), cs], bb_ref[pl.ds(rb, SUBLANE), cs], True)
            out = a * hb + b
            ob_ref[pl.ds(rb, SUBLANE), cs] = out
            hb = jnp.broadcast_to(out[0:1, :], (SUBLANE, lw))
            return hf, hb

        hf, hb = lax.fori_loop(0, ng, body, (h_scr[0, :, cs], h_scr[1, :, cs]))
        h_scr[0, :, cs] = hf
        h_scr[1, :, cs] = hb


def _lru_scan(a, b, rows):
    n = a.shape[1]
    W = LRU_WIDTH
    B = rows.B
    tb = rows.Lc
    nlb = rows.S // tb

    def blk_f(b_, j):
        return jnp.where(j == 0, b_, B + b_ * nlb + (j - 1))

    def blk_b(b_, j):
        return jnp.where(j == 0, b_, B + b_ * nlb + (nlb - j))

    def spec(d, blk):
        return pl.BlockSpec((None, tb, W), lambda b_, j: (d, blk(b_, j), 0))

    kern = functools.partial(_lru_scan_kernel, tb=tb, lw=512)
    return pl.pallas_call(
        kern,
        grid=(B, 1 + nlb),
        in_specs=[spec(0, blk_f), spec(0, blk_f), spec(1, blk_b), spec(1, blk_b)],
        out_specs=[pl.BlockSpec((tb, W), lambda b_, j: (blk_f(b_, j), 0)),
                   pl.BlockSpec((tb, W), lambda b_, j: (blk_b(b_, j), 0))],
        out_shape=[jax.ShapeDtypeStruct((n, W), F32)] * 2,
        scratch_shapes=[pltpu.VMEM((2, SUBLANE, W), F32)],
        compiler_params=_params(("arbitrary", "arbitrary")),
        name="lru_scan",
    )(a, b, a, b)


def _mla_qkv_kernel(cq_ref, ckv_ref, kr_ref, qg_ref, kvg_ref, wq_ref, wk_ref, wv_ref, qkg_ref, cos_ref, sin_ref,
                    q_ref, k_ref, v_ref):
    cqn = _rms(cq_ref[...], qg_ref[...]).astype(BF16)
    ckvn = _rms(ckv_ref[...], kvg_ref[...]).astype(BF16)
    q_all = jnp.dot(cqn, wq_ref[...], preferred_element_type=F32)
    kn_all = jnp.dot(ckvn, wk_ref[...], preferred_element_type=F32)
    v_all = jnp.dot(ckvn, wv_ref[...], preferred_element_type=F32).astype(BF16)
    ones = jnp.ones((v_all.shape[0], LANE), BF16)
    for h in range(MLA_HEADS):
        v_ref[:, h * (MLA_V + LANE):(h + 1) * (MLA_V + LANE)] = jnp.concatenate(
            [v_all[:, h * MLA_V:(h + 1) * MLA_V], ones], axis=1)
    kr = kr_ref[...]
    cos = cos_ref[...]
    sin = sin_ref[...]
    lane = lax.broadcasted_iota(jnp.int32, cos.shape, 1)
    first_half = (lane % (MLA_ROPE // 2)) < (MLA_ROPE // 4)
    hi_w = MLA_PAD - MLA_NOPE

    def norm_rope(lo, hi, g, scale):
        ssq = jnp.sum(lo * lo + hi * hi, axis=-1, keepdims=True)
        r = lax.rsqrt(ssq * (1.0 / MLA_QK) + EPS) * scale
        hi = hi * r * g[:, MLA_NOPE:]
        rot = jnp.where(first_half, pltpu.roll(hi, hi_w - MLA_ROPE // 4, axis=1),
                        pltpu.roll(hi, MLA_ROPE // 4, axis=1))
        return (lo * r * g[:, :MLA_NOPE]).astype(BF16), (hi * cos + rot * sin).astype(BF16)

    for h in range(MLA_HEADS):
        lo_sl = slice(h * MLA_PAD, h * MLA_PAD + MLA_NOPE)
        hi_sl = slice(h * MLA_PAD + MLA_NOPE, (h + 1) * MLA_PAD)
        q_ref[:, lo_sl], q_ref[:, hi_sl] = norm_rope(q_all[:, lo_sl], q_all[:, hi_sl], qkg_ref[0:1, :],
                                                     MLA_QK ** -0.5 * LOG2E)
        k_ref[:, lo_sl], k_ref[:, hi_sl] = norm_rope(kn_all[:, h * MLA_NOPE:(h + 1) * MLA_NOPE], kr,
                                                     qkg_ref[1:2, :], 1.0)


def _mla_qkv(p, q_norm_g, kv_norm_g, wq, wk, wv, qk_g, cos_t, sin_t, rows, *, tm):
    B, Lc, S = rows.B, rows.Lc, rows.S
    n = p.shape[0]
    nct = rows.n_ctx // tm
    cpb = Lc // tm
    lpb = S // tm

    def seq_blk(i):
        il = jnp.maximum(i - nct, 0)
        return (jnp.where(i < nct, i // cpb, il // lpb), jnp.where(i < nct, lpb + i % cpb, il % lpb), 0)

    def pos_blk(i):
        return (jnp.where(i < nct, i % cpb, cpb + jnp.maximum(i - nct, 0) % lpb), 0)

    full = lambda *s: pl.BlockSpec(s, lambda i: (0,) * len(s))
    qkw = MLA_HEADS * MLA_PAD
    vw = MLA_HEADS * MLA_V
    return pl.pallas_call(
        _mla_qkv_kernel,
        grid=(n // tm,),
        in_specs=[pl.BlockSpec((tm, MLA_Q_RANK), lambda i: (i, CD_CQ // MLA_Q_RANK)),
                  pl.BlockSpec((tm, MLA_KV_RANK), lambda i: (i, CD_CKV // MLA_KV_RANK)),
                  pl.BlockSpec((tm, LANE), lambda i: (i, CD_KR // LANE)),
                  full(1, MLA_Q_RANK), full(1, MLA_KV_RANK), full(MLA_Q_RANK, qkw),
                  full(MLA_KV_RANK, MLA_HEADS * MLA_NOPE), full(MLA_KV_RANK, vw), full(2, MLA_PAD),
                  pl.BlockSpec((tm, MLA_PAD - MLA_NOPE), pos_blk),
                  pl.BlockSpec((tm, MLA_PAD - MLA_NOPE), pos_blk)],
        out_specs=[pl.BlockSpec((None, tm, qkw), seq_blk), pl.BlockSpec((None, tm, qkw), seq_blk),
                   pl.BlockSpec((None, tm, vw + MLA_HEADS * LANE), seq_blk)],
        out_shape=[jax.ShapeDtypeStruct((B, Lc + S, qkw), BF16), jax.ShapeDtypeStruct((B, Lc + S, qkw), BF16),
                   jax.ShapeDtypeStruct((B, Lc + S, vw + MLA_HEADS * LANE), BF16)],
        compiler_params=_params(("arbitrary",)),
        name="mla_qkv",
    )(p, p, p, q_norm_g.reshape(1, -1), kv_norm_g.reshape(1, -1), wq, wk, wv, qk_g, cos_t, sin_t)


def _attn_kernel(q_ref, k_ref, v_ref, o_ref, *, sub):
    nt = (((1,), (1,)), ((), ()))
    groups = list(range(0, q_ref.shape[0], sub))

    def scores(r):
        return lax.dot_general(q_ref[r:r + sub, :], k_ref[...], nt, preferred_element_type=F32)

    s_next = scores(groups[0])
    for g, r in enumerate(groups):
        s = s_next
        if g + 1 < len(groups):
            s_next = scores(groups[g + 1])
        e = jnp.exp2(s - jnp.max(s, axis=-1, keepdims=True)).astype(BF16)
        ov = jnp.dot(e, v_ref[...], preferred_element_type=F32)
        o_ref[r:r + sub, :] = (ov[:, :MLA_V] / ov[:, MLA_V:MLA_V + 1]).astype(BF16)


def _attention(q, k, v, rows, *, tq):
    B, Lc, S = rows.B, rows.Lc, rows.S
    nq = S // tq
    return pl.pallas_call(
        functools.partial(_attn_kernel, sub=min(tq, 256)),
        grid=(B, MLA_HEADS, nq),
        in_specs=[pl.BlockSpec((None, tq, MLA_PAD), lambda b, h, i: (b, i, h)),
                  pl.BlockSpec((None, Lc + S, MLA_PAD), lambda b, h, i: (b, 0, h)),
                  pl.BlockSpec((None, Lc + S, MLA_V + LANE), lambda b, h, i: (b, 0, h))],
        out_specs=pl.BlockSpec((tq, MLA_V), lambda b, h, i: (b * nq + i, h)),
        out_shape=jax.ShapeDtypeStruct((B * S, MLA_HEADS * MLA_V), BF16),
        compiler_params=_params(("arbitrary", "arbitrary", "arbitrary")),
        name="attention",
    )(q, k, v)


def _gelu_tanh(x):
    return 0.5 * x * (1.0 + jnp.tanh(math.sqrt(2.0 / math.pi) * (x + 0.044715 * (x * x * x))))


def _merge_cd_kernel(x_ref, mod_ref, yb_ref, hf_ref, hb_ref, att_ref, w1_ref, w2_ref, o_ref):
    y1 = _gelu_tanh(yb_ref[...]) * (hf_ref[...] + hb_ref[...])
    y = (jnp.dot(y1.astype(BF16), w1_ref[...], preferred_element_type=F32)
         + jnp.dot(att_ref[...], w2_ref[...], preferred_element_type=F32))
    o_ref[...] = x_ref[...] + mod_ref[5:6, :] * y


def _merge_cd(x, mods, p, h_f, h_b, att, w_out, rows, *, tm):
    D = x.shape[1]
    W = LRU_WIDTH
    aw = MLA_HEADS * MLA_V
    off = rows.n_ctx // tm
    lat = lambda i: (i + off, 0)
    return pl.pallas_call(
        _merge_cd_kernel,
        grid=(rows.n_lat // tm,),
        in_specs=[pl.BlockSpec((tm, D), lat),
                  pl.BlockSpec((None, N_MOD, D), lambda i: (rows.mod_row(i, tm, False), 0, 0)),
                  pl.BlockSpec((tm, W), lambda i: (i + off, CD_YB // W)),
                  pl.BlockSpec((tm, W), lat), pl.BlockSpec((tm, W), lat),
                  pl.BlockSpec((tm, aw), lambda i: (i, 0)),
                  pl.BlockSpec((W, D), lambda i: (0, 0)),
                  pl.BlockSpec((aw, D), lambda i: (W // aw, 0))],
        out_specs=pl.BlockSpec((tm, D), lambda i: (i, 0)),
        out_shape=jax.ShapeDtypeStruct((rows.n_lat, D), F32),
        compiler_params=_params(("arbitrary",)),
        name="merge_cd",
    )(x, mods, p, h_f, h_b, att, w_out, w_out)


def _ret_tables(rows):
    half = RET_DK // 2
    freqs = ROPE_BASE ** (-jnp.arange(half, dtype=F32) / half)
    ang = jnp.arange(rows.S, dtype=F32)[:, None] * freqs
    cos = jnp.concatenate([jnp.cos(ang), jnp.cos(ang)], axis=-1)
    sin = jnp.concatenate([-jnp.sin(ang), jnp.sin(ang)], axis=-1)
    cos = jnp.concatenate([jnp.ones((rows.Lc, RET_DK), F32), cos], axis=0)
    sin = jnp.concatenate([jnp.zeros((rows.Lc, RET_DK), F32), sin], axis=0)
    return cos, sin


def _mla_tables(rows):
    S = rows.S
    quarter = MLA_ROPE // 4
    freqs = ROPE_BASE ** (-jnp.arange(quarter, dtype=F32) / quarter)
    t = jnp.arange(S)
    row = (t // GRID_W).astype(F32)
    col = (t % GRID_W).astype(F32)

    def part(pos):
        ang = pos[:, None] * freqs
        return (jnp.concatenate([jnp.cos(ang), jnp.cos(ang)], axis=-1),
                jnp.concatenate([-jnp.sin(ang), jnp.sin(ang)], axis=-1))

    rc, rs = part(row)
    cc, cs = part(col)
    tail = MLA_PAD - MLA_QK
    cos = jnp.concatenate([rc, cc, jnp.ones((S, tail), F32)], axis=-1)
    sin = jnp.concatenate([rs, cs, jnp.zeros((S, tail), F32)], axis=-1)
    cos = jnp.concatenate([jnp.ones((rows.Lc, MLA_ROPE + tail), F32), cos], axis=0)
    sin = jnp.concatenate([jnp.zeros((rows.Lc, MLA_ROPE + tail), F32), sin], axis=0)
    return cos, sin


def _pad_cols(w, n):
    return jnp.pad(w, ((0, 0), (0, n - w.shape[1])))


def _head_pad(w, heads, width, padded):
    K = w.shape[0]
    return jnp.pad(w.reshape(K, heads, width), ((0, 0), (0, 0), (0, padded - width))).reshape(K, heads * padded)


def kernel(x, c, ctx, c_ctx, ada_w, ada_b, norm_g, ffn_wg, ffn_wu, ffn_wd, ab_w_in, ab_w_out, ret_decay_logit, ret_gn_g, mlstm_gate_b, mlstm_gn_g, cd_w_in, cd_w_out, lru_conv_w, lru_conv_b, lru_wa, lru_ba, lru_wx, lru_bx, lru_lambda, mla_q_norm_g, mla_kv_norm_g, mla_w_uq, mla_w_uk, mla_w_uv, mla_qk_norm_g):
    B, S, D = x.shape
    Lc = ctx.shape[1]
    depth = ada_w.shape[0]
    F = ffn_wg.shape[-1]
    rows = Rows(B, Lc, S)
    assert B < SUBLANE and Lc % CHUNK == 0 and S % Lc == 0 and S % GRID_W == 0

    tm = min(1024, rows.n_ctx)
    tf = 512
    all_mod = lambda i: rows.mod_row(i, tm, True)
    lat_mod = lambda i: rows.mod_row(i, tm, False)

    cond = jnp.zeros((SUBLANE, D), F32).at[:B].set(c.astype(F32)).at[B].set(c_ctx.astype(F32))
    mods = _adaln(cond, ada_w, ada_b)

    ffn_stacks = (ffn_wg, ffn_wu, ffn_wd)
    w_cur = tuple(w[0, 0].astype(BF16) for w in ffn_stacks)

    def cast_after(l, h):
        nxt = (l, 1) if h == 0 else (l + 1, 0)
        return (ffn_stacks,) + nxt if nxt[0] < depth else None

    for l in range(depth):
        last = l == depth - 1
        j = l // 2
        m_l = mods[l]
        ffn1 = functools.partial(_ffn, mods=m_l, g=norm_g[l, 0], wg=w_cur[0], wu=w_cur[1], wd=w_cur[2],
                                 mod_base=0, tm=tm, tf=tf)
        if l == 0:
            xa, w_next = ffn1(x.reshape(B * S, D).astype(F32), mod_of_tile=lat_mod, out_rows=rows.n_all,
                              out_tile0=rows.n_ctx // tm, cast_next=cast_after(l, 0))
            xa = ffn1(ctx.reshape(B * Lc, D).astype(F32), mod_of_tile=lambda i: B, out_rows=rows.n_all, into=xa)
        else:
            xa, w_next = ffn1(xa, mod_of_tile=all_mod, cast_next=cast_after(l, 0))
        w_cur = w_next

        if l % 2 == 0:
            w_in = ab_w_in[j].astype(BF16)
            p, p_gates = _proj(xa, m_l, norm_g[l, 1], w_in[:, :AB_GATES], rows, tm=tm, tn=1024,
                               w_tail=_pad_cols(w_in[:, AB_GATES:], LANE))
            cos_t, sin_t = _ret_tables(rows)
            ret_f, ret_b = _retention(p, ret_decay_logit[j], cos_t, sin_t, rows)
            ml_f, ml_b = _mlstm(p, p_gates, mlstm_gate_b[j], rows)
            xa = _merge_ab(xa, m_l, ret_f, ret_b, ml_f, ml_b, p, ret_gn_g[j], mlstm_gn_g[j],
                           ab_w_out[j].astype(BF16), rows, tm=min(256, tm))
            if last:
                xa = xa[rows.n_ctx:]
        else:
            assert last, "context outputs of the recurrent/attention mixer are not produced"
            w_in = _pad_cols(cd_w_in[j].astype(BF16), CD_PAD)
            p, = _proj(xa, m_l, norm_g[l, 1], w_in, rows, tm=tm, tn=CD_PAD // 3)
            a_c, b_c = _lru_coef(p, lru_conv_w[j], lru_conv_b[j], lru_wa[j].astype(BF16), lru_wx[j].astype(BF16),
                                 lru_ba[j], lru_bx[j], lru_lambda[j], rows, tm=min(256, Lc))
            h_f, h_b = _lru_scan(a_c, b_c, rows)
            cos_t, sin_t = _mla_tables(rows)
            wq = _head_pad(mla_w_uq[j].astype(BF16), MLA_HEADS, MLA_QK, MLA_PAD)
            qk_g = jnp.pad(mla_qk_norm_g[j].astype(F32), ((0, 0), (0, MLA_PAD - MLA_QK)))
            q, k, v = _mla_qkv(p, mla_q_norm_g[j], mla_kv_norm_g[j], wq, mla_w_uk[j].astype(BF16),
                               mla_w_uv[j].astype(BF16), qk_g, cos_t, sin_t, rows, tm=min(256, Lc))
            att = _attention(q, k, v, rows, tq=min(2048, S))
            xa = _merge_cd(xa, m_l, p, h_f, h_b, att, cd_w_out[j].astype(BF16), rows, tm=min(256, tm))

        res = _ffn(xa, m_l, norm_g[l, 2], *w_cur, mod_base=6, mod_of_tile=lat_mod if last else all_mod,
                   tm=tm, tf=tf, cast_next=cast_after(l, 1))
        xa, w_cur = res if cast_after(l, 1) else (res, None)

    if xa.shape[0] != rows.n_lat:
        xa = xa[rows.n_ctx:]
    return xa.reshape(B, S, D).astype(x.dtype)
```

```python
import functools
import math

import jax
import jax.numpy as jnp
import numpy as np
from jax import lax
from jax.experimental import pallas as pl
from jax.experimental.pallas import tpu as pltpu

F32 = jnp.float32
BF16 = jnp.bfloat16

N_MOD = 9
EPS = 1e-6
CHUNK = 128
GRID_W = 64
ROPE_BASE = 10000.0
NEG_BIG = -1e30

RET_HEADS = 4
RET_DK = 128
RET_DV = 256
MLSTM_HEADS = 4
MLSTM_DK = 128
MLSTM_DV = 256
LRU_WIDTH = 1024
LRU_BLOCKS = 8
LRU_BLOCK = LRU_WIDTH // LRU_BLOCKS
LRU_C = 8.0
MLA_HEADS = 8
MLA_Q_RANK = 512
MLA_KV_RANK = 256
MLA_NOPE = 128
MLA_ROPE = 64
MLA_V = 128
MLA_QK = MLA_NOPE + MLA_ROPE
LOG2E = math.log2(math.e)
MLA_PAD = 256

AB_SPLITS = (RET_HEADS * RET_DK, RET_HEADS * RET_DK, RET_HEADS * RET_DV, RET_HEADS * RET_DV,
             MLSTM_HEADS * MLSTM_DK, MLSTM_HEADS * MLSTM_DK, MLSTM_HEADS * MLSTM_DV, MLSTM_HEADS * MLSTM_DV,
             4 * MLSTM_HEADS)
AB_IN = sum(AB_SPLITS)
CD_SPLITS = (LRU_WIDTH, LRU_WIDTH, MLA_Q_RANK, MLA_KV_RANK, MLA_ROPE)
CD_IN = sum(CD_SPLITS)

LANE = 128
SUBLANE = 8
BF16_SUBLANES = 16
VMEM_LIMIT = 56 * 1024 * 1024
FFN_VMEM_LIMIT = 60 * 1024 * 1024

AB_QR, AB_KR, AB_VR, AB_GR, AB_QM, AB_KM, AB_VM, AB_GM, AB_GATES = np.cumsum((0,) + AB_SPLITS[:-1]).tolist()
CD_YB, CD_XB, CD_CQ, CD_CKV, CD_KR = np.cumsum((0,) + CD_SPLITS[:-1]).tolist()
CD_PAD = 3072


def _params(sem, vmem=VMEM_LIMIT):
    return pltpu.CompilerParams(dimension_semantics=sem, vmem_limit_bytes=vmem)


def _round_up(n, m):
    return (n + m - 1) // m * m


def _sigmoid(x):
    return 0.5 * jnp.tanh(0.5 * x) + 0.5


def _log_sigmoid(x):
    return jnp.minimum(x, 0.0) - jnp.log(1.0 + jnp.exp(-jnp.abs(x)))


def _rms(x, g):
    return x * lax.rsqrt(jnp.mean(x * x, axis=-1, keepdims=True) + EPS) * g


def _adaln_kernel(c_ref, w_ref, b_ref, o_ref):
    c = c_ref[...]
    s = (c * _sigmoid(c)).astype(BF16)
    o_ref[...] = jnp.dot(s, w_ref[...].astype(BF16), preferred_element_type=F32) + b_ref[...]


def _adaln(cond, ada_w, ada_b):
    L, D, N = ada_w.shape
    tn = 1024 if N % 1024 == 0 else N
    out = pl.pallas_call(
        _adaln_kernel,
        grid=(L, N // tn),
        in_specs=[pl.BlockSpec((SUBLANE, D), lambda l, n: (0, 0)),
                  pl.BlockSpec((None, D, tn), lambda l, n: (l, 0, n)),
                  pl.BlockSpec((None, 1, tn), lambda l, n: (l, 0, n))],
        out_specs=pl.BlockSpec((None, SUBLANE, tn), lambda l, n: (l, 0, n)),
        out_shape=jax.ShapeDtypeStruct((L, SUBLANE, N), F32),
        compiler_params=_params(("arbitrary", "arbitrary")),
        name="adaln",
    )(cond, ada_w, ada_b.reshape(L, 1, N))
    return out.reshape(L, SUBLANE, N_MOD, D)


class Rows:
    def __init__(self, B, Lc, S):
        self.B, self.Lc, self.S = B, Lc, S
        self.n_ctx = B * Lc
        self.n_lat = B * S
        self.n_all = self.n_ctx + self.n_lat

    def mod_row(self, tile, tm, with_ctx):
        if not with_ctx:
            return (tile * tm) // self.S
        nct = self.n_ctx // tm
        return jnp.where(tile < nct, self.B, (jnp.maximum(tile - nct, 0) * tm) // self.S)


def _ffn_kernel(*refs, mod_base, sub, last_cols, has_into, n_cast):
    x_ref, mod_ref, g_ref, wg_ref, wu_ref, wd_ref = refs[:6]
    cast_src = refs[6:6 + n_cast]
    o_ref = refs[6 + n_cast + has_into]
    cast_dst = refs[7 + n_cast + has_into:7 + 2 * n_cast + has_into]
    h_scr = refs[-1]
    f = pl.program_id(1)
    nf = pl.num_programs(1)
    tm = x_ref.shape[0]
    tf = wg_ref.shape[1]

    def step(cols, first, final):
        for src, dst in zip(cast_src, cast_dst):
            dst[...] = src[...].astype(BF16)
        if first:
            shift = mod_ref[mod_base:mod_base + 1, :]
            scale = 1.0 + mod_ref[mod_base + 1:mod_base + 2, :]
        if final:
            gate = 0.5 * mod_ref[mod_base + 2:mod_base + 3, :]
        for r in range(0, tm, sub):
            rs = slice(r, r + sub)
            if first:
                h = (_rms(x_ref[rs, :], g_ref[...]) * scale + shift).astype(BF16)
                h_scr[rs, :] = h
            else:
                h = h_scr[rs, :]
            a = jnp.dot(h, wg_ref[:, :cols], preferred_element_type=F32)
            u = jnp.dot(h, wu_ref[:, :cols], preferred_element_type=F32)
            act = (a * _sigmoid(a) * u).astype(BF16)
            part = jnp.dot(act, wd_ref[:cols, :], preferred_element_type=F32)
            if first:
                o_ref[rs, :] = part
            elif final:
                o_ref[rs, :] = x_ref[rs, :] + gate * (o_ref[rs, :] + part)
            else:
                o_ref[rs, :] += part

    pl.when(f == 0)(functools.partial(step, tf, True, False))
    pl.when(jnp.logical_and(f > 0, f < nf - 1))(functools.partial(step, tf, False, False))
    pl.when(f == nf - 1)(functools.partial(step, last_cols, False, True))


def _ffn(x, mods, g, wg, wu, wd, *, mod_base, mod_of_tile, tm, tf, out_rows=None, out_tile0=0, into=None,
         casts=()):
    n, D = x.shape
    F = wg.shape[-1]
    nf = pl.cdiv(F, tf)
    ni = n // tm
    assert nf >= 2 and n % tm == 0
    out_rows = n if out_rows is None else out_rows
    in_specs = [pl.BlockSpec((tm, D), lambda i, f: (i, 0)),
                pl.BlockSpec((None, N_MOD, D), lambda i, f: (mod_of_tile(i), 0, 0)),
                pl.BlockSpec((1, D), lambda i, f: (0, 0)),
                pl.BlockSpec((D, tf), lambda i, f: (0, f)),
                pl.BlockSpec((D, tf), lambda i, f: (0, f)),
                pl.BlockSpec((tf, D), lambda i, f: (f, 0))]
    args = [x, mods, g.reshape(1, D), wg, wu, wd]
    out_specs = [pl.BlockSpec((tm, D), lambda i, f: (out_tile0 + i, 0))]
    out_shape = [jax.ShapeDtypeStruct((out_rows, D), F32)]
    n_cast = len(casts)
    for w, lead in casts:
        rows_w, cols_w = w.shape[len(lead):]
        slab_rows = next(m for m in range(BF16_SUBLANES * pl.cdiv(rows_w, BF16_SUBLANES * ni * nf),
                                          rows_w + 1, BF16_SUBLANES) if rows_w % m == 0)
        nblk = rows_w // slab_rows
        slab = lambda i, f, nblk=nblk: jnp.minimum(i * nf + f, nblk - 1)
        in_specs.append(pl.BlockSpec((None,) * len(lead) + (slab_rows, cols_w),
                                     lambda i, f, slab=slab, lead=lead: lead + (slab(i, f), 0)))
        out_specs.append(pl.BlockSpec((slab_rows, cols_w), lambda i, f, slab=slab: (slab(i, f), 0)))
        out_shape.append(jax.ShapeDtypeStruct((rows_w, cols_w), BF16))
        args.append(w)
    aliases = {}
    if into is not None:
        assert into.shape == (out_rows, D)
        aliases = {len(args): 0}
        in_specs.append(pl.BlockSpec(memory_space=pl.ANY))
        args.append(into)
    kern = functools.partial(_ffn_kernel, mod_base=mod_base, sub=min(tm, 512), last_cols=F - (nf - 1) * tf,
                             has_into=int(into is not None), n_cast=n_cast)
    res = pl.pallas_call(
        kern,
        grid=(ni, nf),
        in_specs=in_specs,
        out_specs=out_specs,
        out_shape=out_shape,
        scratch_shapes=[pltpu.VMEM((tm, D), BF16)],
        input_output_aliases=aliases,
        compiler_params=_params(("arbitrary", "arbitrary"), FFN_VMEM_LIMIT),
        name="ffn",
    )(*args)
    return (res[0], tuple(res[1:])) if n_cast else res[0]


def _proj_kernel(x_ref, mod_ref, g_ref, w_ref, *rest, sub):
    wt_ref, o_ref, ot_ref, h_scr = rest if len(rest) == 4 else (None, rest[0], None, rest[1])
    tm = x_ref.shape[0]

    @pl.when(pl.program_id(1) == 0)
    def _():
        scale = 1.0 + mod_ref[4:5, :]
        for r in range(0, tm, sub):
            h = (_rms(x_ref[r:r + sub, :], g_ref[...]) * scale + mod_ref[3:4, :]).astype(BF16)
            h_scr[r:r + sub, :] = h
            o_ref[r:r + sub, :] = jnp.dot(h, w_ref[...], preferred_element_type=F32)
            if wt_ref is not None:
                ot_ref[r:r + sub, :] = jnp.dot(h, wt_ref[...], preferred_element_type=F32)

    @pl.when(pl.program_id(1) != 0)
    def _():
        for r in range(0, tm, sub):
            o_ref[r:r + sub, :] = jnp.dot(h_scr[r:r + sub, :], w_ref[...], preferred_element_type=F32)


def _proj(x, mods, g, w, rows, *, tm, tn, n_cols=None, w_tail=None):
    n, D = x.shape
    N = w.shape[1] if n_cols is None else n_cols
    assert N % tn == 0
    in_specs = [pl.BlockSpec((tm, D), lambda i, j: (i, 0)),
                pl.BlockSpec((None, N_MOD, D), lambda i, j: (rows.mod_row(i, tm, True), 0, 0)),
                pl.BlockSpec((1, D), lambda i, j: (0, 0)),
                pl.BlockSpec((D, tn), lambda i, j: (0, j))]
    out_specs = [pl.BlockSpec((tm, tn), lambda i, j: (i, j))]
    out_shape = [jax.ShapeDtypeStruct((n, N), F32)]
    args = [x, mods, g.reshape(1, D), w]
    if w_tail is not None:
        in_specs.append(pl.BlockSpec((D, LANE), lambda i, j: (0, 0)))
        out_specs.append(pl.BlockSpec((tm, LANE), lambda i, j: (i, 0)))
        out_shape.append(jax.ShapeDtypeStruct((n, LANE), F32))
        args.append(w_tail)
    return pl.pallas_call(
        functools.partial(_proj_kernel, sub=min(tm, 512)),
        grid=(n // tm, N // tn),
        in_specs=in_specs,
        out_specs=out_specs,
        out_shape=out_shape,
        scratch_shapes=[pltpu.VMEM((tm, D), BF16)],
        compiler_params=_params(("arbitrary", "arbitrary")),
        name="proj",
    )(*args)


def _chunk_maps(rows):
    B = rows.B
    nc = rows.Lc // CHUNK
    nl = rows.S // CHUNK

    def seq_f(j):
        return j

    def seq_b(j):
        return jnp.where(j < nc, nc - 1 - j, nc + nl - 1 - (j - nc))

    def unit(b, s):
        return jnp.where(s < nc, b * nc + s, B * nc + b * nl + (s - nc))

    return nc + nl, seq_f, seq_b, unit


def _rope128(x, cos, sin):
    return x * cos + pltpu.roll(x, 64, axis=1) * sin


def _tri_masks():
    r = lax.broadcasted_iota(jnp.int32, (CHUNK, CHUNK), 0)
    c = lax.broadcasted_iota(jnp.int32, (CHUNK, CHUNK), 1)
    return r - c


def _ret_body(dl_ref, qf_ref, kf_ref, vf_ref, cf_ref, sf_ref, qb_ref, kb_ref, vb_ref, cb_ref, sb_ref,
              of_ref, ob_ref, s_scr):
    lg_all = _log_sigmoid(dl_ref[...])
    rel = _tri_masks().astype(F32)
    pos_c = lax.broadcasted_iota(jnp.int32, (CHUNK, 1), 0).astype(F32)
    dirs = ((qf_ref, kf_ref, vf_ref, cf_ref, sf_ref, of_ref, False),
            (qb_ref, kb_ref, vb_ref, cb_ref, sb_ref, ob_ref, True))
    probs = []
    for d, (q_ref, k_ref, v_ref, c_ref, sn_ref, o_ref, rev) in enumerate(dirs):
        dist = -rel if rev else rel
        step = (CHUNK - 1.0 - pos_c) if rev else pos_c
        for h in range(RET_HEADS):
            probs.append(dict(r=d * RET_HEADS + h, h=h, q_ref=q_ref, k_ref=k_ref, v_ref=v_ref, c_ref=c_ref,
                              sn_ref=sn_ref, o_ref=o_ref, dist=dist, step=step))

    nt = (((1,), (1,)), ((), ()))
    for p in probs:
        h, r = p["h"], p["r"]
        cos = p["c_ref"][...]
        sin = p["sn_ref"][...]
        p["lg"] = lg_all[r:r + 1, 0:1]
        q = _rope128(p["q_ref"][:, h * RET_DK:(h + 1) * RET_DK], cos, sin).astype(BF16)
        p["k"] = _rope128(p["k_ref"][:, h * RET_DK:(h + 1) * RET_DK], cos, sin) * (RET_DK ** -0.5)
        p["v"] = p["v_ref"][:, h * RET_DV:(h + 1) * RET_DV].astype(BF16)
        p["s_prev"] = s_scr[r]
        p["sc"] = lax.dot_general(q, p["k"].astype(BF16), nt, preferred_element_type=F32)
        p["cross"] = jnp.dot(q, p["s_prev"].astype(BF16), preferred_element_type=F32)
    yield
    for p in probs:
        lg = p["lg"]
        zeta = jnp.exp((CHUNK - 1.0 - p["step"]) * lg)
        kz_t = jnp.transpose(p["k"] * zeta).astype(BF16)
        u = jnp.dot(kz_t, p["v"], preferred_element_type=F32)
        s_scr[p["r"]] = jnp.exp(CHUNK * lg) * p["s_prev"] + u
    yield
    for p in probs:
        h, lg, dist = p["h"], p["lg"], p["dist"]
        decay = jnp.where(dist >= 0, jnp.exp(jnp.maximum(dist, 0.0) * lg), 0.0)
        inner = jnp.dot((p["sc"] * decay).astype(BF16), p["v"], preferred_element_type=F32)
        xi = jnp.exp((p["step"] + 1.0) * lg)
        p["o_ref"][:, h * RET_DV:(h + 1) * RET_DV] = inner + p["cross"] * xi


def _mlstm_body(gb_ref, qf_ref, kf_ref, vf_ref, gf_ref, qb_ref, kb_ref, vb_ref, gbk_ref,
                of_ref, ob_ref, c_scr, m_scr):
    H = MLSTM_HEADS
    dv = MLSTM_DV
    rel = _tri_masks()
    lower = (rel >= 0).astype(F32)
    upper = (rel <= 0).astype(F32)
    exact = dict(precision=lax.Precision.HIGHEST, preferred_element_type=F32)
    ones = jnp.ones((CHUNK, LANE), BF16)
    dirs = ((qf_ref, kf_ref, vf_ref, gf_ref, of_ref, False),
            (qb_ref, kb_ref, vb_ref, gbk_ref, ob_ref, True))
    probs = []
    for d, (q_ref, k_ref, v_ref, g_ref, o_ref, rev) in enumerate(dirs):
        gates = g_ref[...] + gb_ref[...]
        cs_col = jnp.dot(upper if rev else lower, _log_sigmoid(gates), **exact)
        g8 = jnp.transpose(gates)[2 * d * H:2 * d * H + SUBLANE, :]
        cs = jnp.dot(_log_sigmoid(g8), lower if rev else upper, **exact)
        mask = (rel <= 0) if rev else (rel >= 0)
        for h in range(H):
            probs.append(dict(
                r=d * H + h, h=h, rev=rev, mask=mask, q_ref=q_ref, k_ref=k_ref, v_ref=v_ref, o_ref=o_ref,
                i_row=g8[h:h + 1, :], b_row=cs[H + h:H + h + 1, :],
                b_col=cs_col[:, 2 * d * H + H + h:2 * d * H + H + h + 1]))

    nt = (((1,), (1,)), ((), ()))
    for p in probs:
        p["m_prev"] = m_scr[p["r"]:p["r"] + 1, 0:1]
        p["log_d"] = jnp.where(p["mask"], p["b_col"] + (p["i_row"] - p["b_row"]), -jnp.inf)
        p["log_inter"] = p["b_col"] + p["m_prev"]
        p["m_t"] = jnp.maximum(p["log_inter"], jnp.max(p["log_d"], axis=1, keepdims=True))
    yield
    for p in probs:
        h = p["h"]
        p["q"] = (p["q_ref"][:, h * MLSTM_DK:(h + 1) * MLSTM_DK] * (MLSTM_DK ** -0.5)).astype(BF16)
        k = p["k_ref"][:, h * MLSTM_DK:(h + 1) * MLSTM_DK]
        p["k_t"] = jnp.transpose(k)
        p["v_ext"] = jnp.concatenate([p["v_ref"][:, h * dv:(h + 1) * dv].astype(BF16), ones], axis=1)
        p["c_prev"] = c_scr[p["r"]]
        p["qk"] = lax.dot_general(p["q"], k.astype(BF16), nt, preferred_element_type=F32)
        p["qc"] = jnp.dot(p["q"], p["c_prev"].astype(BF16), preferred_element_type=F32)
    yield
    for p in probs:
        r = p["r"]
        b_row = p["b_row"]
        b_last = b_row[:, 0:1] if p["rev"] else b_row[:, CHUNK - 1:CHUNK]
        log_w = b_last - b_row + p["i_row"]
        m_loc = jnp.max(log_w, axis=1, keepdims=True)
        kw_t = (p["k_t"] * jnp.exp(log_w - m_loc)).astype(BF16)
        u = jnp.dot(kw_t, p["v_ext"], preferred_element_type=F32)
        m_new = jnp.maximum(b_last + p["m_prev"], m_loc)
        c_scr[r] = jnp.exp(b_last + p["m_prev"] - m_new) * p["c_prev"] + jnp.exp(m_loc - m_new) * u
        m_scr[r:r + 1, :] = jnp.broadcast_to(m_new, (1, LANE))
    yield
    for p in probs:
        p["s"] = (p["qk"] * jnp.exp(p["log_d"] - p["m_t"])).astype(BF16)
        p["inter"] = jnp.exp(p["log_inter"] - p["m_t"])
    yield
    for p in probs:
        h = p["h"]
        tot = jnp.dot(p["s"], p["v_ext"], preferred_element_type=F32) + p["qc"] * p["inter"]
        den = tot[:, dv:dv + 1]
        p["o_ref"][:, h * dv:(h + 1) * dv] = tot[:, :dv] / jnp.maximum(jnp.abs(den), jnp.exp(-p["m_t"]))


N_RET_IN = 10
N_MLSTM_IN = 8


def _mixer_ab_kernel(dl_ref, gb_ref, *refs):
    ret_in, refs = refs[:N_RET_IN], refs[N_RET_IN:]
    ml_in, refs = refs[:N_MLSTM_IN], refs[N_MLSTM_IN:]
    ret_f_ref, ret_b_ref, ml_f_ref, ml_b_ref, s_scr, c_scr, m_scr = refs

    @pl.when(pl.program_id(1) == 0)
    def _():
        s_scr[...] = jnp.zeros_like(s_scr)
        c_scr[...] = jnp.zeros_like(c_scr)
        m_scr[...] = jnp.full_like(m_scr, NEG_BIG)

    pending = [_mlstm_body(gb_ref, *ml_in, ml_f_ref, ml_b_ref, c_scr, m_scr),
               _ret_body(dl_ref, *ret_in, ret_f_ref, ret_b_ref, s_scr)]
    done = object()
    while pending:
        pending = [g for g in pending if next(g, done) is not done]


def _mixer_ab(p, p_gates, decay_logit, gate_b, cos_t, sin_t, rows):
    n = p.shape[0]
    B = rows.B
    H = MLSTM_HEADS
    n_steps, seq_f, seq_b, unit = _chunk_maps(rows)
    rq, rv = RET_HEADS * RET_DK, RET_HEADS * RET_DV
    mq, mv = H * MLSTM_DK, H * MLSTM_DV
    dl = jnp.broadcast_to(decay_logit.astype(F32).reshape(2 * RET_HEADS, 1), (2 * RET_HEADS, LANE))
    gb = jnp.zeros((1, LANE), F32).at[0, :4 * H].set(gate_b.astype(F32).reshape(4 * H))

    def chunk(width, col0, seq):
        return pl.BlockSpec((CHUNK, width), lambda b, j: (unit(b, seq(j)), col0 // width))

    def ret_specs(seq):
        return [chunk(rq, AB_QR, seq), chunk(rq, AB_KR, seq), chunk(rv, AB_VR, seq),
                pl.BlockSpec((CHUNK, RET_DK), lambda b, j: (seq(j), 0)),
                pl.BlockSpec((CHUNK, RET_DK), lambda b, j: (seq(j), 0))]

    def ml_specs(seq):
        return [chunk(mq, AB_QM, seq), chunk(mq, AB_KM, seq), chunk(mv, AB_VM, seq), chunk(LANE, 0, seq)]

    const = lambda r: pl.BlockSpec((r, LANE), lambda b, j: (0, 0))
    return pl.pallas_call(
        _mixer_ab_kernel,
        grid=(B, n_steps),
        in_specs=([const(2 * RET_HEADS), const(1)] + ret_specs(seq_f) + ret_specs(seq_b)
                  + ml_specs(seq_f) + ml_specs(seq_b)),
        out_specs=[chunk(rv, 0, seq_f), chunk(rv, 0, seq_b), chunk(mv, 0, seq_f), chunk(mv, 0, seq_b)],
        out_shape=[jax.ShapeDtypeStruct((n, rv), F32)] * 2 + [jax.ShapeDtypeStruct((n, mv), F32)] * 2,
        scratch_shapes=[pltpu.VMEM((2 * RET_HEADS, RET_DK, RET_DV), F32),
                        pltpu.VMEM((2 * H, MLSTM_DK, MLSTM_DV + LANE), F32),
                        pltpu.VMEM((2 * SUBLANE, LANE), F32)],
        compiler_params=_params(("arbitrary", "arbitrary")),
        name="mixer_ab",
    )(dl, gb, p, p, p, cos_t, sin_t, p, p, p, cos_t, sin_t, p, p, p, p_gates, p, p, p, p_gates)


def _head_ln(y, g, heads, width):
    outs = []
    for h in range(heads):
        yh = y[:, h * width:(h + 1) * width]
        mu = jnp.mean(yh, axis=-1, keepdims=True)
        yc = yh - mu
        var = jnp.mean(yc * yc, axis=-1, keepdims=True)
        outs.append(yc * lax.rsqrt(var + EPS))
    return jnp.concatenate(outs, axis=1) * g


def _merge_ab_kernel(x_ref, mod_ref, rf_ref, rb_ref, mf_ref, mb_ref, gr_ref, gm_ref, rg_ref, mg_ref,
                     w1_ref, w2_ref, o_ref):
    gr = gr_ref[...]
    ret_y = (gr * _sigmoid(gr)) * _head_ln(rf_ref[...] + rb_ref[...], rg_ref[...], RET_HEADS, RET_DV)
    ml_y = _sigmoid(gm_ref[...]) * _head_ln(mf_ref[...] + mb_ref[...], mg_ref[...], MLSTM_HEADS, MLSTM_DV)
    y = (jnp.dot(ret_y.astype(BF16), w1_ref[...], preferred_element_type=F32)
         + jnp.dot(ml_y.astype(BF16), w2_ref[...], preferred_element_type=F32))
    o_ref[...] = x_ref[...] + mod_ref[5:6, :] * y


def _merge_ab(x, mods, ret_f, ret_b, ml_f, ml_b, p, ret_g, ml_g, w_out, rows, *, tm):
    n, D = x.shape
    rw = RET_HEADS * RET_DV
    mw = MLSTM_HEADS * MLSTM_DV
    row = lambda i: (i, 0)
    return pl.pallas_call(
        _merge_ab_kernel,
        grid=(n // tm,),
        in_specs=[pl.BlockSpec((tm, D), row),
                  pl.BlockSpec((None, N_MOD, D), lambda i: (rows.mod_row(i, tm, True), 0, 0)),
                  pl.BlockSpec((tm, rw), row), pl.BlockSpec((tm, rw), row),
                  pl.BlockSpec((tm, mw), row), pl.BlockSpec((tm, mw), row),
                  pl.BlockSpec((tm, rw), lambda i: (i, AB_GR // rw)),
                  pl.BlockSpec((tm, mw), lambda i: (i, AB_GM // mw)),
                  pl.BlockSpec((1, rw), lambda i: (0, 0)),
                  pl.BlockSpec((1, mw), lambda i: (0, 0)),
                  pl.BlockSpec((rw, D), lambda i: (0, 0)),
                  pl.BlockSpec((mw, D), lambda i: (rw // mw, 0))],
        out_specs=pl.BlockSpec((tm, D), row),
        out_shape=jax.ShapeDtypeStruct((n, D), F32),
        compiler_params=_params(("arbitrary",)),
        name="merge_ab",
    )(x, mods, ret_f, ret_b, ml_f, ml_b, p, p, ret_g.reshape(1, rw), ml_g.reshape(1, mw), w_out, w_out)


def _lru_coef_kernel(x_ref, xp_ref, xn_ref, cw_ref, cb_ref, wa_ref, wx_ref, ba_ref, bx_ref, lam_ref,
                     a_ref, b_ref, *, tm, ctx_tiles, ctx_seg, lat_seg):
    i = pl.program_id(0)
    seg_pos = jnp.where(i < ctx_tiles, i % ctx_seg, (i - ctx_tiles) % lat_seg)
    seg_len = jnp.where(i < ctx_tiles, ctx_seg, lat_seg)
    keep_prev = (seg_pos != 0).astype(F32)
    keep_next = (seg_pos != seg_len - 1).astype(F32)
    xe = jnp.concatenate([xp_ref[...] * keep_prev, x_ref[...], xn_ref[...] * keep_next], axis=0)
    ne = tm + 2 * SUBLANE
    xc = cb_ref[...] + cw_ref[2:3, :] * x_ref[...]
    for tap, off in ((0, -2), (1, -1), (3, 1)):
        shifted = pltpu.roll(xe, (-off) % ne, axis=0)[SUBLANE:SUBLANE + tm, :]
        xc = xc + cw_ref[tap:tap + 1, :] * shifted
    for d in range(2):
        lam = lam_ref[d:d + 1, :]
        sp = jnp.maximum(-lam, 0.0) + jnp.log(1.0 + jnp.exp(-jnp.abs(lam)))
        rate = (-LRU_C * LOG2E) * sp
        for g in range(LRU_BLOCKS):
            sl = slice(g * LRU_BLOCK, (g + 1) * LRU_BLOCK)
            xg = xc[:, sl]
            xg16 = xg.astype(BF16)
            r = _sigmoid(jnp.dot(xg16, wa_ref[d, g], preferred_element_type=F32) + ba_ref[d:d + 1, sl])
            ig = _sigmoid(jnp.dot(xg16, wx_ref[d, g], preferred_element_type=F32) + bx_ref[d:d + 1, sl])
            a = jnp.exp2(r * rate[:, sl])
            a_ref[d, :, sl] = a
            b_ref[d, :, sl] = jnp.sqrt(1.0 - a * a) * (ig * xg)


def _lru_coef(p, conv_w, conv_b, wa, wx, ba, bx, lam, rows, *, tm):
    n = p.shape[0]
    W = LRU_WIDTH
    tpb = tm // SUBLANE
    n8 = n // SUBLANE
    kern = functools.partial(_lru_coef_kernel, tm=tm, ctx_tiles=rows.n_ctx // tm,
                             ctx_seg=rows.Lc // tm, lat_seg=rows.S // tm)
    full = lambda *s: pl.BlockSpec(s, lambda i: (0,) * len(s))
    return pl.pallas_call(
        kern,
        grid=(n // tm,),
        in_specs=[pl.BlockSpec((tm, W), lambda i: (i, CD_XB // W)),
                  pl.BlockSpec((SUBLANE, W), lambda i: (jnp.maximum(i * tpb - 1, 0), CD_XB // W)),
                  pl.BlockSpec((SUBLANE, W), lambda i: (jnp.minimum((i + 1) * tpb, n8 - 1), CD_XB // W)),
                  full(4, W), full(1, W), full(2, LRU_BLOCKS, LRU_BLOCK, LRU_BLOCK),
                  full(2, LRU_BLOCKS, LRU_BLOCK, LRU_BLOCK), full(2, W), full(2, W), full(2, W)],
        out_specs=[pl.BlockSpec((2, tm, W), lambda i: (0, i, 0))] * 2,
        out_shape=[jax.ShapeDtypeStruct((2, n, W), F32)] * 2,
        compiler_params=_params(("arbitrary",)),
        name="lru_coef",
    )(p, p, p, conv_w, conv_b.reshape(1, W), wa, wx, ba, bx, lam)


def _lru_scan_kernel(af_ref, bf_ref, ab_ref, bb_ref, of_ref, ob_ref, h_scr, *, tb, lw):
    @pl.when(pl.program_id(1) == 0)
    def _():
        h_scr[...] = jnp.zeros_like(h_scr)

    row = lax.broadcasted_iota(jnp.int32, (SUBLANE, lw), 0)
    ng = tb // SUBLANE

    def scan8(a, b, rev):
        d = 1
        while d < SUBLANE:
            if rev:
                keep = row < SUBLANE - d
                sh = SUBLANE - d
            else:
                keep = row >= d
                sh = d
            b = a * jnp.where(keep, pltpu.roll(b, sh, axis=0), 0.0) + b
            a = a * jnp.where(keep, pltpu.roll(a, sh, axis=0), 1.0)
            d *= 2
        return a, b

    for c in range(LRU_WIDTH // lw):
        cs = slice(c * lw, (c + 1) * lw)

        def body(g, carry):
            hf, hb = carry
            rf = pl.multiple_of(g * SUBLANE, SUBLANE)
            a, b = scan8(af_ref[pl.ds(rf, SUBLANE), cs], bf_ref[pl.ds(rf, SUBLANE), cs], False)
            out = a * hf + b
            of_ref[pl.ds(rf, SUBLANE), cs] = out
            hf = jnp.broadcast_to(out[SUBLANE - 1:SUBLANE, :], (SUBLANE, lw))
            rb = pl.multiple_of((ng - 1 - g) * SUBLANE, SUBLANE)
            a, b = scan8(ab_ref[pl.ds(rb, SUBLANE), cs], bb_ref[pl.ds(rb, SUBLANE), cs], True)
            out = a * hb + b
            ob_ref[pl.ds(rb, SUBLANE), cs] = out
            hb = jnp.broadcast_to(out[0:1, :], (SUBLANE, lw))
            return hf, hb

        hf, hb = lax.fori_loop(0, ng, body, (h_scr[0, :, cs], h_scr[1, :, cs]))
        h_scr[0, :, cs] = hf
        h_scr[1, :, cs] = hb


def _lru_scan(a, b, rows):
    n = a.shape[1]
    W = LRU_WIDTH
    B = rows.B
    tb = rows.Lc
    nlb = rows.S // tb

    def blk_f(b_, j):
        return jnp.where(j == 0, b_, B + b_ * nlb + (j - 1))

    def blk_b(b_, j):
        return jnp.where(j == 0, b_, B + b_ * nlb + (nlb - j))

    def spec(d, blk):
        return pl.BlockSpec((None, tb, W), lambda b_, j: (d, blk(b_, j), 0))

    kern = functools.partial(_lru_scan_kernel, tb=tb, lw=512)
    return pl.pallas_call(
        kern,
        grid=(B, 1 + nlb),
        in_specs=[spec(0, blk_f), spec(0, blk_f), spec(1, blk_b), spec(1, blk_b)],
        out_specs=[pl.BlockSpec((tb, W), lambda b_, j: (blk_f(b_, j), 0)),
                   pl.BlockSpec((tb, W), lambda b_, j: (blk_b(b_, j), 0))],
        out_shape=[jax.ShapeDtypeStruct((n, W), F32)] * 2,
        scratch_shapes=[pltpu.VMEM((2, SUBLANE, W), F32)],
        compiler_params=_params(("arbitrary", "arbitrary")),
        name="lru_scan",
    )(a, b, a, b)


def _mla_qkv_kernel(cq_ref, ckv_ref, kr_ref, qg_ref, kvg_ref, wq_ref, wk_ref, wv_ref, qkg_ref, cos_ref, sin_ref,
                    q_ref, k_ref, v_ref):
    cqn = _rms(cq_ref[...], qg_ref[...]).astype(BF16)
    ckvn = _rms(ckv_ref[...], kvg_ref[...]).astype(BF16)
    q_all = jnp.dot(cqn, wq_ref[...], preferred_element_type=F32)
    kn_all = jnp.dot(ckvn, wk_ref[...], preferred_element_type=F32)
    v_all = jnp.dot(ckvn, wv_ref[...], preferred_element_type=F32).astype(BF16)
    ones = jnp.ones((v_all.shape[0], LANE), BF16)
    for h in range(MLA_HEADS):
        v_ref[:, h * (MLA_V + LANE):(h + 1) * (MLA_V + LANE)] = jnp.concatenate(
            [v_all[:, h * MLA_V:(h + 1) * MLA_V], ones], axis=1)
    kr = kr_ref[...]
    cos = cos_ref[...]
    sin = sin_ref[...]
    lane = lax.broadcasted_iota(jnp.int32, cos.shape, 1)
    first_half = (lane % (MLA_ROPE // 2)) < (MLA_ROPE // 4)
    hi_w = MLA_PAD - MLA_NOPE

    def norm_rope(lo, hi, g, scale):
        ssq = jnp.sum(lo * lo + hi * hi, axis=-1, keepdims=True)
        r = lax.rsqrt(ssq * (1.0 / MLA_QK) + EPS) * scale
        hi = hi * r * g[:, MLA_NOPE:]
        rot = jnp.where(first_half, pltpu.roll(hi, hi_w - MLA_ROPE // 4, axis=1),
                        pltpu.roll(hi, MLA_ROPE // 4, axis=1))
        return (lo * r * g[:, :MLA_NOPE]).astype(BF16), (hi * cos + rot * sin).astype(BF16)

    for h in range(MLA_HEADS):
        lo_sl = slice(h * MLA_PAD, h * MLA_PAD + MLA_NOPE)
        hi_sl = slice(h * MLA_PAD + MLA_NOPE, (h + 1) * MLA_PAD)
        q_ref[:, lo_sl], q_ref[:, hi_sl] = norm_rope(q_all[:, lo_sl], q_all[:, hi_sl], qkg_ref[0:1, :],
                                                     MLA_QK ** -0.5 * LOG2E)
        k_ref[:, lo_sl], k_ref[:, hi_sl] = norm_rope(kn_all[:, h * MLA_NOPE:(h + 1) * MLA_NOPE], kr,
                                                     qkg_ref[1:2, :], 1.0)


def _mla_qkv(p, q_norm_g, kv_norm_g, wq, wk, wv, qk_g, cos_t, sin_t, rows, *, tm):
    B, Lc, S = rows.B, rows.Lc, rows.S
    n = p.shape[0]
    nct = rows.n_ctx // tm
    cpb = Lc // tm
    lpb = S // tm

    def seq_blk(i):
        il = jnp.maximum(i - nct, 0)
        return (jnp.where(i < nct, i // cpb, il // lpb), jnp.where(i < nct, lpb + i % cpb, il % lpb), 0)

    def pos_blk(i):
        return (jnp.where(i < nct, i % cpb, cpb + jnp.maximum(i - nct, 0) % lpb), 0)

    full = lambda *s: pl.BlockSpec(s, lambda i: (0,) * len(s))
    qkw = MLA_HEADS * MLA_PAD
    vw = MLA_HEADS * MLA_V
    return pl.pallas_call(
        _mla_qkv_kernel,
        grid=(n // tm,),
        in_specs=[pl.BlockSpec((tm, MLA_Q_RANK), lambda i: (i, CD_CQ // MLA_Q_RANK)),
                  pl.BlockSpec((tm, MLA_KV_RANK), lambda i: (i, CD_CKV // MLA_KV_RANK)),
                  pl.BlockSpec((tm, LANE), lambda i: (i, CD_KR // LANE)),
                  full(1, MLA_Q_RANK), full(1, MLA_KV_RANK), full(MLA_Q_RANK, qkw),
                  full(MLA_KV_RANK, MLA_HEADS * MLA_NOPE), full(MLA_KV_RANK, vw), full(2, MLA_PAD),
                  pl.BlockSpec((tm, MLA_PAD - MLA_NOPE), pos_blk),
                  pl.BlockSpec((tm, MLA_PAD - MLA_NOPE), pos_blk)],
        out_specs=[pl.BlockSpec((None, tm, qkw), seq_blk), pl.BlockSpec((None, tm, qkw), seq_blk),
                   pl.BlockSpec((None, tm, vw + MLA_HEADS * LANE), seq_blk)],
        out_shape=[jax.ShapeDtypeStruct((B, Lc + S, qkw), BF16), jax.ShapeDtypeStruct((B, Lc + S, qkw), BF16),
                   jax.ShapeDtypeStruct((B, Lc + S, vw + MLA_HEADS * LANE), BF16)],
        compiler_params=_params(("arbitrary",)),
        name="mla_qkv",
    )(p, p, p, q_norm_g.reshape(1, -1), kv_norm_g.reshape(1, -1), wq, wk, wv, qk_g, cos_t, sin_t)


def _attn_kernel(q_ref, k_ref, v_ref, o_ref, *, sub):
    nt = (((1,), (1,)), ((), ()))
    groups = list(range(0, q_ref.shape[0], sub))

    def scores(r):
        return lax.dot_general(q_ref[r:r + sub, :], k_ref[...], nt, preferred_element_type=F32)

    s_next = scores(groups[0])
    for g, r in enumerate(groups):
        s = s_next
        if g + 1 < len(groups):
            s_next = scores(groups[g + 1])
        e = jnp.exp2(s - jnp.max(s, axis=-1, keepdims=True)).astype(BF16)
        ov = jnp.dot(e, v_ref[...], preferred_element_type=F32)
        o_ref[r:r + sub, :] = (ov[:, :MLA_V] / ov[:, MLA_V:MLA_V + 1]).astype(BF16)


def _attention(q, k, v, rows, *, tq):
    B, Lc, S = rows.B, rows.Lc, rows.S
    nq = S // tq
    return pl.pallas_call(
        functools.partial(_attn_kernel, sub=min(tq, 256)),
        grid=(B, MLA_HEADS, nq),
        in_specs=[pl.BlockSpec((None, tq, MLA_PAD), lambda b, h, i: (b, i, h)),
                  pl.BlockSpec((None, Lc + S, MLA_PAD), lambda b, h, i: (b, 0, h)),
                  pl.BlockSpec((None, Lc + S, MLA_V + LANE), lambda b, h, i: (b, 0, h))],
        out_specs=pl.BlockSpec((tq, MLA_V), lambda b, h, i: (b * nq + i, h)),
        out_shape=jax.ShapeDtypeStruct((B * S, MLA_HEADS * MLA_V), BF16),
        compiler_params=_params(("arbitrary", "arbitrary", "arbitrary")),
        name="attention",
    )(q, k, v)


def _gelu_tanh(x):
    return 0.5 * x * (1.0 + jnp.tanh(math.sqrt(2.0 / math.pi) * (x + 0.044715 * (x * x * x))))


def _merge_cd_kernel(x_ref, mod_ref, yb_ref, hf_ref, hb_ref, att_ref, w1_ref, w2_ref, o_ref):
    y1 = _gelu_tanh(yb_ref[...]) * (hf_ref[...] + hb_ref[...])
    y = (jnp.dot(y1.astype(BF16), w1_ref[...], preferred_element_type=F32)
         + jnp.dot(att_ref[...], w2_ref[...], preferred_element_type=F32))
    o_ref[...] = x_ref[...] + mod_ref[5:6, :] * y


def _merge_cd(x, mods, p, h_f, h_b, att, w_out, rows, *, tm):
    D = x.shape[1]
    W = LRU_WIDTH
    aw = MLA_HEADS * MLA_V
    off = rows.n_ctx // tm
    lat = lambda i: (i + off, 0)
    return pl.pallas_call(
        _merge_cd_kernel,
        grid=(rows.n_lat // tm,),
        in_specs=[pl.BlockSpec((tm, D), lat),
                  pl.BlockSpec((None, N_MOD, D), lambda i: (rows.mod_row(i, tm, False), 0, 0)),
                  pl.BlockSpec((tm, W), lambda i: (i + off, CD_YB // W)),
                  pl.BlockSpec((tm, W), lat), pl.BlockSpec((tm, W), lat),
                  pl.BlockSpec((tm, aw), lambda i: (i, 0)),
                  pl.BlockSpec((W, D), lambda i: (0, 0)),
                  pl.BlockSpec((aw, D), lambda i: (W // aw, 0))],
        out_specs=pl.BlockSpec((tm, D), lambda i: (i, 0)),
        out_shape=jax.ShapeDtypeStruct((rows.n_lat, D), F32),
        compiler_params=_params(("arbitrary",)),
        name="merge_cd",
    )(x, mods, p, h_f, h_b, att, w_out, w_out)


def _ret_tables(rows):
    half = RET_DK // 2
    freqs = ROPE_BASE ** (-jnp.arange(half, dtype=F32) / half)
    ang = jnp.arange(rows.S, dtype=F32)[:, None] * freqs
    cos = jnp.concatenate([jnp.cos(ang), jnp.cos(ang)], axis=-1)
    sin = jnp.concatenate([-jnp.sin(ang), jnp.sin(ang)], axis=-1)
    cos = jnp.concatenate([jnp.ones((rows.Lc, RET_DK), F32), cos], axis=0)
    sin = jnp.concatenate([jnp.zeros((rows.Lc, RET_DK), F32), sin], axis=0)
    return cos, sin


def _mla_tables(rows):
    S = rows.S
    quarter = MLA_ROPE // 4
    freqs = ROPE_BASE ** (-jnp.arange(quarter, dtype=F32) / quarter)
    t = jnp.arange(S)
    row = (t // GRID_W).astype(F32)
    col = (t % GRID_W).astype(F32)

    def part(pos):
        ang = pos[:, None] * freqs
        return (jnp.concatenate([jnp.cos(ang), jnp.cos(ang)], axis=-1),
                jnp.concatenate([-jnp.sin(ang), jnp.sin(ang)], axis=-1))

    rc, rs = part(row)
    cc, cs = part(col)
    tail = MLA_PAD - MLA_QK
    cos = jnp.concatenate([rc, cc, jnp.ones((S, tail), F32)], axis=-1)
    sin = jnp.concatenate([rs, cs, jnp.zeros((S, tail), F32)], axis=-1)
    cos = jnp.concatenate([jnp.ones((rows.Lc, MLA_ROPE + tail), F32), cos], axis=0)
    sin = jnp.concatenate([jnp.zeros((rows.Lc, MLA_ROPE + tail), F32), sin], axis=0)
    return cos, sin


def _pad_cols(w, n):
    return jnp.pad(w, ((0, 0), (0, n - w.shape[1])))


def _head_pad(w, heads, width, padded):
    K = w.shape[0]
    return jnp.pad(w.reshape(K, heads, width), ((0, 0), (0, 0), (0, padded - width))).reshape(K, heads * padded)


def kernel(x, c, ctx, c_ctx, ada_w, ada_b, norm_g, ffn_wg, ffn_wu, ffn_wd, ab_w_in, ab_w_out, ret_decay_logit, ret_gn_g, mlstm_gate_b, mlstm_gn_g, cd_w_in, cd_w_out, lru_conv_w, lru_conv_b, lru_wa, lru_ba, lru_wx, lru_bx, lru_lambda, mla_q_norm_g, mla_kv_norm_g, mla_w_uq, mla_w_uk, mla_w_uv, mla_qk_norm_g):
    B, S, D = x.shape
    Lc = ctx.shape[1]
    depth = ada_w.shape[0]
    F = ffn_wg.shape[-1]
    rows = Rows(B, Lc, S)
    assert B < SUBLANE and Lc % CHUNK == 0 and S % Lc == 0 and S % GRID_W == 0

    tm = min(1024, rows.n_ctx)
    tf = 512
    all_mod = lambda i: rows.mod_row(i, tm, True)
    lat_mod = lambda i: rows.mod_row(i, tm, False)

    cond = jnp.zeros((SUBLANE, D), F32).at[:B].set(c.astype(F32)).at[B].set(c_ctx.astype(F32))
    mods = _adaln(cond, ada_w, ada_b)

    ffn_stacks = (ffn_wg, ffn_wu, ffn_wd)
    w_cur = tuple(w[0, 0].astype(BF16) for w in ffn_stacks)

    def ffn_casts(l, h):
        return [(w, (l, h)) for w in ffn_stacks] if l < depth else []

    for l in range(depth):
        last = l == depth - 1
        j = l // 2
        m_l = mods[l]
        ffn1 = functools.partial(_ffn, mods=m_l, g=norm_g[l, 0], wg=w_cur[0], wu=w_cur[1], wd=w_cur[2],
                                 mod_base=0, tm=tm, tf=tf)
        casts = ffn_casts(l, 1)
        if l == 0:
            xa, w_next = ffn1(x.reshape(B * S, D).astype(F32), mod_of_tile=lat_mod, out_rows=rows.n_all,
                              out_tile0=rows.n_ctx // tm, casts=casts)
            xa = ffn1(ctx.reshape(B * Lc, D).astype(F32), mod_of_tile=lambda i: B, out_rows=rows.n_all, into=xa)
        else:
            xa, w_next = ffn1(xa, mod_of_tile=all_mod, casts=casts)
        w_cur = w_next

        if l % 2 == 0:
            w_in, w_out = ab_w_in[j].astype(BF16), ab_w_out[j].astype(BF16)
            p, p_gates = _proj(xa, m_l, norm_g[l, 1], w_in, rows, tm=tm, tn=1024, n_cols=AB_GATES,
                               w_tail=_pad_cols(w_in[:, AB_GATES:], LANE))
            cos_t, sin_t = _ret_tables(rows)
            ret_f, ret_b, ml_f, ml_b = _mixer_ab(p, p_gates, ret_decay_logit[j], mlstm_gate_b[j], cos_t, sin_t,
                                                 rows)
            xa = _merge_ab(xa, m_l, ret_f, ret_b, ml_f, ml_b, p, ret_gn_g[j], mlstm_gn_g[j], w_out, rows,
                           tm=min(256, tm))
            if last:
                xa = xa[rows.n_ctx:]
        else:
            assert last, "context outputs of the recurrent/attention mixer are not produced"
            w_in, w_out, w_uq, w_uk, w_uv = (w[j].astype(BF16) for w in
                                             (cd_w_in, cd_w_out, mla_w_uq, mla_w_uk, mla_w_uv))
            p, = _proj(xa, m_l, norm_g[l, 1], _pad_cols(w_in, CD_PAD), rows, tm=tm, tn=CD_PAD // 3)
            a_c, b_c = _lru_coef(p, lru_conv_w[j], lru_conv_b[j], lru_wa[j].astype(BF16), lru_wx[j].astype(BF16),
                                 lru_ba[j], lru_bx[j], lru_lambda[j], rows, tm=min(256, Lc))
            h_f, h_b = _lru_scan(a_c, b_c, rows)
            cos_t, sin_t = _mla_tables(rows)
            qk_g = jnp.pad(mla_qk_norm_g[j].astype(F32), ((0, 0), (0, MLA_PAD - MLA_QK)))
            q, k, v = _mla_qkv(p, mla_q_norm_g[j], mla_kv_norm_g[j], _head_pad(w_uq, MLA_HEADS, MLA_QK, MLA_PAD),
                               w_uk, w_uv, qk_g, cos_t, sin_t, rows, tm=min(256, Lc))
            att = _attention(q, k, v, rows, tq=min(2048, S))
            xa = _merge_cd(xa, m_l, p, h_f, h_b, att, w_out, rows, tm=min(256, tm))

        casts = ffn_casts(l + 1, 0)
        res = _ffn(xa, m_l, norm_g[l, 2], *w_cur, mod_base=6, mod_of_tile=lat_mod if last else all_mod,
                   tm=tm, tf=tf, casts=casts)
        xa, w_cur = res if casts else (res, None)

    if xa.shape[0] != rows.n_lat:
        xa = xa[rows.n_ctx:]
    return xa.reshape(B, S, D).astype(x.dtype)
```

```python
import functools
import math

import jax
import jax.numpy as jnp
import numpy as np
from jax import lax
from jax.experimental import pallas as pl
from jax.experimental.pallas import tpu as pltpu

F32 = jnp.float32
BF16 = jnp.bfloat16

N_MOD = 9
EPS = 1e-6
CHUNK = 128
GRID_W = 64
ROPE_BASE = 10000.0
NEG_BIG = -1e30

RET_HEADS = 4
RET_DK = 128
RET_DV = 256
MLSTM_HEADS = 4
MLSTM_DK = 128
MLSTM_DV = 256
LRU_WIDTH = 1024
LRU_BLOCKS = 8
LRU_BLOCK = LRU_WIDTH // LRU_BLOCKS
LRU_C = 8.0
MLA_HEADS = 8
MLA_Q_RANK = 512
MLA_KV_RANK = 256
MLA_NOPE = 128
MLA_ROPE = 64
MLA_V = 128
MLA_QK = MLA_NOPE + MLA_ROPE
LOG2E = math.log2(math.e)
MLA_PAD = 256

AB_SPLITS = (RET_HEADS * RET_DK, RET_HEADS * RET_DK, RET_HEADS * RET_DV, RET_HEADS * RET_DV,
             MLSTM_HEADS * MLSTM_DK, MLSTM_HEADS * MLSTM_DK, MLSTM_HEADS * MLSTM_DV, MLSTM_HEADS * MLSTM_DV,
             4 * MLSTM_HEADS)
AB_IN = sum(AB_SPLITS)
CD_SPLITS = (LRU_WIDTH, LRU_WIDTH, MLA_Q_RANK, MLA_KV_RANK, MLA_ROPE)
CD_IN = sum(CD_SPLITS)

LANE = 128
SUBLANE = 8
BF16_SUBLANES = 16
VMEM_LIMIT = 56 * 1024 * 1024
FFN_VMEM_LIMIT = 60 * 1024 * 1024

AB_QR, AB_KR, AB_VR, AB_GR, AB_QM, AB_KM, AB_VM, AB_GM, AB_GATES = np.cumsum((0,) + AB_SPLITS[:-1]).tolist()
CD_YB, CD_XB, CD_CQ, CD_CKV, CD_KR = np.cumsum((0,) + CD_SPLITS[:-1]).tolist()
CD_PAD = 3072


def _params(sem, vmem=VMEM_LIMIT):
    return pltpu.CompilerParams(dimension_semantics=sem, vmem_limit_bytes=vmem)


def _round_up(n, m):
    return (n + m - 1) // m * m


def _sigmoid(x):
    return 0.5 * jnp.tanh(0.5 * x) + 0.5


def _log_sigmoid(x):
    return jnp.minimum(x, 0.0) - jnp.log(1.0 + jnp.exp(-jnp.abs(x)))


def _rms(x, g):
    return x * lax.rsqrt(jnp.mean(x * x, axis=-1, keepdims=True) + EPS) * g


def _adaln_kernel(c_ref, w_ref, b_ref, o_ref):
    c = c_ref[...]
    s = (c * _sigmoid(c)).astype(BF16)
    o_ref[...] = jnp.dot(s, w_ref[...].astype(BF16), preferred_element_type=F32) + b_ref[...]


def _adaln(cond, ada_w, ada_b):
    L, D, N = ada_w.shape
    tn = 1024 if N % 1024 == 0 else N
    out = pl.pallas_call(
        _adaln_kernel,
        grid=(L, N // tn),
        in_specs=[pl.BlockSpec((SUBLANE, D), lambda l, n: (0, 0)),
                  pl.BlockSpec((None, D, tn), lambda l, n: (l, 0, n)),
                  pl.BlockSpec((None, 1, tn), lambda l, n: (l, 0, n))],
        out_specs=pl.BlockSpec((None, SUBLANE, tn), lambda l, n: (l, 0, n)),
        out_shape=jax.ShapeDtypeStruct((L, SUBLANE, N), F32),
        compiler_params=_params(("arbitrary", "arbitrary")),
        name="adaln",
    )(cond, ada_w, ada_b.reshape(L, 1, N))
    return out.reshape(L, SUBLANE, N_MOD, D)


class Rows:
    def __init__(self, B, Lc, S):
        self.B, self.Lc, self.S = B, Lc, S
        self.n_ctx = B * Lc
        self.n_lat = B * S
        self.n_all = self.n_ctx + self.n_lat

    def mod_row(self, tile, tm, with_ctx):
        if not with_ctx:
            return (tile * tm) // self.S
        nct = self.n_ctx // tm
        return jnp.where(tile < nct, self.B, (jnp.maximum(tile - nct, 0) * tm) // self.S)


def _ffn_kernel(*refs, mod_base, sub, last_cols, has_into, n_cast):
    x_ref, mod_ref, g_ref, wg_ref, wu_ref, wd_ref = refs[:6]
    cast_src = refs[6:6 + n_cast]
    o_ref = refs[6 + n_cast + has_into]
    cast_dst = refs[7 + n_cast + has_into:7 + 2 * n_cast + has_into]
    h_scr = refs[-1]
    f = pl.program_id(1)
    nf = pl.num_programs(1)
    tm = x_ref.shape[0]
    tf = wg_ref.shape[1]

    def step(cols, first, final):
        for src, dst in zip(cast_src, cast_dst):
            dst[...] = src[...].astype(BF16)
        if first:
            shift = mod_ref[mod_base:mod_base + 1, :]
            scale = 1.0 + mod_ref[mod_base + 1:mod_base + 2, :]
        if final:
            gate = 0.5 * mod_ref[mod_base + 2:mod_base + 3, :]
        for r in range(0, tm, sub):
            rs = slice(r, r + sub)
            if first:
                h = (_rms(x_ref[rs, :], g_ref[...]) * scale + shift).astype(BF16)
                h_scr[rs, :] = h
            else:
                h = h_scr[rs, :]
            a = jnp.dot(h, wg_ref[:, :cols], preferred_element_type=F32)
            u = jnp.dot(h, wu_ref[:, :cols], preferred_element_type=F32)
            act = (a * _sigmoid(a) * u).astype(BF16)
            part = jnp.dot(act, wd_ref[:cols, :], preferred_element_type=F32)
            if first:
                o_ref[rs, :] = part
            elif final:
                o_ref[rs, :] = x_ref[rs, :] + gate * (o_ref[rs, :] + part)
            else:
                o_ref[rs, :] += part

    pl.when(f == 0)(functools.partial(step, tf, True, False))
    pl.when(jnp.logical_and(f > 0, f < nf - 1))(functools.partial(step, tf, False, False))
    pl.when(f == nf - 1)(functools.partial(step, last_cols, False, True))


def _ffn(x, mods, g, wg, wu, wd, *, mod_base, mod_of_tile, tm, tf, out_rows=None, out_tile0=0, into=None,
         casts=()):
    n, D = x.shape
    F = wg.shape[-1]
    nf = pl.cdiv(F, tf)
    ni = n // tm
    assert nf >= 2 and n % tm == 0
    out_rows = n if out_rows is None else out_rows
    in_specs = [pl.BlockSpec((tm, D), lambda i, f: (i, 0)),
                pl.BlockSpec((None, N_MOD, D), lambda i, f: (mod_of_tile(i), 0, 0)),
                pl.BlockSpec((1, D), lambda i, f: (0, 0)),
                pl.BlockSpec((D, tf), lambda i, f: (0, f)),
                pl.BlockSpec((D, tf), lambda i, f: (0, f)),
                pl.BlockSpec((tf, D), lambda i, f: (f, 0))]
    args = [x, mods, g.reshape(1, D), wg, wu, wd]
    out_specs = [pl.BlockSpec((tm, D), lambda i, f: (out_tile0 + i, 0))]
    out_shape = [jax.ShapeDtypeStruct((out_rows, D), F32)]
    n_cast = len(casts)
    for w, lead in casts:
        rows_w, cols_w = w.shape[len(lead):]
        slab_rows = next(m for m in range(BF16_SUBLANES * pl.cdiv(rows_w, BF16_SUBLANES * ni * nf),
                                          rows_w + 1, BF16_SUBLANES) if rows_w % m == 0)
        nblk = rows_w // slab_rows
        slab = lambda i, f, nblk=nblk: jnp.minimum(i * nf + f, nblk - 1)
        in_specs.append(pl.BlockSpec((None,) * len(lead) + (slab_rows, cols_w),
                                     lambda i, f, slab=slab, lead=lead: lead + (slab(i, f), 0)))
        out_specs.append(pl.BlockSpec((slab_rows, cols_w), lambda i, f, slab=slab: (slab(i, f), 0)))
        out_shape.append(jax.ShapeDtypeStruct((rows_w, cols_w), BF16))
        args.append(w)
    aliases = {}
    if into is not None:
        assert into.shape == (out_rows, D)
        aliases = {len(args): 0}
        in_specs.append(pl.BlockSpec(memory_space=pl.ANY))
        args.append(into)
    kern = functools.partial(_ffn_kernel, mod_base=mod_base, sub=min(tm, 512), last_cols=F - (nf - 1) * tf,
                             has_into=int(into is not None), n_cast=n_cast)
    res = pl.pallas_call(
        kern,
        grid=(ni, nf),
        in_specs=in_specs,
        out_specs=out_specs,
        out_shape=out_shape,
        scratch_shapes=[pltpu.VMEM((tm, D), BF16)],
        input_output_aliases=aliases,
        compiler_params=_params(("arbitrary", "arbitrary"), FFN_VMEM_LIMIT),
        name="ffn",
    )(*args)
    return (res[0], tuple(res[1:])) if n_cast else res[0]


def _proj_kernel(x_ref, mod_ref, g_ref, w_ref, *rest, sub):
    wt_ref, o_ref, ot_ref, h_scr = rest if len(rest) == 4 else (None, rest[0], None, rest[1])
    tm = x_ref.shape[0]

    @pl.when(pl.program_id(1) == 0)
    def _():
        scale = 1.0 + mod_ref[4:5, :]
        for r in range(0, tm, sub):
            h = (_rms(x_ref[r:r + sub, :], g_ref[...]) * scale + mod_ref[3:4, :]).astype(BF16)
            h_scr[r:r + sub, :] = h
            o_ref[r:r + sub, :] = jnp.dot(h, w_ref[...], preferred_element_type=F32)
            if wt_ref is not None:
                ot_ref[r:r + sub, :] = jnp.dot(h, wt_ref[...], preferred_element_type=F32)

    @pl.when(pl.program_id(1) != 0)
    def _():
        for r in range(0, tm, sub):
            o_ref[r:r + sub, :] = jnp.dot(h_scr[r:r + sub, :], w_ref[...], preferred_element_type=F32)


def _proj(x, mods, g, w, rows, *, tm, tn, n_cols=None, w_tail=None):
    n, D = x.shape
    N = w.shape[1] if n_cols is None else n_cols
    assert N % tn == 0
    in_specs = [pl.BlockSpec((tm, D), lambda i, j: (i, 0)),
                pl.BlockSpec((None, N_MOD, D), lambda i, j: (rows.mod_row(i, tm, True), 0, 0)),
                pl.BlockSpec((1, D), lambda i, j: (0, 0)),
                pl.BlockSpec((D, tn), lambda i, j: (0, j))]
    out_specs = [pl.BlockSpec((tm, tn), lambda i, j: (i, j))]
    out_shape = [jax.ShapeDtypeStruct((n, N), F32)]
    args = [x, mods, g.reshape(1, D), w]
    if w_tail is not None:
        in_specs.append(pl.BlockSpec((D, LANE), lambda i, j: (0, 0)))
        out_specs.append(pl.BlockSpec((tm, LANE), lambda i, j: (i, 0)))
        out_shape.append(jax.ShapeDtypeStruct((n, LANE), F32))
        args.append(w_tail)
    return pl.pallas_call(
        functools.partial(_proj_kernel, sub=min(tm, 512)),
        grid=(n // tm, N // tn),
        in_specs=in_specs,
        out_specs=out_specs,
        out_shape=out_shape,
        scratch_shapes=[pltpu.VMEM((tm, D), BF16)],
        compiler_params=_params(("arbitrary", "arbitrary")),
        name="proj",
    )(*args)


def _chunk_maps(rows):
    B = rows.B
    nc = rows.Lc // CHUNK
    nl = rows.S // CHUNK

    def seq_f(j):
        return j

    def seq_b(j):
        return jnp.where(j < nc, nc - 1 - j, nc + nl - 1 - (j - nc))

    def unit(b, s):
        return jnp.where(s < nc, b * nc + s, B * nc + b * nl + (s - nc))

    return nc + nl, seq_f, seq_b, unit


def _rope128(x, cos, sin):
    return x * cos + pltpu.roll(x, 64, axis=1) * sin


def _tri_masks():
    r = lax.broadcasted_iota(jnp.int32, (CHUNK, CHUNK), 0)
    c = lax.broadcasted_iota(jnp.int32, (CHUNK, CHUNK), 1)
    return r - c


def _ret_body(dl_ref, qf_ref, kf_ref, vf_ref, cf_ref, sf_ref, ctf_ref, stf_ref,
              qb_ref, kb_ref, vb_ref, cb_ref, sb_ref, ctb_ref, stb_ref, of_ref, ob_ref, s_scr):
    lg_all = _log_sigmoid(dl_ref[...])
    rel = _tri_masks().astype(F32)
    pos_c = lax.broadcasted_iota(jnp.int32, (CHUNK, 1), 0).astype(F32)
    pos_r = lax.broadcasted_iota(jnp.int32, (1, CHUNK), 1).astype(F32)
    dirs = ((qf_ref, kf_ref, vf_ref, cf_ref, sf_ref, ctf_ref, stf_ref, of_ref, False),
            (qb_ref, kb_ref, vb_ref, cb_ref, sb_ref, ctb_ref, stb_ref, ob_ref, True))
    probs = []
    for d, (q_ref, k_ref, v_ref, c_ref, sn_ref, ct_ref, st_ref, o_ref, rev) in enumerate(dirs):
        dist = -rel if rev else rel
        step_c = (CHUNK - 1.0 - pos_c) if rev else pos_c
        step_r = (CHUNK - 1.0 - pos_r) if rev else pos_r
        for h in range(RET_HEADS):
            probs.append(dict(r=d * RET_HEADS + h, h=h, q_ref=q_ref, k_ref=k_ref, v_ref=v_ref, c_ref=c_ref,
                              sn_ref=sn_ref, ct_ref=ct_ref, st_ref=st_ref, o_ref=o_ref, dist=dist,
                              step_c=step_c, step_r=step_r))

    for p in probs:
        h, r = p["h"], p["r"]
        p["lg"] = lg_all[r:r + 1, 0:1]
        q = _rope128(p["q_ref"][:, h * RET_DK:(h + 1) * RET_DK], p["c_ref"][...], p["sn_ref"][...]).astype(BF16)
        k_t = jnp.transpose(p["k_ref"][:, h * RET_DK:(h + 1) * RET_DK])
        k_t = k_t * p["ct_ref"][...] + pltpu.roll(k_t, RET_DK // 2, axis=0) * p["st_ref"][...]
        p["k_t"] = k_t * (RET_DK ** -0.5)
        p["v"] = p["v_ref"][:, h * RET_DV:(h + 1) * RET_DV].astype(BF16)
        p["s_prev"] = s_scr[r]
        p["sc"] = jnp.dot(q, p["k_t"].astype(BF16), preferred_element_type=F32)
        p["cross"] = jnp.dot(q, p["s_prev"].astype(BF16), preferred_element_type=F32)
    yield
    for p in probs:
        lg = p["lg"]
        zeta = jnp.exp((CHUNK - 1.0 - p["step_r"]) * lg)
        u = jnp.dot((p["k_t"] * zeta).astype(BF16), p["v"], preferred_element_type=F32)
        s_scr[p["r"]] = jnp.exp(CHUNK * lg) * p["s_prev"] + u
    yield
    for p in probs:
        h, lg, dist = p["h"], p["lg"], p["dist"]
        decay = jnp.where(dist >= 0, jnp.exp(jnp.maximum(dist, 0.0) * lg), 0.0)
        inner = jnp.dot((p["sc"] * decay).astype(BF16), p["v"], preferred_element_type=F32)
        xi = jnp.exp((p["step_c"] + 1.0) * lg)
        p["o_ref"][:, h * RET_DV:(h + 1) * RET_DV] = inner + p["cross"] * xi


def _mlstm_body(gb_ref, qf_ref, kf_ref, vf_ref, gf_ref, qb_ref, kb_ref, vb_ref, gbk_ref,
                of_ref, ob_ref, c_scr, m_scr):
    H = MLSTM_HEADS
    dv = MLSTM_DV
    rel = _tri_masks()
    lower = (rel >= 0).astype(F32)
    upper = (rel <= 0).astype(F32)
    exact = dict(precision=lax.Precision.HIGHEST, preferred_element_type=F32)
    ones = jnp.ones((CHUNK, LANE), BF16)
    dirs = ((qf_ref, kf_ref, vf_ref, gf_ref, of_ref, False),
            (qb_ref, kb_ref, vb_ref, gbk_ref, ob_ref, True))
    probs = []
    for d, (q_ref, k_ref, v_ref, g_ref, o_ref, rev) in enumerate(dirs):
        gates = g_ref[...] + gb_ref[...]
        cs_col = jnp.dot(upper if rev else lower, _log_sigmoid(gates), **exact)
        g8 = jnp.transpose(gates)[2 * d * H:2 * d * H + SUBLANE, :]
        cs = jnp.dot(_log_sigmoid(g8), lower if rev else upper, **exact)
        mask = (rel <= 0) if rev else (rel >= 0)
        for h in range(H):
            probs.append(dict(
                r=d * H + h, h=h, rev=rev, mask=mask, q_ref=q_ref, k_ref=k_ref, v_ref=v_ref, o_ref=o_ref,
                i_row=g8[h:h + 1, :], b_row=cs[H + h:H + h + 1, :],
                b_col=cs_col[:, 2 * d * H + H + h:2 * d * H + H + h + 1]))

    nt = (((1,), (1,)), ((), ()))
    for p in probs:
        h = p["h"]
        p["q"] = (p["q_ref"][:, h * MLSTM_DK:(h + 1) * MLSTM_DK] * (MLSTM_DK ** -0.5)).astype(BF16)
        p["k"] = p["k_ref"][:, h * MLSTM_DK:(h + 1) * MLSTM_DK]
        p["c_prev"] = c_scr[p["r"]]
        p["qk"] = lax.dot_general(p["q"], p["k"].astype(BF16), nt, preferred_element_type=F32)
        p["qc"] = jnp.dot(p["q"], p["c_prev"].astype(BF16), preferred_element_type=F32)
    for p in probs:
        p["m_prev"] = m_scr[p["r"]:p["r"] + 1, 0:1]
        p["log_d"] = jnp.where(p["mask"], p["b_col"] + (p["i_row"] - p["b_row"]), -jnp.inf)
        p["log_inter"] = p["b_col"] + p["m_prev"]
        p["m_t"] = jnp.maximum(p["log_inter"], jnp.max(p["log_d"], axis=1, keepdims=True))
    yield
    for p in probs:
        h = p["h"]
        p["k_t"] = jnp.transpose(p["k"])
        p["v_ext"] = jnp.concatenate([p["v_ref"][:, h * dv:(h + 1) * dv].astype(BF16), ones], axis=1)
    yield
    for p in probs:
        r = p["r"]
        b_row = p["b_row"]
        b_last = b_row[:, 0:1] if p["rev"] else b_row[:, CHUNK - 1:CHUNK]
        log_w = b_last - b_row + p["i_row"]
        m_loc = jnp.max(log_w, axis=1, keepdims=True)
        kw_t = (p["k_t"] * jnp.exp(log_w - m_loc)).astype(BF16)
        u = jnp.dot(kw_t, p["v_ext"], preferred_element_type=F32)
        m_new = jnp.maximum(b_last + p["m_prev"], m_loc)
        c_scr[r] = jnp.exp(b_last + p["m_prev"] - m_new) * p["c_prev"] + jnp.exp(m_loc - m_new) * u
        m_scr[r:r + 1, :] = jnp.broadcast_to(m_new, (1, LANE))
    yield
    for p in probs:
        s = (p["qk"] * jnp.exp(p["log_d"] - p["m_t"])).astype(BF16)
        p["sv"] = jnp.dot(s, p["v_ext"], preferred_element_type=F32)
    yield
    for p in probs:
        h = p["h"]
        tot = p["sv"] + p["qc"] * jnp.exp(p["log_inter"] - p["m_t"])
        den = tot[:, dv:dv + 1]
        p["o_ref"][:, h * dv:(h + 1) * dv] = tot[:, :dv] / jnp.maximum(jnp.abs(den), jnp.exp(-p["m_t"]))


N_RET_IN = 14
N_MLSTM_IN = 8


def _mixer_ab_kernel(dl_ref, gb_ref, *refs):
    ret_in, refs = refs[:N_RET_IN], refs[N_RET_IN:]
    ml_in, refs = refs[:N_MLSTM_IN], refs[N_MLSTM_IN:]
    ret_f_ref, ret_b_ref, ml_f_ref, ml_b_ref, s_scr, c_scr, m_scr = refs

    @pl.when(pl.program_id(1) == 0)
    def _():
        s_scr[...] = jnp.zeros_like(s_scr)
        c_scr[...] = jnp.zeros_like(c_scr)
        m_scr[...] = jnp.full_like(m_scr, NEG_BIG)

    pending = [_mlstm_body(gb_ref, *ml_in, ml_f_ref, ml_b_ref, c_scr, m_scr),
               _ret_body(dl_ref, *ret_in, ret_f_ref, ret_b_ref, s_scr)]
    done = object()
    while pending:
        pending = [g for g in pending if next(g, done) is not done]


def _mixer_ab(p, p_gates, decay_logit, gate_b, cos_t, sin_t, rows):
    n = p.shape[0]
    B = rows.B
    H = MLSTM_HEADS
    n_steps, seq_f, seq_b, unit = _chunk_maps(rows)
    rq, rv = RET_HEADS * RET_DK, RET_HEADS * RET_DV
    mq, mv = H * MLSTM_DK, H * MLSTM_DV
    dl = jnp.broadcast_to(decay_logit.astype(F32).reshape(2 * RET_HEADS, 1), (2 * RET_HEADS, LANE))
    gb = jnp.zeros((1, LANE), F32).at[0, :4 * H].set(gate_b.astype(F32).reshape(4 * H))

    def chunk(width, col0, seq):
        return pl.BlockSpec((CHUNK, width), lambda b, j: (unit(b, seq(j)), col0 // width))

    def ret_specs(seq):
        return [chunk(rq, AB_QR, seq), chunk(rq, AB_KR, seq), chunk(rv, AB_VR, seq),
                pl.BlockSpec((CHUNK, RET_DK), lambda b, j: (seq(j), 0)),
                pl.BlockSpec((CHUNK, RET_DK), lambda b, j: (seq(j), 0)),
                pl.BlockSpec((RET_DK, CHUNK), lambda b, j: (0, seq(j))),
                pl.BlockSpec((RET_DK, CHUNK), lambda b, j: (0, seq(j)))]

    def ml_specs(seq):
        return [chunk(mq, AB_QM, seq), chunk(mq, AB_KM, seq), chunk(mv, AB_VM, seq), chunk(LANE, 0, seq)]

    const = lambda r: pl.BlockSpec((r, LANE), lambda b, j: (0, 0))
    ret_args = (p, p, p, cos_t, sin_t, cos_t.T, sin_t.T)
    return pl.pallas_call(
        _mixer_ab_kernel,
        grid=(B, n_steps),
        in_specs=([const(2 * RET_HEADS), const(1)] + ret_specs(seq_f) + ret_specs(seq_b)
                  + ml_specs(seq_f) + ml_specs(seq_b)),
        out_specs=[chunk(rv, 0, seq_f), chunk(rv, 0, seq_b), chunk(mv, 0, seq_f), chunk(mv, 0, seq_b)],
        out_shape=[jax.ShapeDtypeStruct((n, rv), F32)] * 2 + [jax.ShapeDtypeStruct((n, mv), F32)] * 2,
        scratch_shapes=[pltpu.VMEM((2 * RET_HEADS, RET_DK, RET_DV), F32),
                        pltpu.VMEM((2 * H, MLSTM_DK, MLSTM_DV + LANE), F32),
                        pltpu.VMEM((2 * SUBLANE, LANE), F32)],
        compiler_params=_params(("arbitrary", "arbitrary")),
        name="mixer_ab",
    )(dl, gb, *ret_args, *ret_args, p, p, p, p_gates, p, p, p, p_gates)


def _head_ln(y, g, heads, width):
    outs = []
    for h in range(heads):
        yh = y[:, h * width:(h + 1) * width]
        mu = jnp.mean(yh, axis=-1, keepdims=True)
        yc = yh - mu
        var = jnp.mean(yc * yc, axis=-1, keepdims=True)
        outs.append(yc * lax.rsqrt(var + EPS))
    return jnp.concatenate(outs, axis=1) * g


def _merge_ab_kernel(x_ref, mod_ref, rf_ref, rb_ref, mf_ref, mb_ref, gr_ref, gm_ref, rg_ref, mg_ref,
                     w1_ref, w2_ref, o_ref):
    gr = gr_ref[...]
    ret_y = (gr * _sigmoid(gr)) * _head_ln(rf_ref[...] + rb_ref[...], rg_ref[...], RET_HEADS, RET_DV)
    ml_y = _sigmoid(gm_ref[...]) * _head_ln(mf_ref[...] + mb_ref[...], mg_ref[...], MLSTM_HEADS, MLSTM_DV)
    y = (jnp.dot(ret_y.astype(BF16), w1_ref[...], preferred_element_type=F32)
         + jnp.dot(ml_y.astype(BF16), w2_ref[...], preferred_element_type=F32))
    o_ref[...] = x_ref[...] + mod_ref[5:6, :] * y


def _merge_ab(x, mods, ret_f, ret_b, ml_f, ml_b, p, ret_g, ml_g, w_out, rows, *, tm):
    n, D = x.shape
    rw = RET_HEADS * RET_DV
    mw = MLSTM_HEADS * MLSTM_DV
    row = lambda i: (i, 0)
    return pl.pallas_call(
        _merge_ab_kernel,
        grid=(n // tm,),
        in_specs=[pl.BlockSpec((tm, D), row),
                  pl.BlockSpec((None, N_MOD, D), lambda i: (rows.mod_row(i, tm, True), 0, 0)),
                  pl.BlockSpec((tm, rw), row), pl.BlockSpec((tm, rw), row),
                  pl.BlockSpec((tm, mw), row), pl.BlockSpec((tm, mw), row),
                  pl.BlockSpec((tm, rw), lambda i: (i, AB_GR // rw)),
                  pl.BlockSpec((tm, mw), lambda i: (i, AB_GM // mw)),
                  pl.BlockSpec((1, rw), lambda i: (0, 0)),
                  pl.BlockSpec((1, mw), lambda i: (0, 0)),
                  pl.BlockSpec((rw, D), lambda i: (0, 0)),
                  pl.BlockSpec((mw, D), lambda i: (rw // mw, 0))],
        out_specs=pl.BlockSpec((tm, D), row),
        out_shape=jax.ShapeDtypeStruct((n, D), F32),
        compiler_params=_params(("arbitrary",)),
        name="merge_ab",
    )(x, mods, ret_f, ret_b, ml_f, ml_b, p, p, ret_g.reshape(1, rw), ml_g.reshape(1, mw), w_out, w_out)


def _lru_coef_kernel(x_ref, xp_ref, xn_ref, cw_ref, cb_ref, wa_ref, wx_ref, ba_ref, bx_ref, lam_ref,
                     a_ref, b_ref, *, tm, ctx_tiles, ctx_seg, lat_seg):
    i = pl.program_id(0)
    seg_pos = jnp.where(i < ctx_tiles, i % ctx_seg, (i - ctx_tiles) % lat_seg)
    seg_len = jnp.where(i < ctx_tiles, ctx_seg, lat_seg)
    keep_prev = (seg_pos != 0).astype(F32)
    keep_next = (seg_pos != seg_len - 1).astype(F32)
    xe = jnp.concatenate([xp_ref[...] * keep_prev, x_ref[...], xn_ref[...] * keep_next], axis=0)
    ne = tm + 2 * SUBLANE
    xc = cb_ref[...] + cw_ref[2:3, :] * x_ref[...]
    for tap, off in ((0, -2), (1, -1), (3, 1)):
        shifted = pltpu.roll(xe, (-off) % ne, axis=0)[SUBLANE:SUBLANE + tm, :]
        xc = xc + cw_ref[tap:tap + 1, :] * shifted
    for d in range(2):
        lam = lam_ref[d:d + 1, :]
        sp = jnp.maximum(-lam, 0.0) + jnp.log(1.0 + jnp.exp(-jnp.abs(lam)))
        half_rate = (-0.5 * LRU_C * LOG2E) * sp
        for g in range(LRU_BLOCKS):
            sl = slice(g * LRU_BLOCK, (g + 1) * LRU_BLOCK)
            xg = xc[:, sl]
            xg16 = xg.astype(BF16)
            t_r = jnp.tanh(jnp.dot(xg16, wa_ref[d, g], preferred_element_type=F32) + ba_ref[d:d + 1, sl])
            t_i = jnp.tanh(jnp.dot(xg16, wx_ref[d, g], preferred_element_type=F32) + bx_ref[d:d + 1, sl])
            a = jnp.exp2(t_r * half_rate[:, sl] + half_rate[:, sl])
            half_x = 0.5 * xg
            a_ref[d, :, sl] = a
            b_ref[d, :, sl] = jnp.sqrt(1.0 - a * a) * (t_i * half_x + half_x)


def _lru_coef(p, conv_w, conv_b, wa, wx, ba, bx, lam, rows, *, tm):
    n = p.shape[0]
    W = LRU_WIDTH
    tpb = tm // SUBLANE
    n8 = n // SUBLANE
    kern = functools.partial(_lru_coef_kernel, tm=tm, ctx_tiles=rows.n_ctx // tm,
                             ctx_seg=rows.Lc // tm, lat_seg=rows.S // tm)
    full = lambda *s: pl.BlockSpec(s, lambda i: (0,) * len(s))
    return pl.pallas_call(
        kern,
        grid=(n // tm,),
        in_specs=[pl.BlockSpec((tm, W), lambda i: (i, CD_XB // W)),
                  pl.BlockSpec((SUBLANE, W), lambda i: (jnp.maximum(i * tpb - 1, 0), CD_XB // W)),
                  pl.BlockSpec((SUBLANE, W), lambda i: (jnp.minimum((i + 1) * tpb, n8 - 1), CD_XB // W)),
                  full(4, W), full(1, W), full(2, LRU_BLOCKS, LRU_BLOCK, LRU_BLOCK),
                  full(2, LRU_BLOCKS, LRU_BLOCK, LRU_BLOCK), full(2, W), full(2, W), full(2, W)],
        out_specs=[pl.BlockSpec((2, tm, W), lambda i: (0, i, 0))] * 2,
        out_shape=[jax.ShapeDtypeStruct((2, n, W), F32)] * 2,
        compiler_params=_params(("arbitrary",)),
        name="lru_coef",
    )(p, p, p, conv_w, conv_b.reshape(1, W), wa, wx, ba, bx, lam)


def _lru_scan_kernel(af_ref, bf_ref, ab_ref, bb_ref, of_ref, ob_ref, h_scr, *, tb, lw):
    @pl.when(pl.program_id(1) == 0)
    def _():
        h_scr[...] = jnp.zeros_like(h_scr)

    row = lax.broadcasted_iota(jnp.int32, (SUBLANE, lw), 0)
    ng = tb // SUBLANE

    def scan8(a, b, rev):
        d = 1
        while d < SUBLANE:
            if rev:
                keep = row < SUBLANE - d
                sh = SUBLANE - d
            else:
                keep = row >= d
                sh = d
            b = a * jnp.where(keep, pltpu.roll(b, sh, axis=0), 0.0) + b
            a = a * jnp.where(keep, pltpu.roll(a, sh, axis=0), 1.0)
            d *= 2
        return a, b

    for c in range(LRU_WIDTH // lw):
        cs = slice(c * lw, (c + 1) * lw)

        def body(g, carry):
            hf, hb = carry
            rf = pl.multiple_of(g * SUBLANE, SUBLANE)
            a, b = scan8(af_ref[pl.ds(rf, SUBLANE), cs], bf_ref[pl.ds(rf, SUBLANE), cs], False)
            out = a * hf + b
            of_ref[pl.ds(rf, SUBLANE), cs] = out
            hf = jnp.broadcast_to(out[SUBLANE - 1:SUBLANE, :], (SUBLANE, lw))
            rb = pl.multiple_of((ng - 1 - g) * SUBLANE, SUBLANE)
            a, b = scan8(ab_ref[pl.ds(rb, SUBLANE), cs], bb_ref[pl.ds(rb, SUBLANE), cs], True)
            out = a * hb + b
            ob_ref[pl.ds(rb, SUBLANE), cs] = out
            hb = jnp.broadcast_to(out[0:1, :], (SUBLANE, lw))
            return hf, hb

        hf, hb = lax.fori_loop(0, ng, body, (h_scr[0, :, cs], h_scr[1, :, cs]))
        h_scr[0, :, cs] = hf
        h_scr[1, :, cs] = hb


def _lru_scan(a, b, rows):
    n = a.shape[1]
    W = LRU_WIDTH
    B = rows.B
    tb = rows.Lc
    nlb = rows.S // tb

    def blk_f(b_, j):
        return jnp.where(j == 0, b_, B + b_ * nlb + (j - 1))

    def blk_b(b_, j):
        return jnp.where(j == 0, b_, B + b_ * nlb + (nlb - j))

    def spec(d, blk):
        return pl.BlockSpec((None, tb, W), lambda b_, j: (d, blk(b_, j), 0))

    kern = functools.partial(_lru_scan_kernel, tb=tb, lw=512)
    return pl.pallas_call(
        kern,
        grid=(B, 1 + nlb),
        in_specs=[spec(0, blk_f), spec(0, blk_f), spec(1, blk_b), spec(1, blk_b)],
        out_specs=[pl.BlockSpec((tb, W), lambda b_, j: (blk_f(b_, j), 0)),
                   pl.BlockSpec((tb, W), lambda b_, j: (blk_b(b_, j), 0))],
        out_shape=[jax.ShapeDtypeStruct((n, W), F32)] * 2,
        scratch_shapes=[pltpu.VMEM((2, SUBLANE, W), F32)],
        compiler_params=_params(("arbitrary", "arbitrary")),
        name="lru_scan",
    )(a, b, a, b)


def _mla_qkv_kernel(cq_ref, ckv_ref, kr_ref, qg_ref, kvg_ref, wq_ref, wk_ref, wv_ref, qkg_ref, cos_ref, sin_ref,
                    q_ref, k_ref, v_ref):
    cqn = _rms(cq_ref[...], qg_ref[...]).astype(BF16)
    ckvn = _rms(ckv_ref[...], kvg_ref[...]).astype(BF16)
    q_all = jnp.dot(cqn, wq_ref[...], preferred_element_type=F32)
    kn_all = jnp.dot(ckvn, wk_ref[...], preferred_element_type=F32)
    v_all = jnp.dot(ckvn, wv_ref[...], preferred_element_type=F32).astype(BF16)
    ones = jnp.ones((v_all.shape[0], LANE), BF16)
    for h in range(MLA_HEADS):
        v_ref[:, h * (MLA_V + LANE):(h + 1) * (MLA_V + LANE)] = jnp.concatenate(
            [v_all[:, h * MLA_V:(h + 1) * MLA_V], ones], axis=1)
    kr = kr_ref[...]
    cos = cos_ref[...]
    sin = sin_ref[...]
    lane = lax.broadcasted_iota(jnp.int32, cos.shape, 1)
    first_half = (lane % (MLA_ROPE // 2)) < (MLA_ROPE // 4)
    hi_w = MLA_PAD - MLA_NOPE

    def rope(hi):
        rot = jnp.where(first_half, pltpu.roll(hi, hi_w - MLA_ROPE // 4, axis=1),
                        pltpu.roll(hi, MLA_ROPE // 4, axis=1))
        return hi * cos + rot * sin

    def inv_rms(lo, hi):
        return lax.rsqrt(jnp.sum(lo * lo + hi * hi, axis=-1, keepdims=True) * (1.0 / MLA_QK) + EPS)

    g_q, g_k = qkg_ref[0:1, :], qkg_ref[1:2, :]
    kr_rot = rope(kr * g_k[:, MLA_NOPE:])
    for h in range(MLA_HEADS):
        lo_sl = slice(h * MLA_PAD, h * MLA_PAD + MLA_NOPE)
        hi_sl = slice(h * MLA_PAD + MLA_NOPE, (h + 1) * MLA_PAD)
        q_lo, q_hi = q_all[:, lo_sl], q_all[:, hi_sl]
        r = inv_rms(q_lo, q_hi) * (MLA_QK ** -0.5 * LOG2E)
        q_ref[:, lo_sl] = (q_lo * r * g_q[:, :MLA_NOPE]).astype(BF16)
        q_ref[:, hi_sl] = rope(q_hi * r * g_q[:, MLA_NOPE:]).astype(BF16)
        k_lo = kn_all[:, h * MLA_NOPE:(h + 1) * MLA_NOPE]
        r = inv_rms(k_lo, kr)
        k_ref[:, lo_sl] = (k_lo * r * g_k[:, :MLA_NOPE]).astype(BF16)
        k_ref[:, hi_sl] = (kr_rot * r).astype(BF16)


def _mla_qkv(p, q_norm_g, kv_norm_g, wq, wk, wv, qk_g, cos_t, sin_t, rows, *, tm):
    B, Lc, S = rows.B, rows.Lc, rows.S
    n = p.shape[0]
    nct = rows.n_ctx // tm
    cpb = Lc // tm
    lpb = S // tm

    def seq_blk(i):
        il = jnp.maximum(i - nct, 0)
        return (jnp.where(i < nct, i // cpb, il // lpb), jnp.where(i < nct, lpb + i % cpb, il % lpb), 0)

    def pos_blk(i):
        return (jnp.where(i < nct, i % cpb, cpb + jnp.maximum(i - nct, 0) % lpb), 0)

    full = lambda *s: pl.BlockSpec(s, lambda i: (0,) * len(s))
    qkw = MLA_HEADS * MLA_PAD
    vw = MLA_HEADS * MLA_V
    return pl.pallas_call(
        _mla_qkv_kernel,
        grid=(n // tm,),
        in_specs=[pl.BlockSpec((tm, MLA_Q_RANK), lambda i: (i, CD_CQ // MLA_Q_RANK)),
                  pl.BlockSpec((tm, MLA_KV_RANK), lambda i: (i, CD_CKV // MLA_KV_RANK)),
                  pl.BlockSpec((tm, LANE), lambda i: (i, CD_KR // LANE)),
                  full(1, MLA_Q_RANK), full(1, MLA_KV_RANK), full(MLA_Q_RANK, qkw),
                  full(MLA_KV_RANK, MLA_HEADS * MLA_NOPE), full(MLA_KV_RANK, vw), full(2, MLA_PAD),
                  pl.BlockSpec((tm, MLA_PAD - MLA_NOPE), pos_blk),
                  pl.BlockSpec((tm, MLA_PAD - MLA_NOPE), pos_blk)],
        out_specs=[pl.BlockSpec((None, tm, qkw), seq_blk), pl.BlockSpec((None, tm, qkw), seq_blk),
                   pl.BlockSpec((None, tm, vw + MLA_HEADS * LANE), seq_blk)],
        out_shape=[jax.ShapeDtypeStruct((B, Lc + S, qkw), BF16), jax.ShapeDtypeStruct((B, Lc + S, qkw), BF16),
                   jax.ShapeDtypeStruct((B, Lc + S, vw + MLA_HEADS * LANE), BF16)],
        compiler_params=_params(("arbitrary",)),
        name="mla_qkv",
    )(p, p, p, q_norm_g.reshape(1, -1), kv_norm_g.reshape(1, -1), wq, wk, wv, qk_g, cos_t, sin_t)


def _attn_kernel(q_ref, k_ref, v_ref, o_ref, *, sub):
    nt = (((1,), (1,)), ((), ()))
    groups = list(range(0, q_ref.shape[0], sub))

    def scores(r):
        return lax.dot_general(q_ref[r:r + sub, :], k_ref[...], nt, preferred_element_type=F32)

    s_next = scores(groups[0])
    for g, r in enumerate(groups):
        s = s_next
        if g + 1 < len(groups):
            s_next = scores(groups[g + 1])
        e = jnp.exp2(s - jnp.max(s, axis=-1, keepdims=True)).astype(BF16)
        ov = jnp.dot(e, v_ref[...], preferred_element_type=F32)
        o_ref[r:r + sub, :] = (ov[:, :MLA_V] / ov[:, MLA_V:MLA_V + 1]).astype(BF16)


def _attention(q, k, v, rows, *, tq):
    B, Lc, S = rows.B, rows.Lc, rows.S
    nq = S // tq
    return pl.pallas_call(
        functools.partial(_attn_kernel, sub=min(tq, 256)),
        grid=(B, MLA_HEADS, nq),
        in_specs=[pl.BlockSpec((None, tq, MLA_PAD), lambda b, h, i: (b, i, h)),
                  pl.BlockSpec((None, Lc + S, MLA_PAD), lambda b, h, i: (b, 0, h)),
                  pl.BlockSpec((None, Lc + S, MLA_V + LANE), lambda b, h, i: (b, 0, h))],
        out_specs=pl.BlockSpec((tq, MLA_V), lambda b, h, i: (b * nq + i, h)),
        out_shape=jax.ShapeDtypeStruct((B * S, MLA_HEADS * MLA_V), BF16),
        compiler_params=_params(("arbitrary", "arbitrary", "arbitrary")),
        name="attention",
    )(q, k, v)


def _gelu_tanh(x):
    return 0.5 * x * (1.0 + jnp.tanh(math.sqrt(2.0 / math.pi) * (x + 0.044715 * (x * x * x))))


def _merge_cd_kernel(x_ref, mod_ref, yb_ref, hf_ref, hb_ref, att_ref, w1_ref, w2_ref, o_ref):
    y1 = _gelu_tanh(yb_ref[...]) * (hf_ref[...] + hb_ref[...])
    y = (jnp.dot(y1.astype(BF16), w1_ref[...], preferred_element_type=F32)
         + jnp.dot(att_ref[...], w2_ref[...], preferred_element_type=F32))
    o_ref[...] = x_ref[...] + mod_ref[5:6, :] * y


def _merge_cd(x, mods, p, h_f, h_b, att, w_out, rows, *, tm):
    D = x.shape[1]
    W = LRU_WIDTH
    aw = MLA_HEADS * MLA_V
    off = rows.n_ctx // tm
    lat = lambda i: (i + off, 0)
    return pl.pallas_call(
        _merge_cd_kernel,
        grid=(rows.n_lat // tm,),
        in_specs=[pl.BlockSpec((tm, D), lat),
                  pl.BlockSpec((None, N_MOD, D), lambda i: (rows.mod_row(i, tm, False), 0, 0)),
                  pl.BlockSpec((tm, W), lambda i: (i + off, CD_YB // W)),
                  pl.BlockSpec((tm, W), lat), pl.BlockSpec((tm, W), lat),
                  pl.BlockSpec((tm, aw), lambda i: (i, 0)),
                  pl.BlockSpec((W, D), lambda i: (0, 0)),
                  pl.BlockSpec((aw, D), lambda i: (W // aw, 0))],
        out_specs=pl.BlockSpec((tm, D), lambda i: (i, 0)),
        out_shape=jax.ShapeDtypeStruct((rows.n_lat, D), F32),
        compiler_params=_params(("arbitrary",)),
        name="merge_cd",
    )(x, mods, p, h_f, h_b, att, w_out, w_out)


def _ret_tables(rows):
    half = RET_DK // 2
    freqs = ROPE_BASE ** (-jnp.arange(half, dtype=F32) / half)
    ang = jnp.arange(rows.S, dtype=F32)[:, None] * freqs
    cos = jnp.concatenate([jnp.cos(ang), jnp.cos(ang)], axis=-1)
    sin = jnp.concatenate([-jnp.sin(ang), jnp.sin(ang)], axis=-1)
    cos = jnp.concatenate([jnp.ones((rows.Lc, RET_DK), F32), cos], axis=0)
    sin = jnp.concatenate([jnp.zeros((rows.Lc, RET_DK), F32), sin], axis=0)
    return cos, sin


def _mla_tables(rows):
    S = rows.S
    quarter = MLA_ROPE // 4
    freqs = ROPE_BASE ** (-jnp.arange(quarter, dtype=F32) / quarter)
    t = jnp.arange(S)
    row = (t // GRID_W).astype(F32)
    col = (t % GRID_W).astype(F32)

    def part(pos):
        ang = pos[:, None] * freqs
        return (jnp.concatenate([jnp.cos(ang), jnp.cos(ang)], axis=-1),
                jnp.concatenate([-jnp.sin(ang), jnp.sin(ang)], axis=-1))

    rc, rs = part(row)
    cc, cs = part(col)
    tail = MLA_PAD - MLA_QK
    cos = jnp.concatenate([rc, cc, jnp.ones((S, tail), F32)], axis=-1)
    sin = jnp.concatenate([rs, cs, jnp.zeros((S, tail), F32)], axis=-1)
    cos = jnp.concatenate([jnp.ones((rows.Lc, MLA_ROPE + tail), F32), cos], axis=0)
    sin = jnp.concatenate([jnp.zeros((rows.Lc, MLA_ROPE + tail), F32), sin], axis=0)
    return cos, sin


def _pad_cols(w, n):
    return jnp.pad(w, ((0, 0), (0, n - w.shape[1])))


def _head_pad(w, heads, width, padded):
    K = w.shape[0]
    return jnp.pad(w.reshape(K, heads, width), ((0, 0), (0, 0), (0, padded - width))).reshape(K, heads * padded)


def kernel(x, c, ctx, c_ctx, ada_w, ada_b, norm_g, ffn_wg, ffn_wu, ffn_wd, ab_w_in, ab_w_out, ret_decay_logit, ret_gn_g, mlstm_gate_b, mlstm_gn_g, cd_w_in, cd_w_out, lru_conv_w, lru_conv_b, lru_wa, lru_ba, lru_wx, lru_bx, lru_lambda, mla_q_norm_g, mla_kv_norm_g, mla_w_uq, mla_w_uk, mla_w_uv, mla_qk_norm_g):
    B, S, D = x.shape
    Lc = ctx.shape[1]
    depth = ada_w.shape[0]
    F = ffn_wg.shape[-1]
    rows = Rows(B, Lc, S)
    assert B < SUBLANE and Lc % CHUNK == 0 and S % Lc == 0 and S % GRID_W == 0

    tm = min(1024, rows.n_ctx)
    tf = 512
    all_mod = lambda i: rows.mod_row(i, tm, True)
    lat_mod = lambda i: rows.mod_row(i, tm, False)

    cond = jnp.zeros((SUBLANE, D), F32).at[:B].set(c.astype(F32)).at[B].set(c_ctx.astype(F32))
    mods = _adaln(cond, ada_w, ada_b)

    ffn_stacks = (ffn_wg, ffn_wu, ffn_wd)
    w_cur = tuple(w[0, 0].astype(BF16) for w in ffn_stacks)

    def ffn_casts(l, h):
        return [(w, (l, h)) for w in ffn_stacks] if l < depth else []

    for l in range(depth):
        last = l == depth - 1
        j = l // 2
        m_l = mods[l]
        ffn1 = functools.partial(_ffn, mods=m_l, g=norm_g[l, 0], wg=w_cur[0], wu=w_cur[1], wd=w_cur[2],
                                 mod_base=0, tm=tm, tf=tf)
        casts = ffn_casts(l, 1)
        if l == 0:
            xa, w_next = ffn1(x.reshape(B * S, D).astype(F32), mod_of_tile=lat_mod, out_rows=rows.n_all,
                              out_tile0=rows.n_ctx // tm, casts=casts)
            xa = ffn1(ctx.reshape(B * Lc, D).astype(F32), mod_of_tile=lambda i: B, out_rows=rows.n_all, into=xa)
        else:
            xa, w_next = ffn1(xa, mod_of_tile=all_mod, casts=casts)
        w_cur = w_next

        if l % 2 == 0:
            w_in, w_out = ab_w_in[j].astype(BF16), ab_w_out[j].astype(BF16)
            p, p_gates = _proj(xa, m_l, norm_g[l, 1], w_in, rows, tm=tm, tn=1024, n_cols=AB_GATES,
                               w_tail=_pad_cols(w_in[:, AB_GATES:], LANE))
            cos_t, sin_t = _ret_tables(rows)
            ret_f, ret_b, ml_f, ml_b = _mixer_ab(p, p_gates, ret_decay_logit[j], mlstm_gate_b[j], cos_t, sin_t,
                                                 rows)
            xa = _merge_ab(xa, m_l, ret_f, ret_b, ml_f, ml_b, p, ret_gn_g[j], mlstm_gn_g[j], w_out, rows,
                           tm=min(256, tm))
            if last:
                xa = xa[rows.n_ctx:]
        else:
            assert last, "context outputs of the recurrent/attention mixer are not produced"
            w_in, w_out, w_uq, w_uk, w_uv = (w[j].astype(BF16) for w in
                                             (cd_w_in, cd_w_out, mla_w_uq, mla_w_uk, mla_w_uv))
            p, = _proj(xa, m_l, norm_g[l, 1], _pad_cols(w_in, CD_PAD), rows, tm=tm, tn=CD_PAD // 3)
            a_c, b_c = _lru_coef(p, lru_conv_w[j], lru_conv_b[j], (0.5 * lru_wa[j]).astype(BF16),
                                 (0.5 * lru_wx[j]).astype(BF16), 0.5 * lru_ba[j], 0.5 * lru_bx[j], lru_lambda[j],
                                 rows, tm=min(256, Lc))
            h_f, h_b = _lru_scan(a_c, b_c, rows)
            cos_t, sin_t = _mla_tables(rows)
            qk_g = jnp.pad(mla_qk_norm_g[j].astype(F32), ((0, 0), (0, MLA_PAD - MLA_QK)))
            q, k, v = _mla_qkv(p, mla_q_norm_g[j], mla_kv_norm_g[j], _head_pad(w_uq, MLA_HEADS, MLA_QK, MLA_PAD),
                               w_uk, w_uv, qk_g, cos_t, sin_t, rows, tm=min(256, Lc))
            att = _attention(q, k, v, rows, tq=min(2048, S))
            xa = _merge_cd(xa, m_l, p, h_f, h_b, att, w_out, rows, tm=min(256, tm))

        casts = ffn_casts(l + 1, 0)
        res = _ffn(xa, m_l, norm_g[l, 2], *w_cur, mod_base=6, mod_of_tile=lat_mod if last else all_mod,
                   tm=tm, tf=tf, casts=casts)
        xa, w_cur = res if casts else (res, None)

    if xa.shape[0] != rows.n_lat:
        xa = xa[rows.n_ctx:]
    return xa.reshape(B, S, D).astype(x.dtype)
```

```python
import functools
import math

import jax
import jax.numpy as jnp
import numpy as np
from jax import lax
from jax.experimental import pallas as pl
from jax.experimental.pallas import tpu as pltpu

F32 = jnp.float32
BF16 = jnp.bfloat16

N_MOD = 9
EPS = 1e-6
CHUNK = 128
GRID_W = 64
ROPE_BASE = 10000.0
NEG_BIG = -1e30

RET_HEADS = 4
RET_DK = 128
RET_DV = 256
MLSTM_HEADS = 4
MLSTM_DK = 128
MLSTM_DV = 256
LRU_WIDTH = 1024
LRU_BLOCKS = 8
LRU_BLOCK = LRU_WIDTH // LRU_BLOCKS
LRU_C = 8.0
MLA_HEADS = 8
MLA_Q_RANK = 512
MLA_KV_RANK = 256
MLA_NOPE = 128
MLA_ROPE = 64
MLA_V = 128
MLA_QK = MLA_NOPE + MLA_ROPE
LOG2E = math.log2(math.e)
MLA_PAD = 256

AB_SPLITS = (RET_HEADS * RET_DK, RET_HEADS * RET_DK, RET_HEADS * RET_DV, RET_HEADS * RET_DV,
             MLSTM_HEADS * MLSTM_DK, MLSTM_HEADS * MLSTM_DK, MLSTM_HEADS * MLSTM_DV, MLSTM_HEADS * MLSTM_DV,
             4 * MLSTM_HEADS)
AB_IN = sum(AB_SPLITS)
CD_SPLITS = (LRU_WIDTH, LRU_WIDTH, MLA_Q_RANK, MLA_KV_RANK, MLA_ROPE)
CD_IN = sum(CD_SPLITS)

LANE = 128
SUBLANE = 8
BF16_SUBLANES = 16
VMEM_LIMIT = 56 * 1024 * 1024
FFN_VMEM_LIMIT = 60 * 1024 * 1024

AB_QR, AB_KR, AB_VR, AB_GR, AB_QM, AB_KM, AB_VM, AB_GM, AB_GATES = np.cumsum((0,) + AB_SPLITS[:-1]).tolist()
CD_YB, CD_XB, CD_CQ, CD_CKV, CD_KR = np.cumsum((0,) + CD_SPLITS[:-1]).tolist()
CD_PAD = 3072


def _params(sem, vmem=VMEM_LIMIT):
    return pltpu.CompilerParams(dimension_semantics=sem, vmem_limit_bytes=vmem)


def _round_up(n, m):
    return (n + m - 1) // m * m


def _sigmoid(x):
    return 0.5 * jnp.tanh(0.5 * x) + 0.5


def _log_sigmoid(x):
    return jnp.minimum(x, 0.0) - jnp.log(1.0 + jnp.exp(-jnp.abs(x)))


def _rms(x, g):
    return x * lax.rsqrt(jnp.mean(x * x, axis=-1, keepdims=True) + EPS) * g


def _adaln_kernel(c_ref, w_ref, b_ref, o_ref):
    c = c_ref[...]
    s = (c * _sigmoid(c)).astype(BF16)
    o_ref[...] = jnp.dot(s, w_ref[...].astype(BF16), preferred_element_type=F32) + b_ref[...]


def _adaln(cond, ada_w, ada_b):
    L, D, N = ada_w.shape
    tn = 1024 if N % 1024 == 0 else N
    out = pl.pallas_call(
        _adaln_kernel,
        grid=(L, N // tn),
        in_specs=[pl.BlockSpec((SUBLANE, D), lambda l, n: (0, 0)),
                  pl.BlockSpec((None, D, tn), lambda l, n: (l, 0, n)),
                  pl.BlockSpec((None, 1, tn), lambda l, n: (l, 0, n))],
        out_specs=pl.BlockSpec((None, SUBLANE, tn), lambda l, n: (l, 0, n)),
        out_shape=jax.ShapeDtypeStruct((L, SUBLANE, N), F32),
        compiler_params=_params(("arbitrary", "arbitrary")),
        name="adaln",
    )(cond, ada_w, ada_b.reshape(L, 1, N))
    return out.reshape(L, SUBLANE, N_MOD, D)


class Rows:
    def __init__(self, B, Lc, S):
        self.B, self.Lc, self.S = B, Lc, S
        self.n_ctx = B * Lc
        self.n_lat = B * S
        self.n_all = self.n_ctx + self.n_lat

    def mod_row(self, tile, tm, with_ctx):
        if not with_ctx:
            return (tile * tm) // self.S
        nct = self.n_ctx // tm
        return jnp.where(tile < nct, self.B, (jnp.maximum(tile - nct, 0) * tm) // self.S)


def _ffn_kernel(*refs, mod_base, sub, last_cols, has_into, n_cast):
    x_ref, mod_ref, g_ref, wg_ref, wu_ref, wd_ref = refs[:6]
    cast_src = refs[6:6 + n_cast]
    o_ref = refs[6 + n_cast + has_into]
    cast_dst = refs[7 + n_cast + has_into:7 + 2 * n_cast + has_into]
    h_scr = refs[-1]
    f = pl.program_id(1)
    nf = pl.num_programs(1)
    tm = x_ref.shape[0]
    tf = wg_ref.shape[1]

    def step(cols, first, final):
        for src, dst in zip(cast_src, cast_dst):
            dst[...] = src[...].astype(BF16)
        if first:
            shift = mod_ref[mod_base:mod_base + 1, :]
            scale = 1.0 + mod_ref[mod_base + 1:mod_base + 2, :]
        if final:
            gate = 0.5 * mod_ref[mod_base + 2:mod_base + 3, :]
        for r in range(0, tm, sub):
            rs = slice(r, r + sub)
            if first:
                h = (_rms(x_ref[rs, :], g_ref[...]) * scale + shift).astype(BF16)
                h_scr[rs, :] = h
            else:
                h = h_scr[rs, :]
            a = jnp.dot(h, wg_ref[:, :cols], preferred_element_type=F32)
            u = jnp.dot(h, wu_ref[:, :cols], preferred_element_type=F32)
            act = (a * _sigmoid(a) * u).astype(BF16)
            part = jnp.dot(act, wd_ref[:cols, :], preferred_element_type=F32)
            if first:
                o_ref[rs, :] = part
            elif final:
                o_ref[rs, :] = x_ref[rs, :] + gate * (o_ref[rs, :] + part)
            else:
                o_ref[rs, :] += part

    pl.when(f == 0)(functools.partial(step, tf, True, False))
    pl.when(jnp.logical_and(f > 0, f < nf - 1))(functools.partial(step, tf, False, False))
    pl.when(f == nf - 1)(functools.partial(step, last_cols, False, True))


def _ffn(x, mods, g, wg, wu, wd, *, mod_base, mod_of_tile, tm, tf, out_rows=None, out_tile0=0, into=None,
         casts=()):
    n, D = x.shape
    F = wg.shape[-1]
    nf = pl.cdiv(F, tf)
    ni = n // tm
    assert nf >= 2 and n % tm == 0
    out_rows = n if out_rows is None else out_rows
    in_specs = [pl.BlockSpec((tm, D), lambda i, f: (i, 0)),
                pl.BlockSpec((None, N_MOD, D), lambda i, f: (mod_of_tile(i), 0, 0)),
                pl.BlockSpec((1, D), lambda i, f: (0, 0)),
                pl.BlockSpec((D, tf), lambda i, f: (0, f)),
                pl.BlockSpec((D, tf), lambda i, f: (0, f)),
                pl.BlockSpec((tf, D), lambda i, f: (f, 0))]
    args = [x, mods, g.reshape(1, D), wg, wu, wd]
    out_specs = [pl.BlockSpec((tm, D), lambda i, f: (out_tile0 + i, 0))]
    out_shape = [jax.ShapeDtypeStruct((out_rows, D), F32)]
    n_cast = len(casts)
    for w, lead in casts:
        rows_w, cols_w = w.shape[len(lead):]
        slab_rows = next(m for m in range(BF16_SUBLANES * pl.cdiv(rows_w, BF16_SUBLANES * ni * nf),
                                          rows_w + 1, BF16_SUBLANES) if rows_w % m == 0)
        nblk = rows_w // slab_rows
        slab = lambda i, f, nblk=nblk: jnp.minimum(i * nf + f, nblk - 1)
        in_specs.append(pl.BlockSpec((None,) * len(lead) + (slab_rows, cols_w),
                                     lambda i, f, slab=slab, lead=lead: lead + (slab(i, f), 0)))
        out_specs.append(pl.BlockSpec((slab_rows, cols_w), lambda i, f, slab=slab: (slab(i, f), 0)))
        out_shape.append(jax.ShapeDtypeStruct((rows_w, cols_w), BF16))
        args.append(w)
    aliases = {}
    if into is not None:
        assert into.shape == (out_rows, D)
        aliases = {len(args): 0}
        in_specs.append(pl.BlockSpec(memory_space=pl.ANY))
        args.append(into)
    kern = functools.partial(_ffn_kernel, mod_base=mod_base, sub=min(tm, 512), last_cols=F - (nf - 1) * tf,
                             has_into=int(into is not None), n_cast=n_cast)
    res = pl.pallas_call(
        kern,
        grid=(ni, nf),
        in_specs=in_specs,
        out_specs=out_specs,
        out_shape=out_shape,
        scratch_shapes=[pltpu.VMEM((tm, D), BF16)],
        input_output_aliases=aliases,
        compiler_params=_params(("arbitrary", "arbitrary"), FFN_VMEM_LIMIT),
        name="ffn",
    )(*args)
    return (res[0], tuple(res[1:])) if n_cast else res[0]


def _proj_kernel(x_ref, mod_ref, g_ref, w_ref, *rest, sub):
    wt_ref, o_ref, ot_ref, h_scr = rest if len(rest) == 4 else (None, rest[0], None, rest[1])
    tm = x_ref.shape[0]

    @pl.when(pl.program_id(1) == 0)
    def _():
        scale = 1.0 + mod_ref[4:5, :]
        for r in range(0, tm, sub):
            h = (_rms(x_ref[r:r + sub, :], g_ref[...]) * scale + mod_ref[3:4, :]).astype(BF16)
            h_scr[r:r + sub, :] = h
            o_ref[r:r + sub, :] = jnp.dot(h, w_ref[...], preferred_element_type=F32)
            if wt_ref is not None:
                ot_ref[r:r + sub, :] = jnp.dot(h, wt_ref[...], preferred_element_type=F32)

    @pl.when(pl.program_id(1) != 0)
    def _():
        for r in range(0, tm, sub):
            o_ref[r:r + sub, :] = jnp.dot(h_scr[r:r + sub, :], w_ref[...], preferred_element_type=F32)


def _proj(x, mods, g, w, rows, *, tm, tn, n_cols=None, w_tail=None):
    n, D = x.shape
    N = w.shape[1] if n_cols is None else n_cols
    assert N % tn == 0
    in_specs = [pl.BlockSpec((tm, D), lambda i, j: (i, 0)),
                pl.BlockSpec((None, N_MOD, D), lambda i, j: (rows.mod_row(i, tm, True), 0, 0)),
                pl.BlockSpec((1, D), lambda i, j: (0, 0)),
                pl.BlockSpec((D, tn), lambda i, j: (0, j))]
    out_specs = [pl.BlockSpec((tm, tn), lambda i, j: (i, j))]
    out_shape = [jax.ShapeDtypeStruct((n, N), F32)]
    args = [x, mods, g.reshape(1, D), w]
    if w_tail is not None:
        in_specs.append(pl.BlockSpec((D, LANE), lambda i, j: (0, 0)))
        out_specs.append(pl.BlockSpec((tm, LANE), lambda i, j: (i, 0)))
        out_shape.append(jax.ShapeDtypeStruct((n, LANE), F32))
        args.append(w_tail)
    return pl.pallas_call(
        functools.partial(_proj_kernel, sub=min(tm, 512)),
        grid=(n // tm, N // tn),
        in_specs=in_specs,
        out_specs=out_specs,
        out_shape=out_shape,
        scratch_shapes=[pltpu.VMEM((tm, D), BF16)],
        compiler_params=_params(("arbitrary", "arbitrary")),
        name="proj",
    )(*args)


def _chunk_maps(rows):
    B = rows.B
    nc = rows.Lc // CHUNK
    nl = rows.S // CHUNK

    def seq_f(j):
        return j

    def seq_b(j):
        return jnp.where(j < nc, nc - 1 - j, nc + nl - 1 - (j - nc))

    def unit(b, s):
        return jnp.where(s < nc, b * nc + s, B * nc + b * nl + (s - nc))

    return nc + nl, seq_f, seq_b, unit


def _rope128(x, cos, sin):
    return x * cos + pltpu.roll(x, 64, axis=1) * sin


def _tri_masks():
    r = lax.broadcasted_iota(jnp.int32, (CHUNK, CHUNK), 0)
    c = lax.broadcasted_iota(jnp.int32, (CHUNK, CHUNK), 1)
    return r - c


def _ret_body(dl_ref, qf_ref, kf_ref, vf_ref, cf_ref, sf_ref, ctf_ref, stf_ref,
              qb_ref, kb_ref, vb_ref, cb_ref, sb_ref, ctb_ref, stb_ref, of_ref, ob_ref, s_scr):
    lg_all = _log_sigmoid(dl_ref[...])
    rel = _tri_masks().astype(F32)
    pos_c = lax.broadcasted_iota(jnp.int32, (CHUNK, 1), 0).astype(F32)
    pos_r = lax.broadcasted_iota(jnp.int32, (1, CHUNK), 1).astype(F32)
    dirs = ((qf_ref, kf_ref, vf_ref, cf_ref, sf_ref, ctf_ref, stf_ref, of_ref, False),
            (qb_ref, kb_ref, vb_ref, cb_ref, sb_ref, ctb_ref, stb_ref, ob_ref, True))
    probs = []
    for d, (q_ref, k_ref, v_ref, c_ref, sn_ref, ct_ref, st_ref, o_ref, rev) in enumerate(dirs):
        dist = -rel if rev else rel
        step_c = (CHUNK - 1.0 - pos_c) if rev else pos_c
        step_r = (CHUNK - 1.0 - pos_r) if rev else pos_r
        for h in range(RET_HEADS):
            probs.append(dict(r=d * RET_HEADS + h, h=h, q_ref=q_ref, k_ref=k_ref, v_ref=v_ref, c_ref=c_ref,
                              sn_ref=sn_ref, ct_ref=ct_ref, st_ref=st_ref, o_ref=o_ref, dist=dist,
                              step_c=step_c, step_r=step_r))

    for p in probs:
        h, r = p["h"], p["r"]
        p["lg"] = lg_all[r:r + 1, 0:1]
        q = _rope128(p["q_ref"][:, h * RET_DK:(h + 1) * RET_DK], p["c_ref"][...], p["sn_ref"][...]).astype(BF16)
        k_t = jnp.transpose(p["k_ref"][:, h * RET_DK:(h + 1) * RET_DK])
        k_t = k_t * p["ct_ref"][...] + pltpu.roll(k_t, RET_DK // 2, axis=0) * p["st_ref"][...]
        p["k_t"] = k_t * (RET_DK ** -0.5)
        p["v"] = p["v_ref"][:, h * RET_DV:(h + 1) * RET_DV].astype(BF16)
        p["s_prev"] = s_scr[r]
        p["sc"] = jnp.dot(q, p["k_t"].astype(BF16), preferred_element_type=F32)
        p["cross"] = jnp.dot(q, p["s_prev"].astype(BF16), preferred_element_type=F32)
    yield
    for p in probs:
        lg = p["lg"]
        zeta = jnp.exp((CHUNK - 1.0 - p["step_r"]) * lg)
        u = jnp.dot((p["k_t"] * zeta).astype(BF16), p["v"], preferred_element_type=F32)
        s_scr[p["r"]] = jnp.exp(CHUNK * lg) * p["s_prev"] + u
    yield
    for p in probs:
        h, lg, dist = p["h"], p["lg"], p["dist"]
        decay = jnp.where(dist >= 0, jnp.exp(jnp.maximum(dist, 0.0) * lg), 0.0)
        inner = jnp.dot((p["sc"] * decay).astype(BF16), p["v"], preferred_element_type=F32)
        xi = jnp.exp((p["step_c"] + 1.0) * lg)
        p["o_ref"][:, h * RET_DV:(h + 1) * RET_DV] = inner + p["cross"] * xi


def _mlstm_body(gb_ref, qf_ref, kf_ref, vf_ref, gf_ref, qb_ref, kb_ref, vb_ref, gbk_ref,
                of_ref, ob_ref, c_scr, m_scr):
    H = MLSTM_HEADS
    dv = MLSTM_DV
    rel = _tri_masks()
    lower = (rel >= 0).astype(F32)
    upper = (rel <= 0).astype(F32)
    exact = dict(precision=lax.Precision.HIGHEST, preferred_element_type=F32)
    ones = jnp.ones((CHUNK, LANE), BF16)
    dirs = ((qf_ref, kf_ref, vf_ref, gf_ref, of_ref, False),
            (qb_ref, kb_ref, vb_ref, gbk_ref, ob_ref, True))
    probs = []
    for d, (q_ref, k_ref, v_ref, g_ref, o_ref, rev) in enumerate(dirs):
        gates = g_ref[...] + gb_ref[...]
        cs_col = jnp.dot(upper if rev else lower, _log_sigmoid(gates), **exact)
        g8 = jnp.transpose(gates)[2 * d * H:2 * d * H + SUBLANE, :]
        cs = jnp.dot(_log_sigmoid(g8), lower if rev else upper, **exact)
        mask = (rel <= 0) if rev else (rel >= 0)
        for h in range(H):
            probs.append(dict(
                r=d * H + h, h=h, rev=rev, mask=mask, q_ref=q_ref, k_ref=k_ref, v_ref=v_ref, o_ref=o_ref,
                i_row=g8[h:h + 1, :], b_row=cs[H + h:H + h + 1, :],
                b_col=cs_col[:, 2 * d * H + H + h:2 * d * H + H + h + 1]))

    nt = (((1,), (1,)), ((), ()))
    for p in probs:
        h = p["h"]
        p["q"] = (p["q_ref"][:, h * MLSTM_DK:(h + 1) * MLSTM_DK] * (MLSTM_DK ** -0.5)).astype(BF16)
        p["k"] = p["k_ref"][:, h * MLSTM_DK:(h + 1) * MLSTM_DK]
        p["c_prev"] = c_scr[p["r"]]
        p["qk"] = lax.dot_general(p["q"], p["k"].astype(BF16), nt, preferred_element_type=F32)
        p["qc"] = jnp.dot(p["q"], p["c_prev"].astype(BF16), preferred_element_type=F32)
    for p in probs:
        p["m_prev"] = m_scr[p["r"]:p["r"] + 1, 0:1]
        p["log_d"] = jnp.where(p["mask"], p["b_col"] + (p["i_row"] - p["b_row"]), -jnp.inf)
        p["log_inter"] = p["b_col"] + p["m_prev"]
        p["m_t"] = jnp.maximum(p["log_inter"], jnp.max(p["log_d"], axis=1, keepdims=True))
    yield
    for p in probs:
        h = p["h"]
        p["k_t"] = jnp.transpose(p["k"])
        p["v_ext"] = jnp.concatenate([p["v_ref"][:, h * dv:(h + 1) * dv].astype(BF16), ones], axis=1)
    yield
    for p in probs:
        r = p["r"]
        b_row = p["b_row"]
        b_last = b_row[:, 0:1] if p["rev"] else b_row[:, CHUNK - 1:CHUNK]
        log_w = b_last - b_row + p["i_row"]
        m_loc = jnp.max(log_w, axis=1, keepdims=True)
        kw_t = (p["k_t"] * jnp.exp(log_w - m_loc)).astype(BF16)
        u = jnp.dot(kw_t, p["v_ext"], preferred_element_type=F32)
        m_new = jnp.maximum(b_last + p["m_prev"], m_loc)
        c_scr[r] = jnp.exp(b_last + p["m_prev"] - m_new) * p["c_prev"] + jnp.exp(m_loc - m_new) * u
        m_scr[r:r + 1, :] = jnp.broadcast_to(m_new, (1, LANE))
    yield
    for p in probs:
        s = (p["qk"] * jnp.exp(p["log_d"] - p["m_t"])).astype(BF16)
        p["sv"] = jnp.dot(s, p["v_ext"], preferred_element_type=F32)
    yield
    for p in probs:
        h = p["h"]
        tot = p["sv"] + p["qc"] * jnp.exp(p["log_inter"] - p["m_t"])
        den = tot[:, dv:dv + 1]
        p["o_ref"][:, h * dv:(h + 1) * dv] = tot[:, :dv] / jnp.maximum(jnp.abs(den), jnp.exp(-p["m_t"]))


N_RET_IN = 14
N_MLSTM_IN = 8


def _mixer_ab_kernel(dl_ref, gb_ref, *refs):
    ret_in, refs = refs[:N_RET_IN], refs[N_RET_IN:]
    ml_in, refs = refs[:N_MLSTM_IN], refs[N_MLSTM_IN:]
    ret_f_ref, ret_b_ref, ml_f_ref, ml_b_ref, s_scr, c_scr, m_scr = refs

    @pl.when(pl.program_id(1) == 0)
    def _():
        s_scr[...] = jnp.zeros_like(s_scr)
        c_scr[...] = jnp.zeros_like(c_scr)
        m_scr[...] = jnp.full_like(m_scr, NEG_BIG)

    pending = [_mlstm_body(gb_ref, *ml_in, ml_f_ref, ml_b_ref, c_scr, m_scr),
               _ret_body(dl_ref, *ret_in, ret_f_ref, ret_b_ref, s_scr)]
    done = object()
    while pending:
        pending = [g for g in pending if next(g, done) is not done]


def _mixer_ab(p, p_gates, decay_logit, gate_b, cos_t, sin_t, rows):
    n = p.shape[0]
    B = rows.B
    H = MLSTM_HEADS
    n_steps, seq_f, seq_b, unit = _chunk_maps(rows)
    rq, rv = RET_HEADS * RET_DK, RET_HEADS * RET_DV
    mq, mv = H * MLSTM_DK, H * MLSTM_DV
    dl = jnp.broadcast_to(decay_logit.astype(F32).reshape(2 * RET_HEADS, 1), (2 * RET_HEADS, LANE))
    gb = jnp.zeros((1, LANE), F32).at[0, :4 * H].set(gate_b.astype(F32).reshape(4 * H))

    def chunk(width, col0, seq):
        return pl.BlockSpec((CHUNK, width), lambda b, j: (unit(b, seq(j)), col0 // width))

    def ret_specs(seq):
        return [chunk(rq, AB_QR, seq), chunk(rq, AB_KR, seq), chunk(rv, AB_VR, seq),
                pl.BlockSpec((CHUNK, RET_DK), lambda b, j: (seq(j), 0)),
                pl.BlockSpec((CHUNK, RET_DK), lambda b, j: (seq(j), 0)),
                pl.BlockSpec((RET_DK, CHUNK), lambda b, j: (0, seq(j))),
                pl.BlockSpec((RET_DK, CHUNK), lambda b, j: (0, seq(j)))]

    def ml_specs(seq):
        return [chunk(mq, AB_QM, seq), chunk(mq, AB_KM, seq), chunk(mv, AB_VM, seq), chunk(LANE, 0, seq)]

    const = lambda r: pl.BlockSpec((r, LANE), lambda b, j: (0, 0))
    ret_args = (p, p, p, cos_t, sin_t, cos_t.T, sin_t.T)
    return pl.pallas_call(
        _mixer_ab_kernel,
        grid=(B, n_steps),
        in_specs=([const(2 * RET_HEADS), const(1)] + ret_specs(seq_f) + ret_specs(seq_b)
                  + ml_specs(seq_f) + ml_specs(seq_b)),
        out_specs=[chunk(rv, 0, seq_f), chunk(rv, 0, seq_b), chunk(mv, 0, seq_f), chunk(mv, 0, seq_b)],
        out_shape=[jax.ShapeDtypeStruct((n, rv), F32)] * 2 + [jax.ShapeDtypeStruct((n, mv), F32)] * 2,
        scratch_shapes=[pltpu.VMEM((2 * RET_HEADS, RET_DK, RET_DV), F32),
                        pltpu.VMEM((2 * H, MLSTM_DK, MLSTM_DV + LANE), F32),
                        pltpu.VMEM((2 * SUBLANE, LANE), F32)],
        compiler_params=_params(("arbitrary", "arbitrary")),
        name="mixer_ab",
    )(dl, gb, *ret_args, *ret_args, p, p, p, p_gates, p, p, p, p_gates)


def _head_ln(y, g, heads, width):
    outs = []
    for h in range(heads):
        yh = y[:, h * width:(h + 1) * width]
        mu = jnp.mean(yh, axis=-1, keepdims=True)
        yc = yh - mu
        var = jnp.mean(yc * yc, axis=-1, keepdims=True)
        outs.append(yc * lax.rsqrt(var + EPS))
    return jnp.concatenate(outs, axis=1) * g


def _merge_ab_kernel(x_ref, mod_ref, rf_ref, rb_ref, mf_ref, mb_ref, gr_ref, gm_ref, rg_ref, mg_ref,
                     w1_ref, w2_ref, o_ref):
    gr = gr_ref[...]
    ret_y = (gr * _sigmoid(gr)) * _head_ln(rf_ref[...] + rb_ref[...], rg_ref[...], RET_HEADS, RET_DV)
    ml_y = _sigmoid(gm_ref[...]) * _head_ln(mf_ref[...] + mb_ref[...], mg_ref[...], MLSTM_HEADS, MLSTM_DV)
    y = (jnp.dot(ret_y.astype(BF16), w1_ref[...], preferred_element_type=F32)
         + jnp.dot(ml_y.astype(BF16), w2_ref[...], preferred_element_type=F32))
    o_ref[...] = x_ref[...] + mod_ref[5:6, :] * y


def _merge_ab(x, mods, ret_f, ret_b, ml_f, ml_b, p, ret_g, ml_g, w_out, rows, *, tm):
    n, D = x.shape
    rw = RET_HEADS * RET_DV
    mw = MLSTM_HEADS * MLSTM_DV
    row = lambda i: (i, 0)
    return pl.pallas_call(
        _merge_ab_kernel,
        grid=(n // tm,),
        in_specs=[pl.BlockSpec((tm, D), row),
                  pl.BlockSpec((None, N_MOD, D), lambda i: (rows.mod_row(i, tm, True), 0, 0)),
                  pl.BlockSpec((tm, rw), row), pl.BlockSpec((tm, rw), row),
                  pl.BlockSpec((tm, mw), row), pl.BlockSpec((tm, mw), row),
                  pl.BlockSpec((tm, rw), lambda i: (i, AB_GR // rw)),
                  pl.BlockSpec((tm, mw), lambda i: (i, AB_GM // mw)),
                  pl.BlockSpec((1, rw), lambda i: (0, 0)),
                  pl.BlockSpec((1, mw), lambda i: (0, 0)),
                  pl.BlockSpec((rw, D), lambda i: (0, 0)),
                  pl.BlockSpec((mw, D), lambda i: (rw // mw, 0))],
        out_specs=pl.BlockSpec((tm, D), row),
        out_shape=jax.ShapeDtypeStruct((n, D), F32),
        compiler_params=_params(("arbitrary",)),
        name="merge_ab",
    )(x, mods, ret_f, ret_b, ml_f, ml_b, p, p, ret_g.reshape(1, rw), ml_g.reshape(1, mw), w_out, w_out)


def _lru_coef_kernel(x_ref, xp_ref, xn_ref, cw_ref, cb_ref, wa_ref, wx_ref, ba_ref, bx_ref, lam_ref,
                     a_ref, b_ref, *, tm, ctx_tiles, ctx_seg, lat_seg):
    i = pl.program_id(0)
    seg_pos = jnp.where(i < ctx_tiles, i % ctx_seg, (i - ctx_tiles) % lat_seg)
    seg_len = jnp.where(i < ctx_tiles, ctx_seg, lat_seg)
    keep_prev = (seg_pos != 0).astype(F32)
    keep_next = (seg_pos != seg_len - 1).astype(F32)
    xe = jnp.concatenate([xp_ref[...] * keep_prev, x_ref[...], xn_ref[...] * keep_next], axis=0)
    ne = tm + 2 * SUBLANE
    xc = cb_ref[...] + cw_ref[2:3, :] * x_ref[...]
    for tap, off in ((0, -2), (1, -1), (3, 1)):
        shifted = pltpu.roll(xe, (-off) % ne, axis=0)[SUBLANE:SUBLANE + tm, :]
        xc = xc + cw_ref[tap:tap + 1, :] * shifted
    for d in range(2):
        lam = lam_ref[d:d + 1, :]
        sp = jnp.maximum(-lam, 0.0) + jnp.log(1.0 + jnp.exp(-jnp.abs(lam)))
        half_rate = (-0.5 * LRU_C * LOG2E) * sp
        for g in range(LRU_BLOCKS):
            sl = slice(g * LRU_BLOCK, (g + 1) * LRU_BLOCK)
            xg = xc[:, sl]
            xg16 = xg.astype(BF16)
            t_r = jnp.tanh(jnp.dot(xg16, wa_ref[d, g], preferred_element_type=F32) + ba_ref[d:d + 1, sl])
            t_i = jnp.tanh(jnp.dot(xg16, wx_ref[d, g], preferred_element_type=F32) + bx_ref[d:d + 1, sl])
            a = jnp.exp2(t_r * half_rate[:, sl] + half_rate[:, sl])
            half_x = 0.5 * xg
            a_ref[d, :, sl] = a
            y = 1.0 - a * a
            root = jnp.where(y > 0.0, y * lax.rsqrt(y), 0.0)
            b_ref[d, :, sl] = root * (t_i * half_x + half_x)


def _lru_coef(p, conv_w, conv_b, wa, wx, ba, bx, lam, rows, *, tm):
    n = p.shape[0]
    W = LRU_WIDTH
    tpb = tm // SUBLANE
    n8 = n // SUBLANE
    kern = functools.partial(_lru_coef_kernel, tm=tm, ctx_tiles=rows.n_ctx // tm,
                             ctx_seg=rows.Lc // tm, lat_seg=rows.S // tm)
    full = lambda *s: pl.BlockSpec(s, lambda i: (0,) * len(s))
    return pl.pallas_call(
        kern,
        grid=(n // tm,),
        in_specs=[pl.BlockSpec((tm, W), lambda i: (i, CD_XB // W)),
                  pl.BlockSpec((SUBLANE, W), lambda i: (jnp.maximum(i * tpb - 1, 0), CD_XB // W)),
                  pl.BlockSpec((SUBLANE, W), lambda i: (jnp.minimum((i + 1) * tpb, n8 - 1), CD_XB // W)),
                  full(4, W), full(1, W), full(2, LRU_BLOCKS, LRU_BLOCK, LRU_BLOCK),
                  full(2, LRU_BLOCKS, LRU_BLOCK, LRU_BLOCK), full(2, W), full(2, W), full(2, W)],
        out_specs=[pl.BlockSpec((2, tm, W), lambda i: (0, i, 0))] * 2,
        out_shape=[jax.ShapeDtypeStruct((2, n, W), F32)] * 2,
        compiler_params=_params(("arbitrary",)),
        name="lru_coef",
    )(p, p, p, conv_w, conv_b.reshape(1, W), wa, wx, ba, bx, lam)


def _lru_scan_kernel(af_ref, bf_ref, ab_ref, bb_ref, of_ref, ob_ref, h_scr, *, tb, lw):
    @pl.when(pl.program_id(1) == 0)
    def _():
        h_scr[...] = jnp.zeros_like(h_scr)

    row = lax.broadcasted_iota(jnp.int32, (SUBLANE, lw), 0)
    ng = tb // SUBLANE

    def scan8(a, b, rev):
        d = 1
        while d < SUBLANE:
            if rev:
                keep = row < SUBLANE - d
                sh = SUBLANE - d
            else:
                keep = row >= d
                sh = d
            b = a * jnp.where(keep, pltpu.roll(b, sh, axis=0), 0.0) + b
            a = a * jnp.where(keep, pltpu.roll(a, sh, axis=0), 1.0)
            d *= 2
        return a, b

    for c in range(LRU_WIDTH // lw):
        cs = slice(c * lw, (c + 1) * lw)

        def body(g, carry):
            hf, hb = carry
            rf = pl.multiple_of(g * SUBLANE, SUBLANE)
            a, b = scan8(af_ref[pl.ds(rf, SUBLANE), cs], bf_ref[pl.ds(rf, SUBLANE), cs], False)
            out = a * hf + b
            of_ref[pl.ds(rf, SUBLANE), cs] = out
            hf = jnp.broadcast_to(out[SUBLANE - 1:SUBLANE, :], (SUBLANE, lw))
            rb = pl.multiple_of((ng - 1 - g) * SUBLANE, SUBLANE)
            a, b = scan8(ab_ref[pl.ds(rb, SUBLANE), cs], bb_ref[pl.ds(rb, SUBLANE), cs], True)
            out = a * hb + b
            ob_ref[pl.ds(rb, SUBLANE), cs] = out
            hb = jnp.broadcast_to(out[0:1, :], (SUBLANE, lw))
            return hf, hb

        hf, hb = lax.fori_loop(0, ng, body, (h_scr[0, :, cs], h_scr[1, :, cs]), unroll=4)
        h_scr[0, :, cs] = hf
        h_scr[1, :, cs] = hb


def _lru_scan(a, b, rows):
    n = a.shape[1]
    W = LRU_WIDTH
    B = rows.B
    tb = rows.Lc
    nlb = rows.S // tb

    def blk_f(b_, j):
        return jnp.where(j == 0, b_, B + b_ * nlb + (j - 1))

    def blk_b(b_, j):
        return jnp.where(j == 0, b_, B + b_ * nlb + (nlb - j))

    def spec(d, blk):
        return pl.BlockSpec((None, tb, W), lambda b_, j: (d, blk(b_, j), 0))

    kern = functools.partial(_lru_scan_kernel, tb=tb, lw=512)
    return pl.pallas_call(
        kern,
        grid=(B, 1 + nlb),
        in_specs=[spec(0, blk_f), spec(0, blk_f), spec(1, blk_b), spec(1, blk_b)],
        out_specs=[pl.BlockSpec((tb, W), lambda b_, j: (blk_f(b_, j), 0)),
                   pl.BlockSpec((tb, W), lambda b_, j: (blk_b(b_, j), 0))],
        out_shape=[jax.ShapeDtypeStruct((n, W), F32)] * 2,
        scratch_shapes=[pltpu.VMEM((2, SUBLANE, W), F32)],
        compiler_params=_params(("arbitrary", "arbitrary")),
        name="lru_scan",
    )(a, b, a, b)


def _mla_qkv_kernel(cq_ref, ckv_ref, kr_ref, qg_ref, kvg_ref, wq_ref, wk_ref, wv_ref, qkg_ref, cos_ref, sin_ref,
                    q_ref, k_ref, v_ref):
    cqn = _rms(cq_ref[...], qg_ref[...]).astype(BF16)
    ckvn = _rms(ckv_ref[...], kvg_ref[...]).astype(BF16)
    q_all = jnp.dot(cqn, wq_ref[...], preferred_element_type=F32)
    kn_all = jnp.dot(ckvn, wk_ref[...], preferred_element_type=F32)
    v_all = jnp.dot(ckvn, wv_ref[...], preferred_element_type=F32).astype(BF16)
    ones = jnp.ones((v_all.shape[0], LANE), BF16)
    for h in range(MLA_HEADS):
        v_ref[:, h * (MLA_V + LANE):(h + 1) * (MLA_V + LANE)] = jnp.concatenate(
            [v_all[:, h * MLA_V:(h + 1) * MLA_V], ones], axis=1)
    kr = kr_ref[...]
    cos = cos_ref[...]
    sin = sin_ref[...]
    lane = lax.broadcasted_iota(jnp.int32, cos.shape, 1)
    first_half = (lane % (MLA_ROPE // 2)) < (MLA_ROPE // 4)
    hi_w = MLA_PAD - MLA_NOPE

    def rope(hi):
        rot = jnp.where(first_half, pltpu.roll(hi, hi_w - MLA_ROPE // 4, axis=1),
                        pltpu.roll(hi, MLA_ROPE // 4, axis=1))
        return hi * cos + rot * sin

    def inv_rms(lo, hi):
        return lax.rsqrt(jnp.sum(lo * lo + hi * hi, axis=-1, keepdims=True) * (1.0 / MLA_QK) + EPS)

    g_q, g_k = qkg_ref[0:1, :], qkg_ref[1:2, :]
    kr_rot = rope(kr * g_k[:, MLA_NOPE:])
    for h in range(MLA_HEADS):
        lo_sl = slice(h * MLA_PAD, h * MLA_PAD + MLA_NOPE)
        hi_sl = slice(h * MLA_PAD + MLA_NOPE, (h + 1) * MLA_PAD)
        q_lo, q_hi = q_all[:, lo_sl], q_all[:, hi_sl]
        r = inv_rms(q_lo, q_hi) * (MLA_QK ** -0.5 * LOG2E)
        q_ref[:, lo_sl] = (q_lo * r * g_q[:, :MLA_NOPE]).astype(BF16)
        q_ref[:, hi_sl] = rope(q_hi * r * g_q[:, MLA_NOPE:]).astype(BF16)
        k_lo = kn_all[:, h * MLA_NOPE:(h + 1) * MLA_NOPE]
        r = inv_rms(k_lo, kr)
        k_ref[:, lo_sl] = (k_lo * r * g_k[:, :MLA_NOPE]).astype(BF16)
        k_ref[:, hi_sl] = (kr_rot * r).astype(BF16)


def _mla_qkv(p, q_norm_g, kv_norm_g, wq, wk, wv, qk_g, cos_t, sin_t, rows, *, tm):
    B, Lc, S = rows.B, rows.Lc, rows.S
    n = p.shape[0]
    nct = rows.n_ctx // tm
    cpb = Lc // tm
    lpb = S // tm

    def seq_blk(i):
        il = jnp.maximum(i - nct, 0)
        return (jnp.where(i < nct, i // cpb, il // lpb), jnp.where(i < nct, lpb + i % cpb, il % lpb), 0)

    def pos_blk(i):
        return (jnp.where(i < nct, i % cpb, cpb + jnp.maximum(i - nct, 0) % lpb), 0)

    full = lambda *s: pl.BlockSpec(s, lambda i: (0,) * len(s))
    qkw = MLA_HEADS * MLA_PAD
    vw = MLA_HEADS * MLA_V
    return pl.pallas_call(
        _mla_qkv_kernel,
        grid=(n // tm,),
        in_specs=[pl.BlockSpec((tm, MLA_Q_RANK), lambda i: (i, CD_CQ // MLA_Q_RANK)),
                  pl.BlockSpec((tm, MLA_KV_RANK), lambda i: (i, CD_CKV // MLA_KV_RANK)),
                  pl.BlockSpec((tm, LANE), lambda i: (i, CD_KR // LANE)),
                  full(1, MLA_Q_RANK), full(1, MLA_KV_RANK), full(MLA_Q_RANK, qkw),
                  full(MLA_KV_RANK, MLA_HEADS * MLA_NOPE), full(MLA_KV_RANK, vw), full(2, MLA_PAD),
                  pl.BlockSpec((tm, MLA_PAD - MLA_NOPE), pos_blk),
                  pl.BlockSpec((tm, MLA_PAD - MLA_NOPE), pos_blk)],
        out_specs=[pl.BlockSpec((None, tm, qkw), seq_blk), pl.BlockSpec((None, tm, qkw), seq_blk),
                   pl.BlockSpec((None, tm, vw + MLA_HEADS * LANE), seq_blk)],
        out_shape=[jax.ShapeDtypeStruct((B, Lc + S, qkw), BF16), jax.ShapeDtypeStruct((B, Lc + S, qkw), BF16),
                   jax.ShapeDtypeStruct((B, Lc + S, vw + MLA_HEADS * LANE), BF16)],
        compiler_params=_params(("arbitrary",)),
        name="mla_qkv",
    )(p, p, p, q_norm_g.reshape(1, -1), kv_norm_g.reshape(1, -1), wq, wk, wv, qk_g, cos_t, sin_t)


def _attn_kernel(q_ref, k_ref, v_ref, o_ref, *, sub):
    nt = (((1,), (1,)), ((), ()))
    groups = list(range(0, q_ref.shape[0], sub))

    def scores(r):
        return lax.dot_general(q_ref[r:r + sub, :], k_ref[...], nt, preferred_element_type=F32)

    s_next = scores(groups[0])
    for g, r in enumerate(groups):
        s = s_next
        if g + 1 < len(groups):
            s_next = scores(groups[g + 1])
        e = jnp.exp2(s - jnp.max(s, axis=-1, keepdims=True)).astype(BF16)
        ov = jnp.dot(e, v_ref[...], preferred_element_type=F32)
        o_ref[r:r + sub, :] = (ov[:, :MLA_V] / ov[:, MLA_V:MLA_V + 1]).astype(BF16)


def _attention(q, k, v, rows, *, tq):
    B, Lc, S = rows.B, rows.Lc, rows.S
    nq = S // tq
    return pl.pallas_call(
        functools.partial(_attn_kernel, sub=min(tq, 256)),
        grid=(B, MLA_HEADS, nq),
        in_specs=[pl.BlockSpec((None, tq, MLA_PAD), lambda b, h, i: (b, i, h)),
                  pl.BlockSpec((None, Lc + S, MLA_PAD), lambda b, h, i: (b, 0, h)),
                  pl.BlockSpec((None, Lc + S, MLA_V + LANE), lambda b, h, i: (b, 0, h))],
        out_specs=pl.BlockSpec((tq, MLA_V), lambda b, h, i: (b * nq + i, h)),
        out_shape=jax.ShapeDtypeStruct((B * S, MLA_HEADS * MLA_V), BF16),
        compiler_params=_params(("arbitrary", "arbitrary", "arbitrary")),
        name="attention",
    )(q, k, v)


def _gelu_tanh(x):
    return 0.5 * x * (1.0 + jnp.tanh(math.sqrt(2.0 / math.pi) * (x + 0.044715 * (x * x * x))))


def _merge_cd_kernel(x_ref, mod_ref, yb_ref, hf_ref, hb_ref, att_ref, w1_ref, w2_ref, o_ref):
    y1 = _gelu_tanh(yb_ref[...]) * (hf_ref[...] + hb_ref[...])
    y = (jnp.dot(y1.astype(BF16), w1_ref[...], preferred_element_type=F32)
         + jnp.dot(att_ref[...], w2_ref[...], preferred_element_type=F32))
    o_ref[...] = x_ref[...] + mod_ref[5:6, :] * y


def _merge_cd(x, mods, p, h_f, h_b, att, w_out, rows, *, tm):
    D = x.shape[1]
    W = LRU_WIDTH
    aw = MLA_HEADS * MLA_V
    off = rows.n_ctx // tm
    lat = lambda i: (i + off, 0)
    return pl.pallas_call(
        _merge_cd_kernel,
        grid=(rows.n_lat // tm,),
        in_specs=[pl.BlockSpec((tm, D), lat),
                  pl.BlockSpec((None, N_MOD, D), lambda i: (rows.mod_row(i, tm, False), 0, 0)),
                  pl.BlockSpec((tm, W), lambda i: (i + off, CD_YB // W)),
                  pl.BlockSpec((tm, W), lat), pl.BlockSpec((tm, W), lat),
                  pl.BlockSpec((tm, aw), lambda i: (i, 0)),
                  pl.BlockSpec((W, D), lambda i: (0, 0)),
                  pl.BlockSpec((aw, D), lambda i: (W // aw, 0))],
        out_specs=pl.BlockSpec((tm, D), lambda i: (i, 0)),
        out_shape=jax.ShapeDtypeStruct((rows.n_lat, D), F32),
        compiler_params=_params(("arbitrary",)),
        name="merge_cd",
    )(x, mods, p, h_f, h_b, att, w_out, w_out)


def _ret_tables(rows):
    half = RET_DK // 2
    freqs = ROPE_BASE ** (-jnp.arange(half, dtype=F32) / half)
    ang = jnp.arange(rows.S, dtype=F32)[:, None] * freqs
    cos = jnp.concatenate([jnp.cos(ang), jnp.cos(ang)], axis=-1)
    sin = jnp.concatenate([-jnp.sin(ang), jnp.sin(ang)], axis=-1)
    cos = jnp.concatenate([jnp.ones((rows.Lc, RET_DK), F32), cos], axis=0)
    sin = jnp.concatenate([jnp.zeros((rows.Lc, RET_DK), F32), sin], axis=0)
    return cos, sin


def _mla_tables(rows):
    S = rows.S
    quarter = MLA_ROPE // 4
    freqs = ROPE_BASE ** (-jnp.arange(quarter, dtype=F32) / quarter)
    t = jnp.arange(S)
    row = (t // GRID_W).astype(F32)
    col = (t % GRID_W).astype(F32)

    def part(pos):
        ang = pos[:, None] * freqs
        return (jnp.concatenate([jnp.cos(ang), jnp.cos(ang)], axis=-1),
                jnp.concatenate([-jnp.sin(ang), jnp.sin(ang)], axis=-1))

    rc, rs = part(row)
    cc, cs = part(col)
    tail = MLA_PAD - MLA_QK
    cos = jnp.concatenate([rc, cc, jnp.ones((S, tail), F32)], axis=-1)
    sin = jnp.concatenate([rs, cs, jnp.zeros((S, tail), F32)], axis=-1)
    cos = jnp.concatenate([jnp.ones((rows.Lc, MLA_ROPE + tail), F32), cos], axis=0)
    sin = jnp.concatenate([jnp.zeros((rows.Lc, MLA_ROPE + tail), F32), sin], axis=0)
    return cos, sin


def _pad_cols(w, n):
    return jnp.pad(w, ((0, 0), (0, n - w.shape[1])))


def _head_pad(w, heads, width, padded):
    K = w.shape[0]
    return jnp.pad(w.reshape(K, heads, width), ((0, 0), (0, 0), (0, padded - width))).reshape(K, heads * padded)


def kernel(x, c, ctx, c_ctx, ada_w, ada_b, norm_g, ffn_wg, ffn_wu, ffn_wd, ab_w_in, ab_w_out, ret_decay_logit, ret_gn_g, mlstm_gate_b, mlstm_gn_g, cd_w_in, cd_w_out, lru_conv_w, lru_conv_b, lru_wa, lru_ba, lru_wx, lru_bx, lru_lambda, mla_q_norm_g, mla_kv_norm_g, mla_w_uq, mla_w_uk, mla_w_uv, mla_qk_norm_g):
    B, S, D = x.shape
    Lc = ctx.shape[1]
    depth = ada_w.shape[0]
    F = ffn_wg.shape[-1]
    rows = Rows(B, Lc, S)
    assert B < SUBLANE and Lc % CHUNK == 0 and S % Lc == 0 and S % GRID_W == 0

    tm = min(1024, rows.n_ctx)
    tf = 512
    all_mod = lambda i: rows.mod_row(i, tm, True)
    lat_mod = lambda i: rows.mod_row(i, tm, False)

    cond = jnp.zeros((SUBLANE, D), F32).at[:B].set(c.astype(F32)).at[B].set(c_ctx.astype(F32))
    mods = _adaln(cond, ada_w, ada_b)

    ffn_stacks = (ffn_wg, ffn_wu, ffn_wd)
    w_cur = tuple(w[0, 0].astype(BF16) for w in ffn_stacks)

    def ffn_casts(l, h):
        return [(w, (l, h)) for w in ffn_stacks] if l < depth else []

    for l in range(depth):
        last = l == depth - 1
        j = l // 2
        m_l = mods[l]
        ffn1 = functools.partial(_ffn, mods=m_l, g=norm_g[l, 0], wg=w_cur[0], wu=w_cur[1], wd=w_cur[2],
                                 mod_base=0, tm=tm, tf=tf)
        casts = ffn_casts(l, 1)
        if l == 0:
            xa, w_next = ffn1(x.reshape(B * S, D).astype(F32), mod_of_tile=lat_mod, out_rows=rows.n_all,
                              out_tile0=rows.n_ctx // tm, casts=casts)
            xa = ffn1(ctx.reshape(B * Lc, D).astype(F32), mod_of_tile=lambda i: B, out_rows=rows.n_all, into=xa)
        else:
            xa, w_next = ffn1(xa, mod_of_tile=all_mod, casts=casts)
        w_cur = w_next

        if l % 2 == 0:
            w_in, w_out = ab_w_in[j].astype(BF16), ab_w_out[j].astype(BF16)
            p, p_gates = _proj(xa, m_l, norm_g[l, 1], w_in, rows, tm=tm, tn=1024, n_cols=AB_GATES,
                               w_tail=_pad_cols(w_in[:, AB_GATES:], LANE))
            cos_t, sin_t = _ret_tables(rows)
            ret_f, ret_b, ml_f, ml_b = _mixer_ab(p, p_gates, ret_decay_logit[j], mlstm_gate_b[j], cos_t, sin_t,
                                                 rows)
            xa = _merge_ab(xa, m_l, ret_f, ret_b, ml_f, ml_b, p, ret_gn_g[j], mlstm_gn_g[j], w_out, rows,
                           tm=min(256, tm))
            if last:
                xa = xa[rows.n_ctx:]
        else:
            assert last, "context outputs of the recurrent/attention mixer are not produced"
            w_in, w_out, w_uq, w_uk, w_uv = (w[j].astype(BF16) for w in
                                             (cd_w_in, cd_w_out, mla_w_uq, mla_w_uk, mla_w_uv))
            p, = _proj(xa, m_l, norm_g[l, 1], _pad_cols(w_in, CD_PAD), rows, tm=tm, tn=CD_PAD // 3)
            a_c, b_c = _lru_coef(p, lru_conv_w[j], lru_conv_b[j], (0.5 * lru_wa[j]).astype(BF16),
                                 (0.5 * lru_wx[j]).astype(BF16), 0.5 * lru_ba[j], 0.5 * lru_bx[j], lru_lambda[j],
                                 rows, tm=min(256, Lc))
            h_f, h_b = _lru_scan(a_c, b_c, rows)
            cos_t, sin_t = _mla_tables(rows)
            qk_g = jnp.pad(mla_qk_norm_g[j].astype(F32), ((0, 0), (0, MLA_PAD - MLA_QK)))
            q, k, v = _mla_qkv(p, mla_q_norm_g[j], mla_kv_norm_g[j], _head_pad(w_uq, MLA_HEADS, MLA_QK, MLA_PAD),
                               w_uk, w_uv, qk_g, cos_t, sin_t, rows, tm=min(256, Lc))
            att = _attention(q, k, v, rows, tq=min(2048, S))
            xa = _merge_cd(xa, m_l, p, h_f, h_b, att, w_out, rows, tm=min(256, tm))

        casts = ffn_casts(l + 1, 0)
        res = _ffn(xa, m_l, norm_g[l, 2], *w_cur, mod_base=6, mod_of_tile=lat_mod if last else all_mod,
                   tm=tm, tf=tf, casts=casts)
        xa, w_cur = res if casts else (res, None)

    if xa.shape[0] != rows.n_lat:
        xa = xa[rows.n_ctx:]
    return xa.reshape(B, S, D).astype(x.dtype)
```

```python
import functools
import math

import jax
import jax.numpy as jnp
import numpy as np
from jax import lax
from jax.experimental import pallas as pl
from jax.experimental.pallas import tpu as pltpu

F32 = jnp.float32
BF16 = jnp.bfloat16

N_MOD = 9
EPS = 1e-6
CHUNK = 128
GRID_W = 64
ROPE_BASE = 10000.0
NEG_BIG = -1e30

RET_HEADS = 4
RET_DK = 128
RET_DV = 256
MLSTM_HEADS = 4
MLSTM_DK = 128
MLSTM_DV = 256
LRU_WIDTH = 1024
LRU_BLOCKS = 8
LRU_BLOCK = LRU_WIDTH // LRU_BLOCKS
LRU_C = 8.0
MLA_HEADS = 8
MLA_Q_RANK = 512
MLA_KV_RANK = 256
MLA_NOPE = 128
MLA_ROPE = 64
MLA_V = 128
MLA_QK = MLA_NOPE + MLA_ROPE
LOG2E = math.log2(math.e)
MLA_PAD = 256

AB_SPLITS = (RET_HEADS * RET_DK, RET_HEADS * RET_DK, RET_HEADS * RET_DV, RET_HEADS * RET_DV,
             MLSTM_HEADS * MLSTM_DK, MLSTM_HEADS * MLSTM_DK, MLSTM_HEADS * MLSTM_DV, MLSTM_HEADS * MLSTM_DV,
             4 * MLSTM_HEADS)
AB_IN = sum(AB_SPLITS)
CD_SPLITS = (LRU_WIDTH, LRU_WIDTH, MLA_Q_RANK, MLA_KV_RANK, MLA_ROPE)
CD_IN = sum(CD_SPLITS)

LANE = 128
SUBLANE = 8
BF16_SUBLANES = 16
VMEM_LIMIT = 56 * 1024 * 1024
FFN_VMEM_LIMIT = 60 * 1024 * 1024

AB_QR, AB_KR, AB_VR, AB_GR, AB_QM, AB_KM, AB_VM, AB_GM, AB_GATES = np.cumsum((0,) + AB_SPLITS[:-1]).tolist()
CD_YB, CD_XB, CD_CQ, CD_CKV, CD_KR = np.cumsum((0,) + CD_SPLITS[:-1]).tolist()
CD_PAD = 3072


def _params(sem, vmem=VMEM_LIMIT):
    return pltpu.CompilerParams(dimension_semantics=sem, vmem_limit_bytes=vmem)


def _round_up(n, m):
    return (n + m - 1) // m * m


def _sigmoid(x):
    return 0.5 * jnp.tanh(0.5 * x) + 0.5


def _log_sigmoid(x):
    return jnp.minimum(x, 0.0) - jnp.log(1.0 + jnp.exp(-jnp.abs(x)))


def _rms(x, g):
    return x * lax.rsqrt(jnp.mean(x * x, axis=-1, keepdims=True) + EPS) * g


def _adaln_kernel(c_ref, w_ref, b_ref, o_ref):
    c = c_ref[...]
    s = (c * _sigmoid(c)).astype(BF16)
    o_ref[...] = jnp.dot(s, w_ref[...].astype(BF16), preferred_element_type=F32) + b_ref[...]


def _adaln(cond, ada_w, ada_b):
    L, D, N = ada_w.shape
    tn = 1024 if N % 1024 == 0 else N
    out = pl.pallas_call(
        _adaln_kernel,
        grid=(L, N // tn),
        in_specs=[pl.BlockSpec((SUBLANE, D), lambda l, n: (0, 0)),
                  pl.BlockSpec((None, D, tn), lambda l, n: (l, 0, n)),
                  pl.BlockSpec((None, 1, tn), lambda l, n: (l, 0, n))],
        out_specs=pl.BlockSpec((None, SUBLANE, tn), lambda l, n: (l, 0, n)),
        out_shape=jax.ShapeDtypeStruct((L, SUBLANE, N), F32),
        compiler_params=_params(("arbitrary", "arbitrary")),
        name="adaln",
    )(cond, ada_w, ada_b.reshape(L, 1, N))
    return out.reshape(L, SUBLANE, N_MOD, D)


class Rows:
    def __init__(self, B, Lc, S):
        self.B, self.Lc, self.S = B, Lc, S
        self.n_ctx = B * Lc
        self.n_lat = B * S
        self.n_all = self.n_ctx + self.n_lat

    def mod_row(self, tile, tm, with_ctx):
        if not with_ctx:
            return (tile * tm) // self.S
        nct = self.n_ctx // tm
        return jnp.where(tile < nct, self.B, (jnp.maximum(tile - nct, 0) * tm) // self.S)


def _ffn_kernel(*refs, mod_base, sub, last_cols, has_into, n_cast):
    x_ref, mod_ref, g_ref, wg_ref, wu_ref, wd_ref = refs[:6]
    cast_src = refs[6:6 + n_cast]
    o_ref = refs[6 + n_cast + has_into]
    cast_dst = refs[7 + n_cast + has_into:7 + 2 * n_cast + has_into]
    h_scr = refs[-1]
    f = pl.program_id(1)
    nf = pl.num_programs(1)
    tm = x_ref.shape[0]
    tf = wg_ref.shape[1]

    def step(cols, first, final):
        for src, dst in zip(cast_src, cast_dst):
            dst[...] = src[...].astype(BF16)
        if first:
            shift = mod_ref[mod_base:mod_base + 1, :]
            scale = 1.0 + mod_ref[mod_base + 1:mod_base + 2, :]
        if final:
            gate = 0.5 * mod_ref[mod_base + 2:mod_base + 3, :]
        for r in range(0, tm, sub):
            rs = slice(r, r + sub)
            if first:
                h = (_rms(x_ref[rs, :], g_ref[...]) * scale + shift).astype(BF16)
                h_scr[rs, :] = h
            else:
                h = h_scr[rs, :]
            a = jnp.dot(h, wg_ref[:, :cols], preferred_element_type=F32)
            u = jnp.dot(h, wu_ref[:, :cols], preferred_element_type=F32)
            act = (a * _sigmoid(a) * u).astype(BF16)
            part = jnp.dot(act, wd_ref[:cols, :], preferred_element_type=F32)
            if first:
                o_ref[rs, :] = part
            elif final:
                o_ref[rs, :] = x_ref[rs, :] + gate * (o_ref[rs, :] + part)
            else:
                o_ref[rs, :] += part

    pl.when(f == 0)(functools.partial(step, tf, True, False))
    pl.when(jnp.logical_and(f > 0, f < nf - 1))(functools.partial(step, tf, False, False))
    pl.when(f == nf - 1)(functools.partial(step, last_cols, False, True))


def _ffn(x, mods, g, wg, wu, wd, *, mod_base, mod_of_tile, tm, tf, out_rows=None, out_tile0=0, into=None,
         casts=()):
    n, D = x.shape
    F = wg.shape[-1]
    nf = pl.cdiv(F, tf)
    ni = n // tm
    assert nf >= 2 and n % tm == 0
    out_rows = n if out_rows is None else out_rows
    in_specs = [pl.BlockSpec((tm, D), lambda i, f: (i, 0)),
                pl.BlockSpec((None, N_MOD, D), lambda i, f: (mod_of_tile(i), 0, 0)),
                pl.BlockSpec((1, D), lambda i, f: (0, 0)),
                pl.BlockSpec((D, tf), lambda i, f: (0, f)),
                pl.BlockSpec((D, tf), lambda i, f: (0, f)),
                pl.BlockSpec((tf, D), lambda i, f: (f, 0))]
    args = [x, mods, g.reshape(1, D), wg, wu, wd]
    out_specs = [pl.BlockSpec((tm, D), lambda i, f: (out_tile0 + i, 0))]
    out_shape = [jax.ShapeDtypeStruct((out_rows, D), F32)]
    n_cast = len(casts)
    for w, lead in casts:
        rows_w, cols_w = w.shape[len(lead):]
        slab_rows = next(m for m in range(BF16_SUBLANES * pl.cdiv(rows_w, BF16_SUBLANES * ni * nf),
                                          rows_w + 1, BF16_SUBLANES) if rows_w % m == 0)
        nblk = rows_w // slab_rows
        slab = lambda i, f, nblk=nblk: jnp.minimum(i * nf + f, nblk - 1)
        in_specs.append(pl.BlockSpec((None,) * len(lead) + (slab_rows, cols_w),
                                     lambda i, f, slab=slab, lead=lead: lead + (slab(i, f), 0)))
        out_specs.append(pl.BlockSpec((slab_rows, cols_w), lambda i, f, slab=slab: (slab(i, f), 0)))
        out_shape.append(jax.ShapeDtypeStruct((rows_w, cols_w), BF16))
        args.append(w)
    aliases = {}
    if into is not None:
        assert into.shape == (out_rows, D)
        aliases = {len(args): 0}
        in_specs.append(pl.BlockSpec(memory_space=pl.ANY))
        args.append(into)
    kern = functools.partial(_ffn_kernel, mod_base=mod_base, sub=min(tm, 512), last_cols=F - (nf - 1) * tf,
                             has_into=int(into is not None), n_cast=n_cast)
    res = pl.pallas_call(
        kern,
        grid=(ni, nf),
        in_specs=in_specs,
        out_specs=out_specs,
        out_shape=out_shape,
        scratch_shapes=[pltpu.VMEM((tm, D), BF16)],
        input_output_aliases=aliases,
        compiler_params=_params(("arbitrary", "arbitrary"), FFN_VMEM_LIMIT),
        name="ffn",
    )(*args)
    return (res[0], tuple(res[1:])) if n_cast else res[0]


def _proj_kernel(x_ref, mod_ref, g_ref, w_ref, *rest, sub):
    wt_ref, o_ref, ot_ref, h_scr = rest if len(rest) == 4 else (None, rest[0], None, rest[1])
    tm = x_ref.shape[0]

    @pl.when(pl.program_id(1) == 0)
    def _():
        scale = 1.0 + mod_ref[4:5, :]
        for r in range(0, tm, sub):
            h = (_rms(x_ref[r:r + sub, :], g_ref[...]) * scale + mod_ref[3:4, :]).astype(BF16)
            h_scr[r:r + sub, :] = h
            o_ref[r:r + sub, :] = jnp.dot(h, w_ref[...], preferred_element_type=F32)
            if wt_ref is not None:
                ot_ref[r:r + sub, :] = jnp.dot(h, wt_ref[...], preferred_element_type=F32)

    @pl.when(pl.program_id(1) != 0)
    def _():
        for r in range(0, tm, sub):
            o_ref[r:r + sub, :] = jnp.dot(h_scr[r:r + sub, :], w_ref[...], preferred_element_type=F32)


def _proj(x, mods, g, w, rows, *, tm, tn, n_cols=None, w_tail=None):
    n, D = x.shape
    N = w.shape[1] if n_cols is None else n_cols
    assert N % tn == 0
    in_specs = [pl.BlockSpec((tm, D), lambda i, j: (i, 0)),
                pl.BlockSpec((None, N_MOD, D), lambda i, j: (rows.mod_row(i, tm, True), 0, 0)),
                pl.BlockSpec((1, D), lambda i, j: (0, 0)),
                pl.BlockSpec((D, tn), lambda i, j: (0, j))]
    out_specs = [pl.BlockSpec((tm, tn), lambda i, j: (i, j))]
    out_shape = [jax.ShapeDtypeStruct((n, N), F32)]
    args = [x, mods, g.reshape(1, D), w]
    if w_tail is not None:
        in_specs.append(pl.BlockSpec((D, LANE), lambda i, j: (0, 0)))
        out_specs.append(pl.BlockSpec((tm, LANE), lambda i, j: (i, 0)))
        out_shape.append(jax.ShapeDtypeStruct((n, LANE), F32))
        args.append(w_tail)
    return pl.pallas_call(
        functools.partial(_proj_kernel, sub=min(tm, 512)),
        grid=(n // tm, N // tn),
        in_specs=in_specs,
        out_specs=out_specs,
        out_shape=out_shape,
        scratch_shapes=[pltpu.VMEM((tm, D), BF16)],
        compiler_params=_params(("arbitrary", "arbitrary")),
        name="proj",
    )(*args)


def _chunk_maps(rows):
    B = rows.B
    nc = rows.Lc // CHUNK
    nl = rows.S // CHUNK

    def seq_f(j):
        return j

    def seq_b(j):
        return jnp.where(j < nc, nc - 1 - j, nc + nl - 1 - (j - nc))

    def unit(b, s):
        return jnp.where(s < nc, b * nc + s, B * nc + b * nl + (s - nc))

    return nc + nl, seq_f, seq_b, unit


def _rope128(x, cos, sin):
    return x * cos + pltpu.roll(x, 64, axis=1) * sin


def _tri_masks():
    r = lax.broadcasted_iota(jnp.int32, (CHUNK, CHUNK), 0)
    c = lax.broadcasted_iota(jnp.int32, (CHUNK, CHUNK), 1)
    return r - c


def _ret_body(dl_ref, qf_ref, kf_ref, vf_ref, cf_ref, sf_ref, ctf_ref, stf_ref,
              qb_ref, kb_ref, vb_ref, cb_ref, sb_ref, ctb_ref, stb_ref, of_ref, ob_ref, s_scr):
    lg_all = _log_sigmoid(dl_ref[...])
    rel = _tri_masks().astype(F32)
    pos_c = lax.broadcasted_iota(jnp.int32, (CHUNK, 1), 0).astype(F32)
    pos_r = lax.broadcasted_iota(jnp.int32, (1, CHUNK), 1).astype(F32)
    dirs = ((qf_ref, kf_ref, vf_ref, cf_ref, sf_ref, ctf_ref, stf_ref, of_ref, False),
            (qb_ref, kb_ref, vb_ref, cb_ref, sb_ref, ctb_ref, stb_ref, ob_ref, True))
    probs = []
    for d, (q_ref, k_ref, v_ref, c_ref, sn_ref, ct_ref, st_ref, o_ref, rev) in enumerate(dirs):
        dist = -rel if rev else rel
        step_c = (CHUNK - 1.0 - pos_c) if rev else pos_c
        step_r = (CHUNK - 1.0 - pos_r) if rev else pos_r
        for h in range(RET_HEADS):
            probs.append(dict(r=d * RET_HEADS + h, h=h, q_ref=q_ref, k_ref=k_ref, v_ref=v_ref, c_ref=c_ref,
                              sn_ref=sn_ref, ct_ref=ct_ref, st_ref=st_ref, o_ref=o_ref, dist=dist,
                              step_c=step_c, step_r=step_r))

    for p in probs:
        h, r = p["h"], p["r"]
        p["lg"] = lg_all[r:r + 1, 0:1]
        q = _rope128(p["q_ref"][:, h * RET_DK:(h + 1) * RET_DK], p["c_ref"][...], p["sn_ref"][...]).astype(BF16)
        k_t = jnp.transpose(p["k_ref"][:, h * RET_DK:(h + 1) * RET_DK])
        k_t = k_t * p["ct_ref"][...] + pltpu.roll(k_t, RET_DK // 2, axis=0) * p["st_ref"][...]
        p["k_t"] = k_t * (RET_DK ** -0.5)
        p["v"] = p["v_ref"][:, h * RET_DV:(h + 1) * RET_DV].astype(BF16)
        p["s_prev"] = s_scr[r]
        p["sc"] = jnp.dot(q, p["k_t"].astype(BF16), preferred_element_type=F32)
        p["cross"] = jnp.dot(q, p["s_prev"].astype(BF16), preferred_element_type=F32)
    yield
    for p in probs:
        lg = p["lg"]
        zeta = jnp.exp((CHUNK - 1.0 - p["step_r"]) * lg)
        u = jnp.dot((p["k_t"] * zeta).astype(BF16), p["v"], preferred_element_type=F32)
        s_scr[p["r"]] = jnp.exp(CHUNK * lg) * p["s_prev"] + u
    yield
    for p in probs:
        h, lg, dist = p["h"], p["lg"], p["dist"]
        decay = jnp.where(dist >= 0, jnp.exp(jnp.maximum(dist, 0.0) * lg), 0.0)
        inner = jnp.dot((p["sc"] * decay).astype(BF16), p["v"], preferred_element_type=F32)
        xi = jnp.exp((p["step_c"] + 1.0) * lg)
        p["o_ref"][:, h * RET_DV:(h + 1) * RET_DV] = inner + p["cross"] * xi


def _mlstm_body(gb_ref, qf_ref, kf_ref, vf_ref, gf_ref, qb_ref, kb_ref, vb_ref, gbk_ref,
                of_ref, ob_ref, c_scr, m_scr):
    H = MLSTM_HEADS
    dv = MLSTM_DV
    rel = _tri_masks()
    lower = (rel >= 0).astype(F32)
    upper = (rel <= 0).astype(F32)
    exact = dict(precision=lax.Precision.HIGHEST, preferred_element_type=F32)
    ones = jnp.ones((CHUNK, LANE), BF16)
    dirs = ((qf_ref, kf_ref, vf_ref, gf_ref, of_ref, False),
            (qb_ref, kb_ref, vb_ref, gbk_ref, ob_ref, True))
    probs = []
    for d, (q_ref, k_ref, v_ref, g_ref, o_ref, rev) in enumerate(dirs):
        gates = g_ref[...] + gb_ref[...]
        cs_col = jnp.dot(upper if rev else lower, _log_sigmoid(gates), **exact)
        g8 = jnp.transpose(gates)[2 * d * H:2 * d * H + SUBLANE, :]
        cs = jnp.dot(_log_sigmoid(g8), lower if rev else upper, **exact)
        mask = (rel <= 0) if rev else (rel >= 0)
        for h in range(H):
            probs.append(dict(
                r=d * H + h, h=h, rev=rev, mask=mask, q_ref=q_ref, k_ref=k_ref, v_ref=v_ref, o_ref=o_ref,
                i_row=g8[h:h + 1, :], b_row=cs[H + h:H + h + 1, :],
                b_col=cs_col[:, 2 * d * H + H + h:2 * d * H + H + h + 1]))

    nt = (((1,), (1,)), ((), ()))
    for p in probs:
        h = p["h"]
        p["q"] = (p["q_ref"][:, h * MLSTM_DK:(h + 1) * MLSTM_DK] * (MLSTM_DK ** -0.5)).astype(BF16)
        p["k"] = p["k_ref"][:, h * MLSTM_DK:(h + 1) * MLSTM_DK]
        p["c_prev"] = c_scr[p["r"]]
        p["qk"] = lax.dot_general(p["q"], p["k"].astype(BF16), nt, preferred_element_type=F32)
        p["qc"] = jnp.dot(p["q"], p["c_prev"].astype(BF16), preferred_element_type=F32)
    for p in probs:
        p["m_prev"] = m_scr[p["r"]:p["r"] + 1, 0:1]
        b_rep = jnp.broadcast_to(p["b_col"], (CHUNK, CHUNK))
        p["log_d"] = jnp.where(p["mask"], b_rep + (p["i_row"] - p["b_row"]), -jnp.inf)
        p["log_inter"] = b_rep + p["m_prev"]
        m_intra = jnp.broadcast_to(jnp.max(p["log_d"], axis=1, keepdims=True), (CHUNK, CHUNK))
        p["m_t"] = jnp.maximum(p["log_inter"], m_intra)
    yield
    for p in probs:
        h = p["h"]
        p["k_t"] = jnp.transpose(p["k"])
        p["v_ext"] = jnp.concatenate([p["v_ref"][:, h * dv:(h + 1) * dv].astype(BF16), ones], axis=1)
    yield
    for p in probs:
        r = p["r"]
        b_row = p["b_row"]
        b_last = b_row[:, 0:1] if p["rev"] else b_row[:, CHUNK - 1:CHUNK]
        log_w = b_last - b_row + p["i_row"]
        m_loc = jnp.max(log_w, axis=1, keepdims=True)
        kw_t = (p["k_t"] * jnp.exp(log_w - m_loc)).astype(BF16)
        u = jnp.dot(kw_t, p["v_ext"], preferred_element_type=F32)
        m_new = jnp.maximum(b_last + p["m_prev"], m_loc)
        c_scr[r] = jnp.exp(b_last + p["m_prev"] - m_new) * p["c_prev"] + jnp.exp(m_loc - m_new) * u
        m_scr[r:r + 1, :] = jnp.broadcast_to(m_new, (1, LANE))
    yield
    for p in probs:
        s = (p["qk"] * jnp.exp(p["log_d"] - p["m_t"])).astype(BF16)
        p["sv"] = jnp.dot(s, p["v_ext"], preferred_element_type=F32)
    yield
    for p in probs:
        h = p["h"]
        inter = jnp.exp(p["log_inter"] - p["m_t"])
        floor = jnp.exp(-p["m_t"])
        den = p["sv"][:, dv:] + p["qc"][:, dv:] * inter
        denom = jnp.maximum(jnp.abs(den), floor)
        for t in range(dv // LANE):
            ts = slice(t * LANE, (t + 1) * LANE)
            p["o_ref"][:, h * dv + t * LANE:h * dv + (t + 1) * LANE] = (
                (p["sv"][:, ts] + p["qc"][:, ts] * inter) / denom)


N_RET_IN = 14
N_MLSTM_IN = 8


def _mixer_ab_kernel(dl_ref, gb_ref, *refs):
    ret_in, refs = refs[:N_RET_IN], refs[N_RET_IN:]
    ml_in, refs = refs[:N_MLSTM_IN], refs[N_MLSTM_IN:]
    ret_f_ref, ret_b_ref, ml_f_ref, ml_b_ref, s_scr, c_scr, m_scr = refs

    @pl.when(pl.program_id(1) == 0)
    def _():
        s_scr[...] = jnp.zeros_like(s_scr)
        c_scr[...] = jnp.zeros_like(c_scr)
        m_scr[...] = jnp.full_like(m_scr, NEG_BIG)

    pending = [_mlstm_body(gb_ref, *ml_in, ml_f_ref, ml_b_ref, c_scr, m_scr),
               _ret_body(dl_ref, *ret_in, ret_f_ref, ret_b_ref, s_scr)]
    done = object()
    while pending:
        pending = [g for g in pending if next(g, done) is not done]


def _mixer_ab(p, p_gates, decay_logit, gate_b, cos_t, sin_t, rows):
    n = p.shape[0]
    B = rows.B
    H = MLSTM_HEADS
    n_steps, seq_f, seq_b, unit = _chunk_maps(rows)
    rq, rv = RET_HEADS * RET_DK, RET_HEADS * RET_DV
    mq, mv = H * MLSTM_DK, H * MLSTM_DV
    dl = jnp.broadcast_to(decay_logit.astype(F32).reshape(2 * RET_HEADS, 1), (2 * RET_HEADS, LANE))
    gb = jnp.zeros((1, LANE), F32).at[0, :4 * H].set(gate_b.astype(F32).reshape(4 * H))

    def chunk(width, col0, seq):
        return pl.BlockSpec((CHUNK, width), lambda b, j: (unit(b, seq(j)), col0 // width))

    def ret_specs(seq):
        return [chunk(rq, AB_QR, seq), chunk(rq, AB_KR, seq), chunk(rv, AB_VR, seq),
                pl.BlockSpec((CHUNK, RET_DK), lambda b, j: (seq(j), 0)),
                pl.BlockSpec((CHUNK, RET_DK), lambda b, j: (seq(j), 0)),
                pl.BlockSpec((RET_DK, CHUNK), lambda b, j: (0, seq(j))),
                pl.BlockSpec((RET_DK, CHUNK), lambda b, j: (0, seq(j)))]

    def ml_specs(seq):
        return [chunk(mq, AB_QM, seq), chunk(mq, AB_KM, seq), chunk(mv, AB_VM, seq), chunk(LANE, 0, seq)]

    const = lambda r: pl.BlockSpec((r, LANE), lambda b, j: (0, 0))
    ret_args = (p, p, p, cos_t, sin_t, cos_t.T, sin_t.T)
    return pl.pallas_call(
        _mixer_ab_kernel,
        grid=(B, n_steps),
        in_specs=([const(2 * RET_HEADS), const(1)] + ret_specs(seq_f) + ret_specs(seq_b)
                  + ml_specs(seq_f) + ml_specs(seq_b)),
        out_specs=[chunk(rv, 0, seq_f), chunk(rv, 0, seq_b), chunk(mv, 0, seq_f), chunk(mv, 0, seq_b)],
        out_shape=[jax.ShapeDtypeStruct((n, rv), F32)] * 2 + [jax.ShapeDtypeStruct((n, mv), F32)] * 2,
        scratch_shapes=[pltpu.VMEM((2 * RET_HEADS, RET_DK, RET_DV), F32),
                        pltpu.VMEM((2 * H, MLSTM_DK, MLSTM_DV + LANE), F32),
                        pltpu.VMEM((2 * SUBLANE, LANE), F32)],
        compiler_params=_params(("arbitrary", "arbitrary")),
        name="mixer_ab",
    )(dl, gb, *ret_args, *ret_args, p, p, p, p_gates, p, p, p, p_gates)


def _head_ln(y, g, heads, width):
    outs = []
    for h in range(heads):
        yh = y[:, h * width:(h + 1) * width]
        mu = jnp.mean(yh, axis=-1, keepdims=True)
        yc = yh - mu
        var = jnp.mean(yc * yc, axis=-1, keepdims=True)
        outs.append(yc * lax.rsqrt(var + EPS))
    return jnp.concatenate(outs, axis=1) * g


def _merge_ab_kernel(x_ref, mod_ref, rf_ref, rb_ref, mf_ref, mb_ref, gr_ref, gm_ref, rg_ref, mg_ref,
                     w1_ref, w2_ref, o_ref):
    gr = gr_ref[...]
    ret_y = (gr * _sigmoid(gr)) * _head_ln(rf_ref[...] + rb_ref[...], rg_ref[...], RET_HEADS, RET_DV)
    ml_y = _sigmoid(gm_ref[...]) * _head_ln(mf_ref[...] + mb_ref[...], mg_ref[...], MLSTM_HEADS, MLSTM_DV)
    y = (jnp.dot(ret_y.astype(BF16), w1_ref[...], preferred_element_type=F32)
         + jnp.dot(ml_y.astype(BF16), w2_ref[...], preferred_element_type=F32))
    o_ref[...] = x_ref[...] + mod_ref[5:6, :] * y


def _merge_ab(x, mods, ret_f, ret_b, ml_f, ml_b, p, ret_g, ml_g, w_out, rows, *, tm):
    n, D = x.shape
    rw = RET_HEADS * RET_DV
    mw = MLSTM_HEADS * MLSTM_DV
    row = lambda i: (i, 0)
    return pl.pallas_call(
        _merge_ab_kernel,
        grid=(n // tm,),
        in_specs=[pl.BlockSpec((tm, D), row),
                  pl.BlockSpec((None, N_MOD, D), lambda i: (rows.mod_row(i, tm, True), 0, 0)),
                  pl.BlockSpec((tm, rw), row), pl.BlockSpec((tm, rw), row),
                  pl.BlockSpec((tm, mw), row), pl.BlockSpec((tm, mw), row),
                  pl.BlockSpec((tm, rw), lambda i: (i, AB_GR // rw)),
                  pl.BlockSpec((tm, mw), lambda i: (i, AB_GM // mw)),
                  pl.BlockSpec((1, rw), lambda i: (0, 0)),
                  pl.BlockSpec((1, mw), lambda i: (0, 0)),
                  pl.BlockSpec((rw, D), lambda i: (0, 0)),
                  pl.BlockSpec((mw, D), lambda i: (rw // mw, 0))],
        out_specs=pl.BlockSpec((tm, D), row),
        out_shape=jax.ShapeDtypeStruct((n, D), F32),
        compiler_params=_params(("arbitrary",)),
        name="merge_ab",
    )(x, mods, ret_f, ret_b, ml_f, ml_b, p, p, ret_g.reshape(1, rw), ml_g.reshape(1, mw), w_out, w_out)


def _lru_coef_kernel(x_ref, xp_ref, xn_ref, cw_ref, cb_ref, wa_ref, wx_ref, ba_ref, bx_ref, lam_ref,
                     a_ref, b_ref, *, tm, ctx_tiles, ctx_seg, lat_seg):
    i = pl.program_id(0)
    seg_pos = jnp.where(i < ctx_tiles, i % ctx_seg, (i - ctx_tiles) % lat_seg)
    seg_len = jnp.where(i < ctx_tiles, ctx_seg, lat_seg)
    keep_prev = (seg_pos != 0).astype(F32)
    keep_next = (seg_pos != seg_len - 1).astype(F32)
    xe = jnp.concatenate([xp_ref[...] * keep_prev, x_ref[...], xn_ref[...] * keep_next], axis=0)
    ne = tm + 2 * SUBLANE
    xc = cb_ref[...] + cw_ref[2:3, :] * x_ref[...]
    for tap, off in ((0, -2), (1, -1), (3, 1)):
        shifted = pltpu.roll(xe, (-off) % ne, axis=0)[SUBLANE:SUBLANE + tm, :]
        xc = xc + cw_ref[tap:tap + 1, :] * shifted
    for d in range(2):
        lam = lam_ref[d:d + 1, :]
        sp = jnp.maximum(-lam, 0.0) + jnp.log(1.0 + jnp.exp(-jnp.abs(lam)))
        half_rate = (-0.5 * LRU_C * LOG2E) * sp
        for g in range(LRU_BLOCKS):
            sl = slice(g * LRU_BLOCK, (g + 1) * LRU_BLOCK)
            xg = xc[:, sl]
            xg16 = xg.astype(BF16)
            t_r = jnp.tanh(jnp.dot(xg16, wa_ref[d, g], preferred_element_type=F32) + ba_ref[d:d + 1, sl])
            t_i = jnp.tanh(jnp.dot(xg16, wx_ref[d, g], preferred_element_type=F32) + bx_ref[d:d + 1, sl])
            a = jnp.exp2(t_r * half_rate[:, sl] + half_rate[:, sl])
            half_x = 0.5 * xg
            a_ref[d, :, sl] = a
            y = 1.0 - a * a
            root = jnp.where(y > 0.0, y * lax.rsqrt(y), 0.0)
            b_ref[d, :, sl] = root * (t_i * half_x + half_x)


def _lru_coef(p, conv_w, conv_b, wa, wx, ba, bx, lam, rows, *, tm):
    n = p.shape[0]
    W = LRU_WIDTH
    tpb = tm // SUBLANE
    n8 = n // SUBLANE
    kern = functools.partial(_lru_coef_kernel, tm=tm, ctx_tiles=rows.n_ctx // tm,
                             ctx_seg=rows.Lc // tm, lat_seg=rows.S // tm)
    full = lambda *s: pl.BlockSpec(s, lambda i: (0,) * len(s))
    return pl.pallas_call(
        kern,
        grid=(n // tm,),
        in_specs=[pl.BlockSpec((tm, W), lambda i: (i, CD_XB // W)),
                  pl.BlockSpec((SUBLANE, W), lambda i: (jnp.maximum(i * tpb - 1, 0), CD_XB // W)),
                  pl.BlockSpec((SUBLANE, W), lambda i: (jnp.minimum((i + 1) * tpb, n8 - 1), CD_XB // W)),
                  full(4, W), full(1, W), full(2, LRU_BLOCKS, LRU_BLOCK, LRU_BLOCK),
                  full(2, LRU_BLOCKS, LRU_BLOCK, LRU_BLOCK), full(2, W), full(2, W), full(2, W)],
        out_specs=[pl.BlockSpec((2, tm, W), lambda i: (0, i, 0))] * 2,
        out_shape=[jax.ShapeDtypeStruct((2, n, W), F32)] * 2,
        compiler_params=_params(("arbitrary",)),
        name="lru_coef",
    )(p, p, p, conv_w, conv_b.reshape(1, W), wa, wx, ba, bx, lam)


def _lru_scan_kernel(af_ref, bf_ref, ab_ref, bb_ref, of_ref, ob_ref, h_scr, *, tb, lw):
    @pl.when(pl.program_id(1) == 0)
    def _():
        h_scr[...] = jnp.zeros_like(h_scr)

    row = lax.broadcasted_iota(jnp.int32, (SUBLANE, lw), 0)
    ng = tb // SUBLANE

    def scan8(a, b, rev):
        d = 1
        while d < SUBLANE:
            if rev:
                keep = row < SUBLANE - d
                sh = SUBLANE - d
            else:
                keep = row >= d
                sh = d
            b = a * jnp.where(keep, pltpu.roll(b, sh, axis=0), 0.0) + b
            a = a * jnp.where(keep, pltpu.roll(a, sh, axis=0), 1.0)
            d *= 2
        return a, b

    for c in range(LRU_WIDTH // lw):
        cs = slice(c * lw, (c + 1) * lw)

        def body(g, carry):
            hf, hb = carry
            rf = pl.multiple_of(g * SUBLANE, SUBLANE)
            a, b = scan8(af_ref[pl.ds(rf, SUBLANE), cs], bf_ref[pl.ds(rf, SUBLANE), cs], False)
            out = a * hf + b
            of_ref[pl.ds(rf, SUBLANE), cs] = out
            hf = jnp.broadcast_to(out[SUBLANE - 1:SUBLANE, :], (SUBLANE, lw))
            rb = pl.multiple_of((ng - 1 - g) * SUBLANE, SUBLANE)
            a, b = scan8(ab_ref[pl.ds(rb, SUBLANE), cs], bb_ref[pl.ds(rb, SUBLANE), cs], True)
            out = a * hb + b
            ob_ref[pl.ds(rb, SUBLANE), cs] = out
            hb = jnp.broadcast_to(out[0:1, :], (SUBLANE, lw))
            return hf, hb

        hf, hb = lax.fori_loop(0, ng, body, (h_scr[0, :, cs], h_scr[1, :, cs]), unroll=4)
        h_scr[0, :, cs] = hf
        h_scr[1, :, cs] = hb


def _lru_scan(a, b, rows):
    n = a.shape[1]
    W = LRU_WIDTH
    B = rows.B
    tb = rows.Lc
    nlb = rows.S // tb

    def blk_f(b_, j):
        return jnp.where(j == 0, b_, B + b_ * nlb + (j - 1))

    def blk_b(b_, j):
        return jnp.where(j == 0, b_, B + b_ * nlb + (nlb - j))

    def spec(d, blk):
        return pl.BlockSpec((None, tb, W), lambda b_, j: (d, blk(b_, j), 0))

    kern = functools.partial(_lru_scan_kernel, tb=tb, lw=512)
    return pl.pallas_call(
        kern,
        grid=(B, 1 + nlb),
        in_specs=[spec(0, blk_f), spec(0, blk_f), spec(1, blk_b), spec(1, blk_b)],
        out_specs=[pl.BlockSpec((tb, W), lambda b_, j: (blk_f(b_, j), 0)),
                   pl.BlockSpec((tb, W), lambda b_, j: (blk_b(b_, j), 0))],
        out_shape=[jax.ShapeDtypeStruct((n, W), F32)] * 2,
        scratch_shapes=[pltpu.VMEM((2, SUBLANE, W), F32)],
        compiler_params=_params(("arbitrary", "arbitrary")),
        name="lru_scan",
    )(a, b, a, b)


def _mla_qkv_kernel(cq_ref, ckv_ref, kr_ref, qg_ref, kvg_ref, wq_ref, wk_ref, wv_ref, qkg_ref, cos_ref, sin_ref,
                    q_ref, k_ref, v_ref):
    cqn = _rms(cq_ref[...], qg_ref[...]).astype(BF16)
    ckvn = _rms(ckv_ref[...], kvg_ref[...]).astype(BF16)
    q_all = jnp.dot(cqn, wq_ref[...], preferred_element_type=F32)
    kn_all = jnp.dot(ckvn, wk_ref[...], preferred_element_type=F32)
    v_all = jnp.dot(ckvn, wv_ref[...], preferred_element_type=F32).astype(BF16)
    ones = jnp.ones((v_all.shape[0], LANE), BF16)
    for h in range(MLA_HEADS):
        v_ref[:, h * (MLA_V + LANE):(h + 1) * (MLA_V + LANE)] = jnp.concatenate(
            [v_all[:, h * MLA_V:(h + 1) * MLA_V], ones], axis=1)
    kr = kr_ref[...]
    cos = cos_ref[...]
    sin = sin_ref[...]
    lane = lax.broadcasted_iota(jnp.int32, cos.shape, 1)
    first_half = (lane % (MLA_ROPE // 2)) < (MLA_ROPE // 4)
    hi_w = MLA_PAD - MLA_NOPE

    def rope(hi):
        rot = jnp.where(first_half, pltpu.roll(hi, hi_w - MLA_ROPE // 4, axis=1),
                        pltpu.roll(hi, MLA_ROPE // 4, axis=1))
        return hi * cos + rot * sin

    def inv_rms(lo, hi):
        return lax.rsqrt(jnp.sum(lo * lo + hi * hi, axis=-1, keepdims=True) * (1.0 / MLA_QK) + EPS)

    g_q, g_k = qkg_ref[0:1, :], qkg_ref[1:2, :]
    kr_rot = rope(kr * g_k[:, MLA_NOPE:])
    for h in range(MLA_HEADS):
        lo_sl = slice(h * MLA_PAD, h * MLA_PAD + MLA_NOPE)
        hi_sl = slice(h * MLA_PAD + MLA_NOPE, (h + 1) * MLA_PAD)
        q_lo, q_hi = q_all[:, lo_sl], q_all[:, hi_sl]
        r = inv_rms(q_lo, q_hi) * (MLA_QK ** -0.5 * LOG2E)
        q_ref[:, lo_sl] = (q_lo * r * g_q[:, :MLA_NOPE]).astype(BF16)
        q_ref[:, hi_sl] = rope(q_hi * r * g_q[:, MLA_NOPE:]).astype(BF16)
        k_lo = kn_all[:, h * MLA_NOPE:(h + 1) * MLA_NOPE]
        r = inv_rms(k_lo, kr)
        k_ref[:, lo_sl] = (k_lo * r * g_k[:, :MLA_NOPE]).astype(BF16)
        k_ref[:, hi_sl] = (kr_rot * r).astype(BF16)


def _mla_qkv(p, q_norm_g, kv_norm_g, wq, wk, wv, qk_g, cos_t, sin_t, rows, *, tm):
    B, Lc, S = rows.B, rows.Lc, rows.S
    n = p.shape[0]
    nct = rows.n_ctx // tm
    cpb = Lc // tm
    lpb = S // tm

    def seq_blk(i):
        il = jnp.maximum(i - nct, 0)
        return (jnp.where(i < nct, i // cpb, il // lpb), jnp.where(i < nct, lpb + i % cpb, il % lpb), 0)

    def pos_blk(i):
        return (jnp.where(i < nct, i % cpb, cpb + jnp.maximum(i - nct, 0) % lpb), 0)

    full = lambda *s: pl.BlockSpec(s, lambda i: (0,) * len(s))
    qkw = MLA_HEADS * MLA_PAD
    vw = MLA_HEADS * MLA_V
    return pl.pallas_call(
        _mla_qkv_kernel,
        grid=(n // tm,),
        in_specs=[pl.BlockSpec((tm, MLA_Q_RANK), lambda i: (i, CD_CQ // MLA_Q_RANK)),
                  pl.BlockSpec((tm, MLA_KV_RANK), lambda i: (i, CD_CKV // MLA_KV_RANK)),
                  pl.BlockSpec((tm, LANE), lambda i: (i, CD_KR // LANE)),
                  full(1, MLA_Q_RANK), full(1, MLA_KV_RANK), full(MLA_Q_RANK, qkw),
                  full(MLA_KV_RANK, MLA_HEADS * MLA_NOPE), full(MLA_KV_RANK, vw), full(2, MLA_PAD),
                  pl.BlockSpec((tm, MLA_PAD - MLA_NOPE), pos_blk),
                  pl.BlockSpec((tm, MLA_PAD - MLA_NOPE), pos_blk)],
        out_specs=[pl.BlockSpec((None, tm, qkw), seq_blk), pl.BlockSpec((None, tm, qkw), seq_blk),
                   pl.BlockSpec((None, tm, vw + MLA_HEADS * LANE), seq_blk)],
        out_shape=[jax.ShapeDtypeStruct((B, Lc + S, qkw), BF16), jax.ShapeDtypeStruct((B, Lc + S, qkw), BF16),
                   jax.ShapeDtypeStruct((B, Lc + S, vw + MLA_HEADS * LANE), BF16)],
        compiler_params=_params(("arbitrary",)),
        name="mla_qkv",
    )(p, p, p, q_norm_g.reshape(1, -1), kv_norm_g.reshape(1, -1), wq, wk, wv, qk_g, cos_t, sin_t)


def _attn_kernel(q_ref, k_ref, v_ref, o_ref, *, sub):
    nt = (((1,), (1,)), ((), ()))
    groups = list(range(0, q_ref.shape[0], sub))

    def scores(r):
        return lax.dot_general(q_ref[r:r + sub, :], k_ref[...], nt, preferred_element_type=F32)

    s_next = scores(groups[0])
    for g, r in enumerate(groups):
        s = s_next
        if g + 1 < len(groups):
            s_next = scores(groups[g + 1])
        e = jnp.exp2(s - jnp.max(s, axis=-1, keepdims=True)).astype(BF16)
        ov = jnp.dot(e, v_ref[...], preferred_element_type=F32)
        o_ref[r:r + sub, :] = (ov[:, :MLA_V] / ov[:, MLA_V:MLA_V + 1]).astype(BF16)


def _attention(q, k, v, rows, *, tq):
    B, Lc, S = rows.B, rows.Lc, rows.S
    nq = S // tq
    return pl.pallas_call(
        functools.partial(_attn_kernel, sub=min(tq, 256)),
        grid=(B, MLA_HEADS, nq),
        in_specs=[pl.BlockSpec((None, tq, MLA_PAD), lambda b, h, i: (b, i, h)),
                  pl.BlockSpec((None, Lc + S, MLA_PAD), lambda b, h, i: (b, 0, h)),
                  pl.BlockSpec((None, Lc + S, MLA_V + LANE), lambda b, h, i: (b, 0, h))],
        out_specs=pl.BlockSpec((tq, MLA_V), lambda b, h, i: (b * nq + i, h)),
        out_shape=jax.ShapeDtypeStruct((B * S, MLA_HEADS * MLA_V), BF16),
        compiler_params=_params(("arbitrary", "arbitrary", "arbitrary")),
        name="attention",
    )(q, k, v)


def _gelu_tanh(x):
    return 0.5 * x * (1.0 + jnp.tanh(math.sqrt(2.0 / math.pi) * (x + 0.044715 * (x * x * x))))


def _merge_cd_kernel(x_ref, mod_ref, yb_ref, hf_ref, hb_ref, att_ref, w1_ref, w2_ref, o_ref):
    y1 = _gelu_tanh(yb_ref[...]) * (hf_ref[...] + hb_ref[...])
    y = (jnp.dot(y1.astype(BF16), w1_ref[...], preferred_element_type=F32)
         + jnp.dot(att_ref[...], w2_ref[...], preferred_element_type=F32))
    o_ref[...] = x_ref[...] + mod_ref[5:6, :] * y


def _merge_cd(x, mods, p, h_f, h_b, att, w_out, rows, *, tm):
    D = x.shape[1]
    W = LRU_WIDTH
    aw = MLA_HEADS * MLA_V
    off = rows.n_ctx // tm
    lat = lambda i: (i + off, 0)
    return pl.pallas_call(
        _merge_cd_kernel,
        grid=(rows.n_lat // tm,),
        in_specs=[pl.BlockSpec((tm, D), lat),
                  pl.BlockSpec((None, N_MOD, D), lambda i: (rows.mod_row(i, tm, False), 0, 0)),
                  pl.BlockSpec((tm, W), lambda i: (i + off, CD_YB // W)),
                  pl.BlockSpec((tm, W), lat), pl.BlockSpec((tm, W), lat),
                  pl.BlockSpec((tm, aw), lambda i: (i, 0)),
                  pl.BlockSpec((W, D), lambda i: (0, 0)),
                  pl.BlockSpec((aw, D), lambda i: (W // aw, 0))],
        out_specs=pl.BlockSpec((tm, D), lambda i: (i, 0)),
        out_shape=jax.ShapeDtypeStruct((rows.n_lat, D), F32),
        compiler_params=_params(("arbitrary",)),
        name="merge_cd",
    )(x, mods, p, h_f, h_b, att, w_out, w_out)


def _ret_tables(rows):
    half = RET_DK // 2
    freqs = ROPE_BASE ** (-jnp.arange(half, dtype=F32) / half)
    ang = jnp.arange(rows.S, dtype=F32)[:, None] * freqs
    cos = jnp.concatenate([jnp.cos(ang), jnp.cos(ang)], axis=-1)
    sin = jnp.concatenate([-jnp.sin(ang), jnp.sin(ang)], axis=-1)
    cos = jnp.concatenate([jnp.ones((rows.Lc, RET_DK), F32), cos], axis=0)
    sin = jnp.concatenate([jnp.zeros((rows.Lc, RET_DK), F32), sin], axis=0)
    return cos, sin


def _mla_tables(rows):
    S = rows.S
    quarter = MLA_ROPE // 4
    freqs = ROPE_BASE ** (-jnp.arange(quarter, dtype=F32) / quarter)
    t = jnp.arange(S)
    row = (t // GRID_W).astype(F32)
    col = (t % GRID_W).astype(F32)

    def part(pos):
        ang = pos[:, None] * freqs
        return (jnp.concatenate([jnp.cos(ang), jnp.cos(ang)], axis=-1),
                jnp.concatenate([-jnp.sin(ang), jnp.sin(ang)], axis=-1))

    rc, rs = part(row)
    cc, cs = part(col)
    tail = MLA_PAD - MLA_QK
    cos = jnp.concatenate([rc, cc, jnp.ones((S, tail), F32)], axis=-1)
    sin = jnp.concatenate([rs, cs, jnp.zeros((S, tail), F32)], axis=-1)
    cos = jnp.concatenate([jnp.ones((rows.Lc, MLA_ROPE + tail), F32), cos], axis=0)
    sin = jnp.concatenate([jnp.zeros((rows.Lc, MLA_ROPE + tail), F32), sin], axis=0)
    return cos, sin


def _pad_cols(w, n):
    return jnp.pad(w, ((0, 0), (0, n - w.shape[1])))


def _head_pad(w, heads, width, padded):
    K = w.shape[0]
    return jnp.pad(w.reshape(K, heads, width), ((0, 0), (0, 0), (0, padded - width))).reshape(K, heads * padded)


def kernel(x, c, ctx, c_ctx, ada_w, ada_b, norm_g, ffn_wg, ffn_wu, ffn_wd, ab_w_in, ab_w_out, ret_decay_logit, ret_gn_g, mlstm_gate_b, mlstm_gn_g, cd_w_in, cd_w_out, lru_conv_w, lru_conv_b, lru_wa, lru_ba, lru_wx, lru_bx, lru_lambda, mla_q_norm_g, mla_kv_norm_g, mla_w_uq, mla_w_uk, mla_w_uv, mla_qk_norm_g):
    B, S, D = x.shape
    Lc = ctx.shape[1]
    depth = ada_w.shape[0]
    F = ffn_wg.shape[-1]
    rows = Rows(B, Lc, S)
    assert B < SUBLANE and Lc % CHUNK == 0 and S % Lc == 0 and S % GRID_W == 0

    tm = min(1024, rows.n_ctx)
    tf = 512
    all_mod = lambda i: rows.mod_row(i, tm, True)
    lat_mod = lambda i: rows.mod_row(i, tm, False)

    cond = jnp.zeros((SUBLANE, D), F32).at[:B].set(c.astype(F32)).at[B].set(c_ctx.astype(F32))
    mods = _adaln(cond, ada_w, ada_b)

    ffn_stacks = (ffn_wg, ffn_wu, ffn_wd)
    w_cur = tuple(w[0, 0].astype(BF16) for w in ffn_stacks)

    def ffn_casts(l, h):
        return [(w, (l, h)) for w in ffn_stacks] if l < depth else []

    for l in range(depth):
        last = l == depth - 1
        j = l // 2
        m_l = mods[l]
        ffn1 = functools.partial(_ffn, mods=m_l, g=norm_g[l, 0], wg=w_cur[0], wu=w_cur[1], wd=w_cur[2],
                                 mod_base=0, tm=tm, tf=tf)
        casts = ffn_casts(l, 1)
        if l == 0:
            xa, w_next = ffn1(x.reshape(B * S, D).astype(F32), mod_of_tile=lat_mod, out_rows=rows.n_all,
                              out_tile0=rows.n_ctx // tm, casts=casts)
            xa = ffn1(ctx.reshape(B * Lc, D).astype(F32), mod_of_tile=lambda i: B, out_rows=rows.n_all, into=xa)
        else:
            xa, w_next = ffn1(xa, mod_of_tile=all_mod, casts=casts)
        w_cur = w_next

        if l % 2 == 0:
            w_in, w_out = ab_w_in[j].astype(BF16), ab_w_out[j].astype(BF16)
            p, p_gates = _proj(xa, m_l, norm_g[l, 1], w_in, rows, tm=tm, tn=1024, n_cols=AB_GATES,
                               w_tail=_pad_cols(w_in[:, AB_GATES:], LANE))
            cos_t, sin_t = _ret_tables(rows)
            ret_f, ret_b, ml_f, ml_b = _mixer_ab(p, p_gates, ret_decay_logit[j], mlstm_gate_b[j], cos_t, sin_t,
                                                 rows)
            xa = _merge_ab(xa, m_l, ret_f, ret_b, ml_f, ml_b, p, ret_gn_g[j], mlstm_gn_g[j], w_out, rows,
                           tm=min(256, tm))
            if last:
                xa = xa[rows.n_ctx:]
        else:
            assert last, "context outputs of the recurrent/attention mixer are not produced"
            w_in, w_out, w_uq, w_uk, w_uv = (w[j].astype(BF16) for w in
                                             (cd_w_in, cd_w_out, mla_w_uq, mla_w_uk, mla_w_uv))
            p, = _proj(xa, m_l, norm_g[l, 1], _pad_cols(w_in, CD_PAD), rows, tm=tm, tn=CD_PAD // 3)
            a_c, b_c = _lru_coef(p, lru_conv_w[j], lru_conv_b[j], (0.5 * lru_wa[j]).astype(BF16),
                                 (0.5 * lru_wx[j]).astype(BF16), 0.5 * lru_ba[j], 0.5 * lru_bx[j], lru_lambda[j],
                                 rows, tm=min(256, Lc))
            h_f, h_b = _lru_scan(a_c, b_c, rows)
            cos_t, sin_t = _mla_tables(rows)
            qk_g = jnp.pad(mla_qk_norm_g[j].astype(F32), ((0, 0), (0, MLA_PAD - MLA_QK)))
            q, k, v = _mla_qkv(p, mla_q_norm_g[j], mla_kv_norm_g[j], _head_pad(w_uq, MLA_HEADS, MLA_QK, MLA_PAD),
                               w_uk, w_uv, qk_g, cos_t, sin_t, rows, tm=min(256, Lc))
            att = _attention(q, k, v, rows, tq=min(2048, S))
            xa = _merge_cd(xa, m_l, p, h_f, h_b, att, w_out, rows, tm=min(256, tm))

        casts = ffn_casts(l + 1, 0)
        res = _ffn(xa, m_l, norm_g[l, 2], *w_cur, mod_base=6, mod_of_tile=lat_mod if last else all_mod,
                   tm=tm, tf=tf, casts=casts)
        xa, w_cur = res if casts else (res, None)

    if xa.shape[0] != rows.n_lat:
        xa = xa[rows.n_ctx:]
    return xa.reshape(B, S, D).astype(x.dtype)
```

```python
import functools
import math

import jax
import jax.numpy as jnp
import numpy as np
from jax import lax
from jax.experimental import pallas as pl
from jax.experimental.pallas import tpu as pltpu

F32 = jnp.float32
BF16 = jnp.bfloat16

N_MOD = 9
EPS = 1e-6
CHUNK = 128
GRID_W = 64
ROPE_BASE = 10000.0
NEG_BIG = -1e30

RET_HEADS = 4
RET_DK = 128
RET_DV = 256
MLSTM_HEADS = 4
MLSTM_DK = 128
MLSTM_DV = 256
LRU_WIDTH = 1024
LRU_BLOCKS = 8
LRU_BLOCK = LRU_WIDTH // LRU_BLOCKS
LRU_C = 8.0
MLA_HEADS = 8
MLA_Q_RANK = 512
MLA_KV_RANK = 256
MLA_NOPE = 128
MLA_ROPE = 64
MLA_V = 128
MLA_QK = MLA_NOPE + MLA_ROPE
LOG2E = math.log2(math.e)
MLA_PAD = 256

AB_SPLITS = (RET_HEADS * RET_DK, RET_HEADS * RET_DK, RET_HEADS * RET_DV, RET_HEADS * RET_DV,
             MLSTM_HEADS * MLSTM_DK, MLSTM_HEADS * MLSTM_DK, MLSTM_HEADS * MLSTM_DV, MLSTM_HEADS * MLSTM_DV,
             4 * MLSTM_HEADS)
AB_IN = sum(AB_SPLITS)
CD_SPLITS = (LRU_WIDTH, LRU_WIDTH, MLA_Q_RANK, MLA_KV_RANK, MLA_ROPE)
CD_IN = sum(CD_SPLITS)

LANE = 128
SUBLANE = 8
BF16_SUBLANES = 16
VMEM_LIMIT = 56 * 1024 * 1024
FFN_VMEM_LIMIT = 60 * 1024 * 1024

AB_QR, AB_KR, AB_VR, AB_GR, AB_QM, AB_KM, AB_VM, AB_GM, AB_GATES = np.cumsum((0,) + AB_SPLITS[:-1]).tolist()
CD_YB, CD_XB, CD_CQ, CD_CKV, CD_KR = np.cumsum((0,) + CD_SPLITS[:-1]).tolist()
CD_PAD = 3072


def _params(sem, vmem=VMEM_LIMIT):
    return pltpu.CompilerParams(dimension_semantics=sem, vmem_limit_bytes=vmem)


def _round_up(n, m):
    return (n + m - 1) // m * m


def _sigmoid(x):
    return 0.5 * jnp.tanh(0.5 * x) + 0.5


def _log_sigmoid(x):
    return jnp.minimum(x, 0.0) - jnp.log(1.0 + jnp.exp(-jnp.abs(x)))


def _rms(x, g):
    return x * lax.rsqrt(jnp.mean(x * x, axis=-1, keepdims=True) + EPS) * g


def _adaln_kernel(c_ref, w_ref, b_ref, o_ref):
    c = c_ref[...]
    s = (c * _sigmoid(c)).astype(BF16)
    o_ref[...] = jnp.dot(s, w_ref[...].astype(BF16), preferred_element_type=F32) + b_ref[...]


def _adaln(cond, ada_w, ada_b):
    L, D, N = ada_w.shape
    tn = 1024 if N % 1024 == 0 else N
    out = pl.pallas_call(
        _adaln_kernel,
        grid=(L, N // tn),
        in_specs=[pl.BlockSpec((SUBLANE, D), lambda l, n: (0, 0)),
                  pl.BlockSpec((None, D, tn), lambda l, n: (l, 0, n)),
                  pl.BlockSpec((None, 1, tn), lambda l, n: (l, 0, n))],
        out_specs=pl.BlockSpec((None, SUBLANE, tn), lambda l, n: (l, 0, n)),
        out_shape=jax.ShapeDtypeStruct((L, SUBLANE, N), F32),
        compiler_params=_params(("arbitrary", "arbitrary")),
        name="adaln",
    )(cond, ada_w, ada_b.reshape(L, 1, N))
    return out.reshape(L, SUBLANE, N_MOD, D)


class Rows:
    def __init__(self, B, Lc, S):
        self.B, self.Lc, self.S = B, Lc, S
        self.n_ctx = B * Lc
        self.n_lat = B * S
        self.n_all = self.n_ctx + self.n_lat

    def mod_row(self, tile, tm, with_ctx):
        if not with_ctx:
            return (tile * tm) // self.S
        nct = self.n_ctx // tm
        return jnp.where(tile < nct, self.B, (jnp.maximum(tile - nct, 0) * tm) // self.S)


def _ffn_kernel(*refs, mod_base, sub, last_cols, has_into, n_cast):
    x_ref, mod_ref, g_ref, wg_ref, wu_ref, wd_ref = refs[:6]
    cast_src = refs[6:6 + n_cast]
    o_ref = refs[6 + n_cast + has_into]
    cast_dst = refs[7 + n_cast + has_into:7 + 2 * n_cast + has_into]
    h_scr = refs[-1]
    f = pl.program_id(1)
    nf = pl.num_programs(1)
    tm = x_ref.shape[0]
    tf = wg_ref.shape[1]

    def step(cols, first, final):
        for src, dst in zip(cast_src, cast_dst):
            dst[...] = src[...].astype(BF16)
        if first:
            shift = mod_ref[mod_base:mod_base + 1, :]
            scale = 1.0 + mod_ref[mod_base + 1:mod_base + 2, :]
        if final:
            gate = 0.5 * mod_ref[mod_base + 2:mod_base + 3, :]
        for r in range(0, tm, sub):
            rs = slice(r, r + sub)
            if first:
                h = (_rms(x_ref[rs, :], g_ref[...]) * scale + shift).astype(BF16)
                h_scr[rs, :] = h
            else:
                h = h_scr[rs, :]
            a = jnp.dot(h, wg_ref[:, :cols], preferred_element_type=F32)
            u = jnp.dot(h, wu_ref[:, :cols], preferred_element_type=F32)
            act = (a * _sigmoid(a) * u).astype(BF16)
            part = jnp.dot(act, wd_ref[:cols, :], preferred_element_type=F32)
            if first:
                o_ref[rs, :] = part
            elif final:
                o_ref[rs, :] = x_ref[rs, :] + gate * (o_ref[rs, :] + part)
            else:
                o_ref[rs, :] += part

    pl.when(f == 0)(functools.partial(step, tf, True, False))
    pl.when(jnp.logical_and(f > 0, f < nf - 1))(functools.partial(step, tf, False, False))
    pl.when(f == nf - 1)(functools.partial(step, last_cols, False, True))


def _ffn(x, mods, g, wg, wu, wd, *, mod_base, mod_of_tile, tm, tf, out_rows=None, out_tile0=0, into=None,
         casts=()):
    n, D = x.shape
    F = wg.shape[-1]
    nf = pl.cdiv(F, tf)
    ni = n // tm
    assert nf >= 2 and n % tm == 0
    out_rows = n if out_rows is None else out_rows
    in_specs = [pl.BlockSpec((tm, D), lambda i, f: (i, 0)),
                pl.BlockSpec((None, N_MOD, D), lambda i, f: (mod_of_tile(i), 0, 0)),
                pl.BlockSpec((1, D), lambda i, f: (0, 0)),
                pl.BlockSpec((D, tf), lambda i, f: (0, f)),
                pl.BlockSpec((D, tf), lambda i, f: (0, f)),
                pl.BlockSpec((tf, D), lambda i, f: (f, 0))]
    args = [x, mods, g.reshape(1, D), wg, wu, wd]
    out_specs = [pl.BlockSpec((tm, D), lambda i, f: (out_tile0 + i, 0))]
    out_shape = [jax.ShapeDtypeStruct((out_rows, D), F32)]
    n_cast = len(casts)
    for w, lead in casts:
        rows_w, cols_w = w.shape[len(lead):]
        slab_rows = next(m for m in range(BF16_SUBLANES * pl.cdiv(rows_w, BF16_SUBLANES * ni * nf),
                                          rows_w + 1, BF16_SUBLANES) if rows_w % m == 0)
        nblk = rows_w // slab_rows
        slab = lambda i, f, nblk=nblk: jnp.minimum(i * nf + f, nblk - 1)
        in_specs.append(pl.BlockSpec((None,) * len(lead) + (slab_rows, cols_w),
                                     lambda i, f, slab=slab, lead=lead: lead + (slab(i, f), 0)))
        out_specs.append(pl.BlockSpec((slab_rows, cols_w), lambda i, f, slab=slab: (slab(i, f), 0)))
        out_shape.append(jax.ShapeDtypeStruct((rows_w, cols_w), BF16))
        args.append(w)
    aliases = {}
    if into is not None:
        assert into.shape == (out_rows, D)
        aliases = {len(args): 0}
        in_specs.append(pl.BlockSpec(memory_space=pl.ANY))
        args.append(into)
    kern = functools.partial(_ffn_kernel, mod_base=mod_base, sub=min(tm, 512), last_cols=F - (nf - 1) * tf,
                             has_into=int(into is not None), n_cast=n_cast)
    res = pl.pallas_call(
        kern,
        grid=(ni, nf),
        in_specs=in_specs,
        out_specs=out_specs,
        out_shape=out_shape,
        scratch_shapes=[pltpu.VMEM((tm, D), BF16)],
        input_output_aliases=aliases,
        compiler_params=_params(("arbitrary", "arbitrary"), FFN_VMEM_LIMIT),
        name="ffn",
    )(*args)
    return (res[0], tuple(res[1:])) if n_cast else res[0]


def _proj_kernel(x_ref, mod_ref, g_ref, w_ref, *rest, sub):
    wt_ref, o_ref, ot_ref, h_scr = rest if len(rest) == 4 else (None, rest[0], None, rest[1])
    tm = x_ref.shape[0]

    @pl.when(pl.program_id(1) == 0)
    def _():
        scale = 1.0 + mod_ref[4:5, :]
        for r in range(0, tm, sub):
            h = (_rms(x_ref[r:r + sub, :], g_ref[...]) * scale + mod_ref[3:4, :]).astype(BF16)
            h_scr[r:r + sub, :] = h
            o_ref[r:r + sub, :] = jnp.dot(h, w_ref[...], preferred_element_type=F32)
            if wt_ref is not None:
                ot_ref[r:r + sub, :] = jnp.dot(h, wt_ref[...], preferred_element_type=F32)

    @pl.when(pl.program_id(1) != 0)
    def _():
        for r in range(0, tm, sub):
            o_ref[r:r + sub, :] = jnp.dot(h_scr[r:r + sub, :], w_ref[...], preferred_element_type=F32)


def _proj(x, mods, g, w, rows, *, tm, tn, n_cols=None, w_tail=None):
    n, D = x.shape
    N = w.shape[1] if n_cols is None else n_cols
    assert N % tn == 0
    in_specs = [pl.BlockSpec((tm, D), lambda i, j: (i, 0)),
                pl.BlockSpec((None, N_MOD, D), lambda i, j: (rows.mod_row(i, tm, True), 0, 0)),
                pl.BlockSpec((1, D), lambda i, j: (0, 0)),
                pl.BlockSpec((D, tn), lambda i, j: (0, j))]
    out_specs = [pl.BlockSpec((tm, tn), lambda i, j: (i, j))]
    out_shape = [jax.ShapeDtypeStruct((n, N), F32)]
    args = [x, mods, g.reshape(1, D), w]
    if w_tail is not None:
        in_specs.append(pl.BlockSpec((D, LANE), lambda i, j: (0, 0)))
        out_specs.append(pl.BlockSpec((tm, LANE), lambda i, j: (i, 0)))
        out_shape.append(jax.ShapeDtypeStruct((n, LANE), F32))
        args.append(w_tail)
    return pl.pallas_call(
        functools.partial(_proj_kernel, sub=min(tm, 512)),
        grid=(n // tm, N // tn),
        in_specs=in_specs,
        out_specs=out_specs,
        out_shape=out_shape,
        scratch_shapes=[pltpu.VMEM((tm, D), BF16)],
        compiler_params=_params(("arbitrary", "arbitrary")),
        name="proj",
    )(*args)


def _chunk_maps(rows):
    B = rows.B
    nc = rows.Lc // CHUNK
    nl = rows.S // CHUNK

    def seq_f(j):
        return j

    def seq_b(j):
        return jnp.where(j < nc, nc - 1 - j, nc + nl - 1 - (j - nc))

    def unit(b, s):
        return jnp.where(s < nc, b * nc + s, B * nc + b * nl + (s - nc))

    return nc + nl, seq_f, seq_b, unit


def _rope128(x, cos, sin):
    return x * cos + pltpu.roll(x, 64, axis=1) * sin


def _tri_masks():
    r = lax.broadcasted_iota(jnp.int32, (CHUNK, CHUNK), 0)
    c = lax.broadcasted_iota(jnp.int32, (CHUNK, CHUNK), 1)
    return r - c


def _ret_body(dl_ref, qf_ref, kf_ref, vf_ref, cf_ref, sf_ref, ctf_ref, stf_ref,
              qb_ref, kb_ref, vb_ref, cb_ref, sb_ref, ctb_ref, stb_ref, of_ref, ob_ref, s_scr):
    lg_all = _log_sigmoid(dl_ref[...])
    rel = _tri_masks().astype(F32)
    pos_c = lax.broadcasted_iota(jnp.int32, (CHUNK, 1), 0).astype(F32)
    pos_r = lax.broadcasted_iota(jnp.int32, (1, CHUNK), 1).astype(F32)
    dirs = ((qf_ref, kf_ref, vf_ref, cf_ref, sf_ref, ctf_ref, stf_ref, of_ref, False),
            (qb_ref, kb_ref, vb_ref, cb_ref, sb_ref, ctb_ref, stb_ref, ob_ref, True))
    probs = []
    for d, (q_ref, k_ref, v_ref, c_ref, sn_ref, ct_ref, st_ref, o_ref, rev) in enumerate(dirs):
        dist = -rel if rev else rel
        step_c = (CHUNK - 1.0 - pos_c) if rev else pos_c
        step_r = (CHUNK - 1.0 - pos_r) if rev else pos_r
        for h in range(RET_HEADS):
            probs.append(dict(r=d * RET_HEADS + h, h=h, q_ref=q_ref, k_ref=k_ref, v_ref=v_ref, c_ref=c_ref,
                              sn_ref=sn_ref, ct_ref=ct_ref, st_ref=st_ref, o_ref=o_ref, dist=dist,
                              step_c=step_c, step_r=step_r))

    for p in probs:
        h, r = p["h"], p["r"]
        p["lg"] = lg_all[r:r + 1, 0:1]
        q = _rope128(p["q_ref"][:, h * RET_DK:(h + 1) * RET_DK], p["c_ref"][...], p["sn_ref"][...]).astype(BF16)
        k_t = jnp.transpose(p["k_ref"][:, h * RET_DK:(h + 1) * RET_DK])
        k_t = k_t * p["ct_ref"][...] + pltpu.roll(k_t, RET_DK // 2, axis=0) * p["st_ref"][...]
        p["k_t"] = k_t * (RET_DK ** -0.5)
        p["v"] = p["v_ref"][:, h * RET_DV:(h + 1) * RET_DV].astype(BF16)
        p["s_prev"] = s_scr[r]
        p["sc"] = jnp.dot(q, p["k_t"].astype(BF16), preferred_element_type=F32)
        p["cross"] = jnp.dot(q, p["s_prev"].astype(BF16), preferred_element_type=F32)
    yield
    for p in probs:
        lg = p["lg"]
        zeta = jnp.exp((CHUNK - 1.0 - p["step_r"]) * lg)
        u = jnp.dot((p["k_t"] * zeta).astype(BF16), p["v"], preferred_element_type=F32)
        s_scr[p["r"]] = jnp.exp(CHUNK * lg) * p["s_prev"] + u
    yield
    for p in probs:
        h, lg, dist = p["h"], p["lg"], p["dist"]
        decay = jnp.where(dist >= 0, jnp.exp(jnp.maximum(dist, 0.0) * lg), 0.0)
        inner = jnp.dot((p["sc"] * decay).astype(BF16), p["v"], preferred_element_type=F32)
        xi = jnp.exp((p["step_c"] + 1.0) * lg)
        p["o_ref"][:, h * RET_DV:(h + 1) * RET_DV] = inner + p["cross"] * xi


def _mlstm_body(gb_ref, qf_ref, kf_ref, vf_ref, gf_ref, qb_ref, kb_ref, vb_ref, gbk_ref,
                of_ref, ob_ref, c_scr, m_scr):
    H = MLSTM_HEADS
    dv = MLSTM_DV
    rel = _tri_masks()
    lower = (rel >= 0).astype(F32)
    upper = (rel <= 0).astype(F32)
    exact = dict(precision=lax.Precision.HIGHEST, preferred_element_type=F32)
    ones = jnp.ones((CHUNK, LANE), BF16)
    dirs = ((qf_ref, kf_ref, vf_ref, gf_ref, of_ref, False),
            (qb_ref, kb_ref, vb_ref, gbk_ref, ob_ref, True))
    probs = []
    for d, (q_ref, k_ref, v_ref, g_ref, o_ref, rev) in enumerate(dirs):
        gates = g_ref[...] + gb_ref[...]
        cs_col = jnp.dot(upper if rev else lower, _log_sigmoid(gates), **exact)
        g8 = jnp.transpose(gates)[2 * d * H:2 * d * H + SUBLANE, :]
        cs = jnp.dot(_log_sigmoid(g8), lower if rev else upper, **exact)
        mask = (rel <= 0) if rev else (rel >= 0)
        for h in range(H):
            probs.append(dict(
                r=d * H + h, h=h, rev=rev, mask=mask, q_ref=q_ref, k_ref=k_ref, v_ref=v_ref, o_ref=o_ref,
                i_row=g8[h:h + 1, :], b_row=cs[H + h:H + h + 1, :],
                b_col=cs_col[:, 2 * d * H + H + h:2 * d * H + H + h + 1]))

    nt = (((1,), (1,)), ((), ()))
    for p in probs:
        h = p["h"]
        p["q"] = (p["q_ref"][:, h * MLSTM_DK:(h + 1) * MLSTM_DK] * (MLSTM_DK ** -0.5)).astype(BF16)
        p["k"] = p["k_ref"][:, h * MLSTM_DK:(h + 1) * MLSTM_DK]
        p["c_prev"] = c_scr[p["r"]]
        p["qk"] = lax.dot_general(p["q"], p["k"].astype(BF16), nt, preferred_element_type=F32)
        p["qc"] = jnp.dot(p["q"], p["c_prev"].astype(BF16), preferred_element_type=F32)
    for p in probs:
        p["m_prev"] = m_scr[p["r"]:p["r"] + 1, 0:1]
        b_rep = jnp.broadcast_to(p["b_col"], (CHUNK, CHUNK))
        p["log_d"] = jnp.where(p["mask"], b_rep + (p["i_row"] - p["b_row"]), -jnp.inf)
        p["log_inter"] = b_rep + p["m_prev"]
        m_intra = jnp.broadcast_to(jnp.max(p["log_d"], axis=1, keepdims=True), (CHUNK, CHUNK))
        p["m_t"] = jnp.maximum(p["log_inter"], m_intra)
    yield
    for p in probs:
        h = p["h"]
        p["k_t"] = jnp.transpose(p["k"])
        p["v_ext"] = jnp.concatenate([p["v_ref"][:, h * dv:(h + 1) * dv].astype(BF16), ones], axis=1)
    yield
    for p in probs:
        r = p["r"]
        b_row = p["b_row"]
        b_last = b_row[:, 0:1] if p["rev"] else b_row[:, CHUNK - 1:CHUNK]
        log_w = b_last - b_row + p["i_row"]
        m_loc = jnp.max(log_w, axis=1, keepdims=True)
        kw_t = (p["k_t"] * jnp.exp(log_w - m_loc)).astype(BF16)
        u = jnp.dot(kw_t, p["v_ext"], preferred_element_type=F32)
        m_new = jnp.maximum(b_last + p["m_prev"], m_loc)
        c_scr[r] = jnp.exp(b_last + p["m_prev"] - m_new) * p["c_prev"] + jnp.exp(m_loc - m_new) * u
        m_scr[r:r + 1, :] = jnp.broadcast_to(m_new, (1, LANE))
    yield
    for p in probs:
        s = (p["qk"] * jnp.exp(p["log_d"] - p["m_t"])).astype(BF16)
        p["sv"] = jnp.dot(s, p["v_ext"], preferred_element_type=F32)
    yield
    for p in probs:
        h = p["h"]
        inter = jnp.exp(p["log_inter"] - p["m_t"])
        floor = jnp.exp(-p["m_t"])
        den = p["sv"][:, dv:] + p["qc"][:, dv:] * inter
        denom = jnp.maximum(jnp.abs(den), floor)
        for t in range(dv // LANE):
            ts = slice(t * LANE, (t + 1) * LANE)
            p["o_ref"][:, h * dv + t * LANE:h * dv + (t + 1) * LANE] = (
                (p["sv"][:, ts] + p["qc"][:, ts] * inter) / denom)


N_RET_IN = 14
N_MLSTM_IN = 8


def _mixer_ab_kernel(dl_ref, gb_ref, *refs):
    ret_in, refs = refs[:N_RET_IN], refs[N_RET_IN:]
    ml_in, refs = refs[:N_MLSTM_IN], refs[N_MLSTM_IN:]
    ret_f_ref, ret_b_ref, ml_f_ref, ml_b_ref, s_scr, c_scr, m_scr = refs

    @pl.when(pl.program_id(1) == 0)
    def _():
        s_scr[...] = jnp.zeros_like(s_scr)
        c_scr[...] = jnp.zeros_like(c_scr)
        m_scr[...] = jnp.full_like(m_scr, NEG_BIG)

    pending = [_mlstm_body(gb_ref, *ml_in, ml_f_ref, ml_b_ref, c_scr, m_scr),
               _ret_body(dl_ref, *ret_in, ret_f_ref, ret_b_ref, s_scr)]
    done = object()
    while pending:
        pending = [g for g in pending if next(g, done) is not done]


def _mixer_ab(p, p_gates, decay_logit, gate_b, cos_t, sin_t, rows):
    n = p.shape[0]
    B = rows.B
    H = MLSTM_HEADS
    n_steps, seq_f, seq_b, unit = _chunk_maps(rows)
    rq, rv = RET_HEADS * RET_DK, RET_HEADS * RET_DV
    mq, mv = H * MLSTM_DK, H * MLSTM_DV
    dl = jnp.broadcast_to(decay_logit.astype(F32).reshape(2 * RET_HEADS, 1), (2 * RET_HEADS, LANE))
    gb = jnp.zeros((1, LANE), F32).at[0, :4 * H].set(gate_b.astype(F32).reshape(4 * H))

    def chunk(width, col0, seq):
        return pl.BlockSpec((CHUNK, width), lambda b, j: (unit(b, seq(j)), col0 // width))

    def ret_specs(seq):
        return [chunk(rq, AB_QR, seq), chunk(rq, AB_KR, seq), chunk(rv, AB_VR, seq),
                pl.BlockSpec((CHUNK, RET_DK), lambda b, j: (seq(j), 0)),
                pl.BlockSpec((CHUNK, RET_DK), lambda b, j: (seq(j), 0)),
                pl.BlockSpec((RET_DK, CHUNK), lambda b, j: (0, seq(j))),
                pl.BlockSpec((RET_DK, CHUNK), lambda b, j: (0, seq(j)))]

    def ml_specs(seq):
        return [chunk(mq, AB_QM, seq), chunk(mq, AB_KM, seq), chunk(mv, AB_VM, seq), chunk(LANE, 0, seq)]

    const = lambda r: pl.BlockSpec((r, LANE), lambda b, j: (0, 0))
    ret_args = (p, p, p, cos_t, sin_t, cos_t.T, sin_t.T)
    return pl.pallas_call(
        _mixer_ab_kernel,
        grid=(B, n_steps),
        in_specs=([const(2 * RET_HEADS), const(1)] + ret_specs(seq_f) + ret_specs(seq_b)
                  + ml_specs(seq_f) + ml_specs(seq_b)),
        out_specs=[chunk(rv, 0, seq_f), chunk(rv, 0, seq_b), chunk(mv, 0, seq_f), chunk(mv, 0, seq_b)],
        out_shape=[jax.ShapeDtypeStruct((n, rv), F32)] * 2 + [jax.ShapeDtypeStruct((n, mv), F32)] * 2,
        scratch_shapes=[pltpu.VMEM((2 * RET_HEADS, RET_DK, RET_DV), F32),
                        pltpu.VMEM((2 * H, MLSTM_DK, MLSTM_DV + LANE), F32),
                        pltpu.VMEM((2 * SUBLANE, LANE), F32)],
        compiler_params=_params(("arbitrary", "arbitrary")),
        name="mixer_ab",
    )(dl, gb, *ret_args, *ret_args, p, p, p, p_gates, p, p, p, p_gates)


def _head_ln(y, g, heads, width):
    outs = []
    for h in range(heads):
        yh = y[:, h * width:(h + 1) * width]
        mu = jnp.mean(yh, axis=-1, keepdims=True)
        yc = yh - mu
        var = jnp.mean(yc * yc, axis=-1, keepdims=True)
        outs.append(yc * lax.rsqrt(var + EPS))
    return jnp.concatenate(outs, axis=1) * g


def _merge_ab_kernel(x_ref, mod_ref, rf_ref, rb_ref, mf_ref, mb_ref, gr_ref, gm_ref, rg_ref, mg_ref,
                     w1_ref, w2_ref, o_ref):
    gr = gr_ref[...]
    ret_y = (gr * _sigmoid(gr)) * _head_ln(rf_ref[...] + rb_ref[...], rg_ref[...], RET_HEADS, RET_DV)
    ml_y = _sigmoid(gm_ref[...]) * _head_ln(mf_ref[...] + mb_ref[...], mg_ref[...], MLSTM_HEADS, MLSTM_DV)
    y = (jnp.dot(ret_y.astype(BF16), w1_ref[...], preferred_element_type=F32)
         + jnp.dot(ml_y.astype(BF16), w2_ref[...], preferred_element_type=F32))
    o_ref[...] = x_ref[...] + mod_ref[5:6, :] * y


def _merge_ab(x, mods, ret_f, ret_b, ml_f, ml_b, p, ret_g, ml_g, w_out, rows, *, tm):
    n, D = x.shape
    rw = RET_HEADS * RET_DV
    mw = MLSTM_HEADS * MLSTM_DV
    row = lambda i: (i, 0)
    return pl.pallas_call(
        _merge_ab_kernel,
        grid=(n // tm,),
        in_specs=[pl.BlockSpec((tm, D), row),
                  pl.BlockSpec((None, N_MOD, D), lambda i: (rows.mod_row(i, tm, True), 0, 0)),
                  pl.BlockSpec((tm, rw), row), pl.BlockSpec((tm, rw), row),
                  pl.BlockSpec((tm, mw), row), pl.BlockSpec((tm, mw), row),
                  pl.BlockSpec((tm, rw), lambda i: (i, AB_GR // rw)),
                  pl.BlockSpec((tm, mw), lambda i: (i, AB_GM // mw)),
                  pl.BlockSpec((1, rw), lambda i: (0, 0)),
                  pl.BlockSpec((1, mw), lambda i: (0, 0)),
                  pl.BlockSpec((rw, D), lambda i: (0, 0), pipeline_mode=pl.Buffered(1)),
                  pl.BlockSpec((mw, D), lambda i: (rw // mw, 0), pipeline_mode=pl.Buffered(1))],
        out_specs=pl.BlockSpec((tm, D), row),
        out_shape=jax.ShapeDtypeStruct((n, D), F32),
        compiler_params=_params(("arbitrary",)),
        name="merge_ab",
    )(x, mods, ret_f, ret_b, ml_f, ml_b, p, p, ret_g.reshape(1, rw), ml_g.reshape(1, mw), w_out, w_out)


def _lru_coef_kernel(x_ref, xp_ref, xn_ref, cw_ref, cb_ref, wa_ref, wx_ref, ba_ref, bx_ref, lam_ref,
                     a_ref, b_ref, *, tm, ctx_tiles, ctx_seg, lat_seg):
    i = pl.program_id(0)
    seg_pos = jnp.where(i < ctx_tiles, i % ctx_seg, (i - ctx_tiles) % lat_seg)
    seg_len = jnp.where(i < ctx_tiles, ctx_seg, lat_seg)
    keep_prev = (seg_pos != 0).astype(F32)
    keep_next = (seg_pos != seg_len - 1).astype(F32)
    xe = jnp.concatenate([xp_ref[...] * keep_prev, x_ref[...], xn_ref[...] * keep_next], axis=0)
    ne = tm + 2 * SUBLANE
    xc = cb_ref[...] + cw_ref[2:3, :] * x_ref[...]
    for tap, off in ((0, -2), (1, -1), (3, 1)):
        shifted = pltpu.roll(xe, (-off) % ne, axis=0)[SUBLANE:SUBLANE + tm, :]
        xc = xc + cw_ref[tap:tap + 1, :] * shifted
    for d in range(2):
        lam = lam_ref[d:d + 1, :]
        sp = jnp.maximum(-lam, 0.0) + jnp.log(1.0 + jnp.exp(-jnp.abs(lam)))
        half_rate = (-0.5 * LRU_C * LOG2E) * sp
        for g in range(LRU_BLOCKS):
            sl = slice(g * LRU_BLOCK, (g + 1) * LRU_BLOCK)
            xg = xc[:, sl]
            xg16 = xg.astype(BF16)
            t_r = jnp.tanh(jnp.dot(xg16, wa_ref[d, g], preferred_element_type=F32) + ba_ref[d:d + 1, sl])
            t_i = jnp.tanh(jnp.dot(xg16, wx_ref[d, g], preferred_element_type=F32) + bx_ref[d:d + 1, sl])
            a = jnp.exp2(t_r * half_rate[:, sl] + half_rate[:, sl])
            half_x = 0.5 * xg
            a_ref[d, :, sl] = a
            y = 1.0 - a * a
            root = jnp.where(y > 0.0, y * lax.rsqrt(y), 0.0)
            b_ref[d, :, sl] = root * (t_i * half_x + half_x)


def _lru_coef(p, conv_w, conv_b, wa, wx, ba, bx, lam, rows, *, tm):
    n = p.shape[0]
    W = LRU_WIDTH
    tpb = tm // SUBLANE
    n8 = n // SUBLANE
    kern = functools.partial(_lru_coef_kernel, tm=tm, ctx_tiles=rows.n_ctx // tm,
                             ctx_seg=rows.Lc // tm, lat_seg=rows.S // tm)
    full = lambda *s: pl.BlockSpec(s, lambda i: (0,) * len(s))
    return pl.pallas_call(
        kern,
        grid=(n // tm,),
        in_specs=[pl.BlockSpec((tm, W), lambda i: (i, CD_XB // W)),
                  pl.BlockSpec((SUBLANE, W), lambda i: (jnp.maximum(i * tpb - 1, 0), CD_XB // W)),
                  pl.BlockSpec((SUBLANE, W), lambda i: (jnp.minimum((i + 1) * tpb, n8 - 1), CD_XB // W)),
                  full(4, W), full(1, W), full(2, LRU_BLOCKS, LRU_BLOCK, LRU_BLOCK),
                  full(2, LRU_BLOCKS, LRU_BLOCK, LRU_BLOCK), full(2, W), full(2, W), full(2, W)],
        out_specs=[pl.BlockSpec((2, tm, W), lambda i: (0, i, 0))] * 2,
        out_shape=[jax.ShapeDtypeStruct((2, n, W), F32)] * 2,
        compiler_params=_params(("arbitrary",)),
        name="lru_coef",
    )(p, p, p, conv_w, conv_b.reshape(1, W), wa, wx, ba, bx, lam)


def _lru_scan_kernel(af_ref, bf_ref, ab_ref, bb_ref, of_ref, ob_ref, h_scr, *, tb, lw):
    @pl.when(pl.program_id(1) == 0)
    def _():
        h_scr[...] = jnp.zeros_like(h_scr)

    row = lax.broadcasted_iota(jnp.int32, (SUBLANE, lw), 0)
    ng = tb // SUBLANE

    def scan8(a, b, rev):
        d = 1
        while d < SUBLANE:
            if rev:
                keep = row < SUBLANE - d
                sh = SUBLANE - d
            else:
                keep = row >= d
                sh = d
            b = a * jnp.where(keep, pltpu.roll(b, sh, axis=0), 0.0) + b
            a = a * jnp.where(keep, pltpu.roll(a, sh, axis=0), 1.0)
            d *= 2
        return a, b

    for c in range(LRU_WIDTH // lw):
        cs = slice(c * lw, (c + 1) * lw)

        def body(g, carry):
            hf, hb = carry
            rf = pl.multiple_of(g * SUBLANE, SUBLANE)
            a, b = scan8(af_ref[pl.ds(rf, SUBLANE), cs], bf_ref[pl.ds(rf, SUBLANE), cs], False)
            out = a * hf + b
            of_ref[pl.ds(rf, SUBLANE), cs] = out
            hf = jnp.broadcast_to(out[SUBLANE - 1:SUBLANE, :], (SUBLANE, lw))
            rb = pl.multiple_of((ng - 1 - g) * SUBLANE, SUBLANE)
            a, b = scan8(ab_ref[pl.ds(rb, SUBLANE), cs], bb_ref[pl.ds(rb, SUBLANE), cs], True)
            out = a * hb + b
            ob_ref[pl.ds(rb, SUBLANE), cs] = out
            hb = jnp.broadcast_to(out[0:1, :], (SUBLANE, lw))
            return hf, hb

        hf, hb = lax.fori_loop(0, ng, body, (h_scr[0, :, cs], h_scr[1, :, cs]), unroll=4)
        h_scr[0, :, cs] = hf
        h_scr[1, :, cs] = hb


def _lru_scan(a, b, rows):
    n = a.shape[1]
    W = LRU_WIDTH
    B = rows.B
    tb = rows.Lc
    nlb = rows.S // tb

    def blk_f(b_, j):
        return jnp.where(j == 0, b_, B + b_ * nlb + (j - 1))

    def blk_b(b_, j):
        return jnp.where(j == 0, b_, B + b_ * nlb + (nlb - j))

    def spec(d, blk):
        return pl.BlockSpec((None, tb, W), lambda b_, j: (d, blk(b_, j), 0))

    kern = functools.partial(_lru_scan_kernel, tb=tb, lw=512)
    return pl.pallas_call(
        kern,
        grid=(B, 1 + nlb),
        in_specs=[spec(0, blk_f), spec(0, blk_f), spec(1, blk_b), spec(1, blk_b)],
        out_specs=[pl.BlockSpec((tb, W), lambda b_, j: (blk_f(b_, j), 0)),
                   pl.BlockSpec((tb, W), lambda b_, j: (blk_b(b_, j), 0))],
        out_shape=[jax.ShapeDtypeStruct((n, W), F32)] * 2,
        scratch_shapes=[pltpu.VMEM((2, SUBLANE, W), F32)],
        compiler_params=_params(("arbitrary", "arbitrary")),
        name="lru_scan",
    )(a, b, a, b)


def _mla_qkv_kernel(cq_ref, ckv_ref, kr_ref, qg_ref, kvg_ref, wq_ref, wk_ref, wv_ref, qkg_ref, cos_ref, sin_ref,
                    q_ref, k_ref, v_ref):
    cqn = _rms(cq_ref[...], qg_ref[...]).astype(BF16)
    ckvn = _rms(ckv_ref[...], kvg_ref[...]).astype(BF16)
    q_all = jnp.dot(cqn, wq_ref[...], preferred_element_type=F32)
    kn_all = jnp.dot(ckvn, wk_ref[...], preferred_element_type=F32)
    v_all = jnp.dot(ckvn, wv_ref[...], preferred_element_type=F32).astype(BF16)
    ones = jnp.ones((v_all.shape[0], LANE), BF16)
    for h in range(MLA_HEADS):
        v_ref[:, h * (MLA_V + LANE):(h + 1) * (MLA_V + LANE)] = jnp.concatenate(
            [v_all[:, h * MLA_V:(h + 1) * MLA_V], ones], axis=1)
    kr = kr_ref[...]
    cos = cos_ref[...]
    sin = sin_ref[...]
    lane = lax.broadcasted_iota(jnp.int32, cos.shape, 1)
    first_half = (lane % (MLA_ROPE // 2)) < (MLA_ROPE // 4)
    hi_w = MLA_PAD - MLA_NOPE

    def rope(hi):
        rot = jnp.where(first_half, pltpu.roll(hi, hi_w - MLA_ROPE // 4, axis=1),
                        pltpu.roll(hi, MLA_ROPE // 4, axis=1))
        return hi * cos + rot * sin

    def inv_rms(lo, hi):
        return lax.rsqrt(jnp.sum(lo * lo + hi * hi, axis=-1, keepdims=True) * (1.0 / MLA_QK) + EPS)

    g_q, g_k = qkg_ref[0:1, :], qkg_ref[1:2, :]
    kr_rot = rope(kr * g_k[:, MLA_NOPE:])
    for h in range(MLA_HEADS):
        lo_sl = slice(h * MLA_PAD, h * MLA_PAD + MLA_NOPE)
        hi_sl = slice(h * MLA_PAD + MLA_NOPE, (h + 1) * MLA_PAD)
        q_lo, q_hi = q_all[:, lo_sl], q_all[:, hi_sl]
        r = inv_rms(q_lo, q_hi) * (MLA_QK ** -0.5 * LOG2E)
        q_ref[:, lo_sl] = (q_lo * r * g_q[:, :MLA_NOPE]).astype(BF16)
        q_ref[:, hi_sl] = rope(q_hi * r * g_q[:, MLA_NOPE:]).astype(BF16)
        k_lo = kn_all[:, h * MLA_NOPE:(h + 1) * MLA_NOPE]
        r = inv_rms(k_lo, kr)
        k_ref[:, lo_sl] = (k_lo * r * g_k[:, :MLA_NOPE]).astype(BF16)
        k_ref[:, hi_sl] = (kr_rot * r).astype(BF16)


def _mla_qkv(p, q_norm_g, kv_norm_g, wq, wk, wv, qk_g, cos_t, sin_t, rows, *, tm):
    B, Lc, S = rows.B, rows.Lc, rows.S
    n = p.shape[0]
    nct = rows.n_ctx // tm
    cpb = Lc // tm
    lpb = S // tm

    def seq_blk(i):
        il = jnp.maximum(i - nct, 0)
        return (jnp.where(i < nct, i // cpb, il // lpb), jnp.where(i < nct, lpb + i % cpb, il % lpb), 0)

    def pos_blk(i):
        return (jnp.where(i < nct, i % cpb, cpb + jnp.maximum(i - nct, 0) % lpb), 0)

    full = lambda *s: pl.BlockSpec(s, lambda i: (0,) * len(s))
    qkw = MLA_HEADS * MLA_PAD
    vw = MLA_HEADS * MLA_V
    return pl.pallas_call(
        _mla_qkv_kernel,
        grid=(n // tm,),
        in_specs=[pl.BlockSpec((tm, MLA_Q_RANK), lambda i: (i, CD_CQ // MLA_Q_RANK)),
                  pl.BlockSpec((tm, MLA_KV_RANK), lambda i: (i, CD_CKV // MLA_KV_RANK)),
                  pl.BlockSpec((tm, LANE), lambda i: (i, CD_KR // LANE)),
                  full(1, MLA_Q_RANK), full(1, MLA_KV_RANK), full(MLA_Q_RANK, qkw),
                  full(MLA_KV_RANK, MLA_HEADS * MLA_NOPE), full(MLA_KV_RANK, vw), full(2, MLA_PAD),
                  pl.BlockSpec((tm, MLA_PAD - MLA_NOPE), pos_blk),
                  pl.BlockSpec((tm, MLA_PAD - MLA_NOPE), pos_blk)],
        out_specs=[pl.BlockSpec((None, tm, qkw), seq_blk), pl.BlockSpec((None, tm, qkw), seq_blk),
                   pl.BlockSpec((None, tm, vw + MLA_HEADS * LANE), seq_blk)],
        out_shape=[jax.ShapeDtypeStruct((B, Lc + S, qkw), BF16), jax.ShapeDtypeStruct((B, Lc + S, qkw), BF16),
                   jax.ShapeDtypeStruct((B, Lc + S, vw + MLA_HEADS * LANE), BF16)],
        compiler_params=_params(("arbitrary",)),
        name="mla_qkv",
    )(p, p, p, q_norm_g.reshape(1, -1), kv_norm_g.reshape(1, -1), wq, wk, wv, qk_g, cos_t, sin_t)


def _attn_kernel(q_ref, k_ref, v_ref, o_ref, *, sub):
    nt = (((1,), (1,)), ((), ()))
    groups = list(range(0, q_ref.shape[0], sub))

    def scores(r):
        return lax.dot_general(q_ref[r:r + sub, :], k_ref[...], nt, preferred_element_type=F32)

    s_next = scores(groups[0])
    for g, r in enumerate(groups):
        s = s_next
        if g + 1 < len(groups):
            s_next = scores(groups[g + 1])
        e = jnp.exp2(s - jnp.max(s, axis=-1, keepdims=True)).astype(BF16)
        ov = jnp.dot(e, v_ref[...], preferred_element_type=F32)
        o_ref[r:r + sub, :] = (ov[:, :MLA_V] / ov[:, MLA_V:MLA_V + 1]).astype(BF16)


def _attention(q, k, v, rows, *, tq):
    B, Lc, S = rows.B, rows.Lc, rows.S
    nq = S // tq
    return pl.pallas_call(
        functools.partial(_attn_kernel, sub=min(tq, 256)),
        grid=(B, MLA_HEADS, nq),
        in_specs=[pl.BlockSpec((None, tq, MLA_PAD), lambda b, h, i: (b, i, h)),
                  pl.BlockSpec((None, Lc + S, MLA_PAD), lambda b, h, i: (b, 0, h)),
                  pl.BlockSpec((None, Lc + S, MLA_V + LANE), lambda b, h, i: (b, 0, h))],
        out_specs=pl.BlockSpec((tq, MLA_V), lambda b, h, i: (b * nq + i, h)),
        out_shape=jax.ShapeDtypeStruct((B * S, MLA_HEADS * MLA_V), BF16),
        compiler_params=_params(("arbitrary", "arbitrary", "arbitrary")),
        name="attention",
    )(q, k, v)


def _gelu_tanh(x):
    return 0.5 * x * (1.0 + jnp.tanh(math.sqrt(2.0 / math.pi) * (x + 0.044715 * (x * x * x))))


def _merge_cd_kernel(x_ref, mod_ref, yb_ref, hf_ref, hb_ref, att_ref, w1_ref, w2_ref, o_ref):
    y1 = _gelu_tanh(yb_ref[...]) * (hf_ref[...] + hb_ref[...])
    y = (jnp.dot(y1.astype(BF16), w1_ref[...], preferred_element_type=F32)
         + jnp.dot(att_ref[...], w2_ref[...], preferred_element_type=F32))
    o_ref[...] = x_ref[...] + mod_ref[5:6, :] * y


def _merge_cd(x, mods, p, h_f, h_b, att, w_out, rows, *, tm):
    D = x.shape[1]
    W = LRU_WIDTH
    aw = MLA_HEADS * MLA_V
    off = rows.n_ctx // tm
    lat = lambda i: (i + off, 0)
    return pl.pallas_call(
        _merge_cd_kernel,
        grid=(rows.n_lat // tm,),
        in_specs=[pl.BlockSpec((tm, D), lat),
                  pl.BlockSpec((None, N_MOD, D), lambda i: (rows.mod_row(i, tm, False), 0, 0)),
                  pl.BlockSpec((tm, W), lambda i: (i + off, CD_YB // W)),
                  pl.BlockSpec((tm, W), lat), pl.BlockSpec((tm, W), lat),
                  pl.BlockSpec((tm, aw), lambda i: (i, 0)),
                  pl.BlockSpec((W, D), lambda i: (0, 0), pipeline_mode=pl.Buffered(1)),
                  pl.BlockSpec((aw, D), lambda i: (W // aw, 0), pipeline_mode=pl.Buffered(1))],
        out_specs=pl.BlockSpec((tm, D), lambda i: (i, 0)),
        out_shape=jax.ShapeDtypeStruct((rows.n_lat, D), F32),
        compiler_params=_params(("arbitrary",)),
        name="merge_cd",
    )(x, mods, p, h_f, h_b, att, w_out, w_out)


def _ret_tables(rows):
    half = RET_DK // 2
    freqs = ROPE_BASE ** (-jnp.arange(half, dtype=F32) / half)
    ang = jnp.arange(rows.S, dtype=F32)[:, None] * freqs
    cos = jnp.concatenate([jnp.cos(ang), jnp.cos(ang)], axis=-1)
    sin = jnp.concatenate([-jnp.sin(ang), jnp.sin(ang)], axis=-1)
    cos = jnp.concatenate([jnp.ones((rows.Lc, RET_DK), F32), cos], axis=0)
    sin = jnp.concatenate([jnp.zeros((rows.Lc, RET_DK), F32), sin], axis=0)
    return cos, sin


def _mla_tables(rows):
    S = rows.S
    quarter = MLA_ROPE // 4
    freqs = ROPE_BASE ** (-jnp.arange(quarter, dtype=F32) / quarter)
    t = jnp.arange(S)
    row = (t // GRID_W).astype(F32)
    col = (t % GRID_W).astype(F32)

    def part(pos):
        ang = pos[:, None] * freqs
        return (jnp.concatenate([jnp.cos(ang), jnp.cos(ang)], axis=-1),
                jnp.concatenate([-jnp.sin(ang), jnp.sin(ang)], axis=-1))

    rc, rs = part(row)
    cc, cs = part(col)
    tail = MLA_PAD - MLA_QK
    cos = jnp.concatenate([rc, cc, jnp.ones((S, tail), F32)], axis=-1)
    sin = jnp.concatenate([rs, cs, jnp.zeros((S, tail), F32)], axis=-1)
    cos = jnp.concatenate([jnp.ones((rows.Lc, MLA_ROPE + tail), F32), cos], axis=0)
    sin = jnp.concatenate([jnp.zeros((rows.Lc, MLA_ROPE + tail), F32), sin], axis=0)
    return cos, sin


def _pad_cols(w, n):
    return jnp.pad(w, ((0, 0), (0, n - w.shape[1])))


def _head_pad(w, heads, width, padded):
    K = w.shape[0]
    return jnp.pad(w.reshape(K, heads, width), ((0, 0), (0, 0), (0, padded - width))).reshape(K, heads * padded)


def kernel(x, c, ctx, c_ctx, ada_w, ada_b, norm_g, ffn_wg, ffn_wu, ffn_wd, ab_w_in, ab_w_out, ret_decay_logit, ret_gn_g, mlstm_gate_b, mlstm_gn_g, cd_w_in, cd_w_out, lru_conv_w, lru_conv_b, lru_wa, lru_ba, lru_wx, lru_bx, lru_lambda, mla_q_norm_g, mla_kv_norm_g, mla_w_uq, mla_w_uk, mla_w_uv, mla_qk_norm_g):
    B, S, D = x.shape
    Lc = ctx.shape[1]
    depth = ada_w.shape[0]
    F = ffn_wg.shape[-1]
    rows = Rows(B, Lc, S)
    assert B < SUBLANE and Lc % CHUNK == 0 and S % Lc == 0 and S % GRID_W == 0

    tm = min(1024, rows.n_ctx)
    tf = 512
    all_mod = lambda i: rows.mod_row(i, tm, True)
    lat_mod = lambda i: rows.mod_row(i, tm, False)

    cond = jnp.zeros((SUBLANE, D), F32).at[:B].set(c.astype(F32)).at[B].set(c_ctx.astype(F32))
    mods = _adaln(cond, ada_w, ada_b)

    ffn_stacks = (ffn_wg, ffn_wu, ffn_wd)
    w_cur = tuple(w[0, 0].astype(BF16) for w in ffn_stacks)

    def ffn_casts(l, h):
        return [(w, (l, h)) for w in ffn_stacks] if l < depth else []

    for l in range(depth):
        last = l == depth - 1
        j = l // 2
        m_l = mods[l]
        ffn1 = functools.partial(_ffn, mods=m_l, g=norm_g[l, 0], wg=w_cur[0], wu=w_cur[1], wd=w_cur[2],
                                 mod_base=0, tm=tm, tf=tf)
        casts = ffn_casts(l, 1)
        if l == 0:
            xa, w_next = ffn1(x.reshape(B * S, D).astype(F32), mod_of_tile=lat_mod, out_rows=rows.n_all,
                              out_tile0=rows.n_ctx // tm, casts=casts)
            xa = ffn1(ctx.reshape(B * Lc, D).astype(F32), mod_of_tile=lambda i: B, out_rows=rows.n_all, into=xa)
        else:
            xa, w_next = ffn1(xa, mod_of_tile=all_mod, casts=casts)
        w_cur = w_next

        if l % 2 == 0:
            w_in, w_out = ab_w_in[j].astype(BF16), ab_w_out[j].astype(BF16)
            p, p_gates = _proj(xa, m_l, norm_g[l, 1], w_in, rows, tm=tm, tn=1024, n_cols=AB_GATES,
                               w_tail=_pad_cols(w_in[:, AB_GATES:], LANE))
            cos_t, sin_t = _ret_tables(rows)
            ret_f, ret_b, ml_f, ml_b = _mixer_ab(p, p_gates, ret_decay_logit[j], mlstm_gate_b[j], cos_t, sin_t,
                                                 rows)
            xa = _merge_ab(xa, m_l, ret_f, ret_b, ml_f, ml_b, p, ret_gn_g[j], mlstm_gn_g[j], w_out, rows,
                           tm=min(512, tm))
            if last:
                xa = xa[rows.n_ctx:]
        else:
            assert last, "context outputs of the recurrent/attention mixer are not produced"
            w_in, w_out, w_uq, w_uk, w_uv = (w[j].astype(BF16) for w in
                                             (cd_w_in, cd_w_out, mla_w_uq, mla_w_uk, mla_w_uv))
            p, = _proj(xa, m_l, norm_g[l, 1], _pad_cols(w_in, CD_PAD), rows, tm=tm, tn=CD_PAD // 3)
            a_c, b_c = _lru_coef(p, lru_conv_w[j], lru_conv_b[j], (0.5 * lru_wa[j]).astype(BF16),
                                 (0.5 * lru_wx[j]).astype(BF16), 0.5 * lru_ba[j], 0.5 * lru_bx[j], lru_lambda[j],
                                 rows, tm=min(256, Lc))
            h_f, h_b = _lru_scan(a_c, b_c, rows)
            cos_t, sin_t = _mla_tables(rows)
            qk_g = jnp.pad(mla_qk_norm_g[j].astype(F32), ((0, 0), (0, MLA_PAD - MLA_QK)))
            q, k, v = _mla_qkv(p, mla_q_norm_g[j], mla_kv_norm_g[j], _head_pad(w_uq, MLA_HEADS, MLA_QK, MLA_PAD),
                               w_uk, w_uv, qk_g, cos_t, sin_t, rows, tm=min(256, Lc))
            att = _attention(q, k, v, rows, tq=min(2048, S))
            xa = _merge_cd(xa, m_l, p, h_f, h_b, att, w_out, rows, tm=min(512, tm))

        casts = ffn_casts(l + 1, 0)
        res = _ffn(xa, m_l, norm_g[l, 2], *w_cur, mod_base=6, mod_of_tile=lat_mod if last else all_mod,
                   tm=tm, tf=tf, casts=casts)
        xa, w_cur = res if casts else (res, None)

    if xa.shape[0] != rows.n_lat:
        xa = xa[rows.n_ctx:]
    return xa.reshape(B, S, D).astype(x.dtype)
```
